```python
import math
import jax, jax.numpy as jnp
from jax import lax
import numpy as np

D_MODEL = 1024
BATCH = 8
SEQ = 4096
DEPTH = 2

N_META = 16
GRID_W = 64
HEAD_DIM = 64
EPS = 1e-6

SSD_HEADS = 4
SSD_HEAD_DIM = 64
SSD_INNER = SSD_HEADS * SSD_HEAD_DIM
SSD_GROUPS = 2
SSD_STATE = 128
SSD_CONV = 5
SSD_CHUNK = 128
SSD_CONV_DIM = SSD_INNER + 2 * SSD_GROUPS * SSD_STATE

GLA_HEADS = 4
GLA_DK = 32
GLA_DV = 64
GLA_KEY = GLA_HEADS * GLA_DK
GLA_VAL = GLA_HEADS * GLA_DV
GLA_GATE_RANK = 16
GLA_GATE_NORM = 16.0
GLA_CHUNK = 64

SWA_Q_HEADS = 4
SWA_KV_HEADS = 2
WINDOW = 128
ATT_BLOCK = 128

G2_Q_HEADS = 4
G2_KV_HEADS = 2
ROPE_THETA = 10000.0

REL_BUCKETS = 32
REL_MAX_DIST = 128

MIX_WIDTH = SSD_INNER + GLA_VAL + SWA_Q_HEADS * HEAD_DIM + G2_Q_HEADS * HEAD_DIM

FFN_DENSE = 2816
N_EXPERTS = 8
TOP_K = 2
FFN_EXPERT = 3584

IN_SIZES = (
    SSD_INNER, SSD_CONV_DIM, 2 * SSD_HEADS,
    GLA_KEY, GLA_KEY, GLA_VAL, GLA_VAL, 2 * GLA_GATE_RANK,
    SWA_Q_HEADS * HEAD_DIM, SWA_KV_HEADS * HEAD_DIM, SWA_KV_HEADS * HEAD_DIM,
    G2_Q_HEADS * HEAD_DIM, G2_KV_HEADS * HEAD_DIM, G2_KV_HEADS * HEAD_DIM,
)
IN_WIDTH = sum(IN_SIZES)

kernel_name = 'hybrid_parallel_heads_encoder'


def _rmsnorm(x, w):
    xf = x.astype(jnp.float32)
    xf = xf * lax.rsqrt(jnp.mean(jnp.square(xf), axis=-1, keepdims=True) + EPS)
    return (xf * w.astype(jnp.float32)).astype(x.dtype)


def _pad_front(t, n):
    return jnp.pad(t, [(0, 0), (n, 0)] + [(0, 0)] * (t.ndim - 2))


def _flip(t):
    return jnp.flip(t, axis=1)


def _scan_states(decay, states):
    def step(s, inp):
        d, st = inp
        return d * s + st, s
    s0 = jnp.zeros_like(states[:, 0])
    _, s_in = lax.scan(step, s0, (jnp.moveaxis(decay, 1, 0), jnp.moveaxis(states, 1, 0)))
    return jnp.moveaxis(s_in, 0, 1)


def _depthwise_conv_centred(u, w, b):
    k = w.shape[0]
    out = lax.conv_general_dilated(u, w[:, None, :].astype(u.dtype), window_strides=(1,),
                                   padding=[((k - 1) // 2, k // 2)],
                                   dimension_numbers=('NWC', 'WIO', 'NWC'),
                                   feature_group_count=u.shape[-1])
    return out + b.astype(u.dtype)


def _ssd_chunked(x, dt, a, b, c):
    bsz, t, h, p = x.shape
    q = SSD_CHUNK
    nc = t // q
    xd = (x * dt[..., None]).reshape(bsz, nc, q, h, p)
    a_cs = jnp.cumsum((dt * a).reshape(bsz, nc, q, h), axis=2)
    b = b.reshape(bsz, nc, q, h, -1)
    c = c.reshape(bsz, nc, q, h, -1)
    causal = jnp.tril(jnp.ones((q, q), bool))[..., None]
    seg = a_cs[:, :, :, None, :] - a_cs[:, :, None, :, :]
    decay = jnp.exp(jnp.where(causal, seg, -jnp.inf))
    scores = jnp.einsum('bclhn,bcshn->bclsh', c, b) * decay
    y_diag = jnp.einsum('bclsh,bcshp->bclhp', scores, xd)
    decay_to_end = jnp.exp(a_cs[:, :, -1:, :] - a_cs)
    chunk_states = jnp.einsum('bclhn,bclhp->bchpn', b * decay_to_end[..., None], xd)
    s_in = _scan_states(jnp.exp(a_cs[:, :, -1, :])[..., None, None], chunk_states)
    y_off = jnp.einsum('bclhn,bchpn->bclhp', c, s_in) * jnp.exp(a_cs)[..., None]
    return (y_diag + y_off).reshape(bsz, t, h, p)


def _ssd_mixer(z, xbc, dt_raw, conv_w, conv_b, dt_bias, a_log, d_skip, norm_w):
    f32 = jnp.float32
    bsz, L, _ = xbc.shape
    xbc = jax.nn.silu(_depthwise_conv_centred(xbc, conv_w, conv_b)).astype(f32)
    xs, bs, cs = jnp.split(xbc, [SSD_INNER, SSD_INNER + SSD_GROUPS * SSD_STATE], axis=-1)
    xs = xs.reshape(bsz, L, SSD_HEADS, SSD_HEAD_DIM)
    rep = SSD_HEADS // SSD_GROUPS
    bs = jnp.repeat(bs.reshape(bsz, L, SSD_GROUPS, SSD_STATE), rep, axis=2)
    cs = jnp.repeat(cs.reshape(bsz, L, SSD_GROUPS, SSD_STATE), rep, axis=2)
    dt = jax.nn.softplus(dt_raw.astype(f32).reshape(bsz, L, 2, SSD_HEADS) + dt_bias.astype(f32))
    a = -jnp.exp(a_log.astype(f32))
    pad = (-L) % SSD_CHUNK
    xp, bp, cp, dtp = [_pad_front(t, pad) for t in (xs, bs, cs, dt)]
    y_f = _ssd_chunked(xp, dtp[:, :, 0], a[0], bp, cp)
    y_b = _flip(_ssd_chunked(_flip(xp), _flip(dtp[:, :, 1]), a[1], _flip(bp), _flip(cp)))
    y = (y_f + y_b)[:, pad:] + d_skip.astype(f32)[:, None] * xs
    y = y.reshape(bsz, L, SSD_INNER) * jax.nn.silu(z.astype(f32))
    return _rmsnorm(y, norm_w).astype(z.dtype)


def _gla_chunked(q, k, v, g):
    bsz, t, h, dk = q.shape
    dv = v.shape[-1]
    cl = GLA_CHUNK
    n = t // cl
    q, k, g = [u.reshape(bsz, n, cl, h, dk) for u in (q, k, g)]
    v = v.reshape(bsz, n, cl, h, dv)
    bcum = jnp.cumsum(g, axis=2)
    q_t = q * jnp.exp(bcum)
    k_t = k * jnp.exp(-bcum)
    att = jnp.einsum('bclhk,bcshk->bchls', q_t, k_t)
    att = jnp.where(jnp.tril(jnp.ones((cl, cl), bool)), att, 0.0)
    o_intra = jnp.einsum('bchls,bcshv->bclhv', att, v)
    b_last = bcum[:, :, -1]
    chunk_states = jnp.einsum('bclhk,bclhv->bchkv', k * jnp.exp(b_last[:, :, None] - bcum), v)
    s_in = _scan_states(jnp.exp(b_last)[..., None], chunk_states)
    o_inter = jnp.einsum('bclhk,bchkv->bclhv', q_t, s_in)
    return (o_intra + o_inter).reshape(bsz, t, h, dv)


def _gla_mixer(q, k, v, r, a, gate_w2, gate_b, norm_w):
    f32 = jnp.float32
    bsz, L, _ = q.shape
    q = q.astype(f32).reshape(bsz, L, GLA_HEADS, GLA_DK) * GLA_DK ** -0.5
    k = k.astype(f32).reshape(bsz, L, GLA_HEADS, GLA_DK)
    v = v.astype(f32).reshape(bsz, L, GLA_HEADS, GLA_DV)
    a = a.astype(f32).reshape(bsz, L, 2, GLA_GATE_RANK)
    g = jax.nn.log_sigmoid(jnp.einsum('bldr,drk->bldk', a, gate_w2.astype(f32)) + gate_b.astype(f32)) / GLA_GATE_NORM
    g = g.reshape(bsz, L, 2, GLA_HEADS, GLA_DK)
    pad = (-L) % GLA_CHUNK
    qp, kp, vp, gp = [_pad_front(t, pad) for t in (q, k, v, g)]
    o_f = _gla_chunked(qp, kp, vp, gp[:, :, 0])
    o_b = _flip(_gla_chunked(_flip(qp), _flip(kp), _flip(vp), _flip(gp[:, :, 1])))
    o = _rmsnorm((o_f + o_b)[:, pad:], norm_w)
    o = o.reshape(bsz, L, GLA_VAL) * jax.nn.silu(r.astype(f32))
    return o.astype(r.dtype)


def _t5_bucket(rel):
    nb = REL_BUCKETS // 2
    max_exact = nb // 2
    ret = (rel > 0).astype(jnp.int32) * nb
    n = jnp.abs(rel)
    nf = jnp.maximum(n, 1).astype(jnp.float32)
    large = max_exact + (jnp.log(nf / max_exact) / math.log(REL_MAX_DIST / max_exact)
                         * (nb - max_exact)).astype(jnp.int32)
    large = jnp.minimum(large, nb - 1)
    return ret + jnp.where(n < max_exact, n, large)


def _swa_mixer(q, k, v, sink, rel_bias):
    f32 = jnp.float32
    bsz, L, _ = q.shape
    S = L - N_META
    blk = ATT_BLOCK
    nb = S // blk
    R = SWA_Q_HEADS // SWA_KV_HEADS
    G = SWA_KV_HEADS
    q = q.reshape(bsz, L, G, R, HEAD_DIM) * HEAD_DIM ** -0.5
    k = k.reshape(bsz, L, G, HEAD_DIM)
    v = v.reshape(bsz, L, G, HEAD_DIM)
    qm, qr = q[:, :N_META], q[:, N_META:]
    km, kr = k[:, :N_META], k[:, N_META:]
    vm, vr = v[:, :N_META], v[:, N_META:]

    def band(t):
        tp = jnp.pad(t, [(0, 0), (blk, blk), (0, 0), (0, 0)]).reshape(bsz, nb + 2, blk, G, HEAD_DIM)
        return jnp.concatenate([tp[:, :-2], tp[:, 1:-1], tp[:, 2:]], axis=2)

    def head_bias(bucket):
        bb = jnp.moveaxis(rel_bias[bucket], -1, -3).astype(f32)
        return bb.reshape(bb.shape[:-3] + (G, R) + bb.shape[-2:])

    sink_l = sink.astype(f32).reshape(G, R, 1, 1)
    kb, vb = band(kr), band(vr)
    qb = qr.reshape(bsz, nb, blk, G, R, HEAD_DIM)
    qi = jnp.arange(blk)
    ki = jnp.arange(3 * blk)
    rel = ki[None, :] - blk - qi[:, None]
    kidx = jnp.arange(nb)[:, None] * blk + ki[None, :] - blk
    mask = (jnp.abs(rel) <= WINDOW)[None] & ((kidx >= 0) & (kidx < S))[:, None, :]
    qpos = N_META + jnp.arange(S).reshape(nb, blk)
    rel_m = jnp.arange(N_META)[None, None, :] - qpos[:, :, None]
    s_band = jnp.einsum('bnqgrd,bnkgd->bngrqk', qb, kb).astype(f32) + head_bias(_t5_bucket(rel))
    s_band = jnp.where(mask[None, :, None, None], s_band, -jnp.inf)
    s_meta = jnp.einsum('bnqgrd,bmgd->bngrqm', qb, km).astype(f32) + head_bias(_t5_bucket(rel_m))
    s_sink = jnp.broadcast_to(sink_l, s_band.shape[:-1] + (1,))
    p = jax.nn.softmax(jnp.concatenate([s_band, s_meta, s_sink], axis=-1), axis=-1).astype(v.dtype)
    o_r = (jnp.einsum('bngrqk,bnkgd->bnqgrd', p[..., :3 * blk], vb)
           + jnp.einsum('bngrqm,bmgd->bnqgrd', p[..., 3 * blk:3 * blk + N_META], vm))
    o_r = o_r.reshape(bsz, S, SWA_Q_HEADS * HEAD_DIM)

    kc = jnp.concatenate([km, kr[:, :blk]], axis=1)
    vc = jnp.concatenate([vm, vr[:, :blk]], axis=1)
    rel_q = jnp.arange(N_META + blk)[None, :] - jnp.arange(N_META)[:, None]
    s_q = jnp.einsum('bqgrd,bkgd->bgrqk', qm, kc).astype(f32) + head_bias(_t5_bucket(rel_q))
    s_q = jnp.where(jnp.abs(rel_q) <= WINDOW, s_q, -jnp.inf)
    s_q_sink = jnp.broadcast_to(sink_l, s_q.shape[:-1] + (1,))
    p_q = jax.nn.softmax(jnp.concatenate([s_q, s_q_sink], axis=-1), axis=-1).astype(v.dtype)
    o_m = jnp.einsum('bgrqk,bkgd->bqgrd', p_q[..., :-1], vc).reshape(bsz, N_META, SWA_Q_HEADS * HEAD_DIM)
    return jnp.concatenate([o_m, o_r], axis=1)


def _axial_rope(L):
    f32 = jnp.float32
    n_tok = L - N_META
    rows = n_tok // GRID_W
    t = jnp.arange(rows * GRID_W)
    meta_pos = jnp.arange(N_META) - N_META
    row = jnp.concatenate([meta_pos, t // GRID_W]).astype(f32)
    col = jnp.concatenate([meta_pos, t % GRID_W]).astype(f32)
    half = HEAD_DIM // 2
    inv = ROPE_THETA ** (-jnp.arange(0, half, 2, dtype=f32) / half)
    ang = jnp.concatenate([row[:, None] * inv, col[:, None] * inv], axis=-1)
    return jnp.cos(ang), jnp.sin(ang)


def _apply_rope(x, cos, sin):
    shp = x.shape
    xf = x.astype(jnp.float32).reshape(shp[:-1] + (HEAD_DIM // 2, 2))
    bshape = (1, shp[1]) + (1,) * (x.ndim - 3) + (HEAD_DIM // 2,)
    c = cos.reshape(bshape)
    s = sin.reshape(bshape)
    x0, x1 = xf[..., 0], xf[..., 1]
    return jnp.stack([x0 * c - x1 * s, x0 * s + x1 * c], axis=-1).reshape(shp).astype(x.dtype)


def _gqa2d_mixer(q, k, v, qn_w, kn_w):
    bsz, L, _ = q.shape
    G = G2_KV_HEADS
    R = G2_Q_HEADS // G2_KV_HEADS
    blk = ATT_BLOCK
    q = _rmsnorm(q.reshape(bsz, L, G, R, HEAD_DIM), qn_w)
    k = _rmsnorm(k.reshape(bsz, L, G, HEAD_DIM), kn_w)
    v = v.reshape(bsz, L, G, HEAD_DIM)
    cos, sin = _axial_rope(L)
    q = _apply_rope(q, cos, sin) * HEAD_DIM ** -0.5
    k = _apply_rope(k, cos, sin)
    pad = (-L) % blk
    nq = (L + pad) // blk
    qp = jnp.moveaxis(_pad_front(q, pad).reshape(bsz, nq, blk, G, R, HEAD_DIM), 1, 0)

    def block(qb):
        s = jnp.einsum('bqgrd,bkgd->bgrqk', qb, k).astype(jnp.float32)
        p = jax.nn.softmax(s, axis=-1).astype(v.dtype)
        return jnp.einsum('bgrqk,bkgd->bqgrd', p, v)

    o = lax.map(block, qp)
    o = jnp.moveaxis(o, 0, 1).reshape(bsz, nq * blk, G2_Q_HEADS * HEAD_DIM)
    return o[:, pad:]


def _swiglu(u, wg, wu, wd):
    return (jax.nn.silu(u @ wg) * (u @ wu)) @ wd


def _moe(u, router, wg, wu, wd):
    bsz, L, d = u.shape
    t = u.reshape(-1, d)
    logits = (t @ router).astype(jnp.float32)
    top_v, top_i = lax.top_k(logits, TOP_K)
    gates = jax.nn.softmax(top_v, axis=-1)
    combine = jnp.sum(jax.nn.one_hot(top_i, N_EXPERTS, dtype=jnp.float32) * gates[..., None], axis=1)
    out = jnp.zeros_like(t)
    for e in range(N_EXPERTS):
        out = out + combine[:, e:e + 1].astype(t.dtype) * _swiglu(t, wg[e], wu[e], wd[e])
    return out.reshape(bsz, L, d)


def setup_inputs(seed: int = 0) -> dict:
    key = jax.random.key(seed)
    ks = iter(jax.random.split(key, 32))
    f32 = jnp.float32
    n_dense = (DEPTH + 1) // 2
    n_moe = DEPTH // 2

    def nrm(shape, scale):
        return jax.random.normal(next(ks), shape, f32) * scale

    def gain(shape):
        return 1.0 + 0.05 * jax.random.normal(next(ks), shape, f32)

    x = nrm((BATCH, SEQ, D_MODEL), 1.0)
    meta_tokens = nrm((N_META, D_MODEL), 1.0)
    rel_bias = nrm((REL_BUCKETS, SWA_Q_HEADS), 0.5)
    norm_mix_w = gain((DEPTH, D_MODEL))
    norm_ffn_w = gain((DEPTH, D_MODEL))
    w_in = nrm((DEPTH, D_MODEL, IN_WIDTH), D_MODEL ** -0.5)
    ssd_conv_w = nrm((DEPTH, SSD_CONV, SSD_CONV_DIM), SSD_CONV ** -0.5)
    ssd_conv_b = nrm((DEPTH, SSD_CONV_DIM), 0.02)
    dt0 = jnp.exp(jax.random.uniform(next(ks), (DEPTH, 2, SSD_HEADS), f32, math.log(1e-3), math.log(1e-1)))
    ssd_dt_bias = dt0 + jnp.log(-jnp.expm1(-dt0))
    ssd_a_log = jnp.log(jax.random.uniform(next(ks), (DEPTH, 2, SSD_HEADS), f32, 1.0, 16.0))
    ssd_d = gain((DEPTH, SSD_HEADS))
    ssd_norm_w = gain((DEPTH, SSD_INNER))
    gla_gate_w2 = nrm((DEPTH, 2, GLA_GATE_RANK, GLA_KEY), GLA_GATE_RANK ** -0.5)
    gla_gate_b = nrm((DEPTH, 2, GLA_KEY), 0.1)
    gla_norm_w = gain((DEPTH, GLA_DV))
    swa_sink = nrm((DEPTH, SWA_Q_HEADS), 0.5)
    gqa_q_norm_w = gain((DEPTH, HEAD_DIM))
    gqa_k_norm_w = gain((DEPTH, HEAD_DIM))
    w_out = nrm((DEPTH, MIX_WIDTH, D_MODEL), MIX_WIDTH ** -0.5)
    ffn_w_gate = nrm((n_dense, D_MODEL, FFN_DENSE), D_MODEL ** -0.5)
    ffn_w_up = nrm((n_dense, D_MODEL, FFN_DENSE), D_MODEL ** -0.5)
    ffn_w_down = nrm((n_dense, FFN_DENSE, D_MODEL), FFN_DENSE ** -0.5)
    moe_router = nrm((n_moe, D_MODEL, N_EXPERTS), D_MODEL ** -0.5)
    moe_w_gate = nrm((n_moe, N_EXPERTS, D_MODEL, FFN_EXPERT), D_MODEL ** -0.5)
    moe_w_up = nrm((n_moe, N_EXPERTS, D_MODEL, FFN_EXPERT), D_MODEL ** -0.5)
    moe_w_down = nrm((n_moe, N_EXPERTS, FFN_EXPERT, D_MODEL), FFN_EXPERT ** -0.5)
    final_norm_w = gain((D_MODEL,))
    return {'x': x, 'meta_tokens': meta_tokens, 'rel_bias': rel_bias,
            'norm_mix_w': norm_mix_w, 'norm_ffn_w': norm_ffn_w, 'w_in': w_in,
            'ssd_conv_w': ssd_conv_w, 'ssd_conv_b': ssd_conv_b, 'ssd_dt_bias': ssd_dt_bias,
            'ssd_a_log': ssd_a_log, 'ssd_d': ssd_d, 'ssd_norm_w': ssd_norm_w,
            'gla_gate_w2': gla_gate_w2, 'gla_gate_b': gla_gate_b, 'gla_norm_w': gla_norm_w,
            'swa_sink': swa_sink, 'gqa_q_norm_w': gqa_q_norm_w, 'gqa_k_norm_w': gqa_k_norm_w,
            'w_out': w_out, 'ffn_w_gate': ffn_w_gate, 'ffn_w_up': ffn_w_up, 'ffn_w_down': ffn_w_down,
            'moe_router': moe_router, 'moe_w_gate': moe_w_gate, 'moe_w_up': moe_w_up,
            'moe_w_down': moe_w_down, 'final_norm_w': final_norm_w}


def reference(x, meta_tokens, rel_bias, norm_mix_w, norm_ffn_w, w_in, ssd_conv_w, ssd_conv_b,
              ssd_dt_bias, ssd_a_log, ssd_d, ssd_norm_w, gla_gate_w2, gla_gate_b, gla_norm_w,
              swa_sink, gqa_q_norm_w, gqa_k_norm_w, w_out, ffn_w_gate, ffn_w_up, ffn_w_down,
              moe_router, moe_w_gate, moe_w_up, moe_w_down, final_norm_w):
    bsz = x.shape[0]
    offs = [int(o) for o in np.cumsum(IN_SIZES)[:-1]]
    meta = jnp.broadcast_to(meta_tokens[None].astype(x.dtype), (bsz, N_META, D_MODEL))
    h = jnp.concatenate([meta, x], axis=1)
    for i in range(DEPTH):
        u = _rmsnorm(h, norm_mix_w[i])
        proj = u @ w_in[i]
        (z, xbc, dt_raw, gq, gk, gv, gr, ga, sq, sk, sv, aq, ak, av) = jnp.split(proj, offs, axis=-1)
        y_ssd = _ssd_mixer(z, xbc, dt_raw, ssd_conv_w[i], ssd_conv_b[i], ssd_dt_bias[i],
                           ssd_a_log[i], ssd_d[i], ssd_norm_w[i])
        y_gla = _gla_mixer(gq, gk, gv, gr, ga, gla_gate_w2[i], gla_gate_b[i], gla_norm_w[i])
        y_swa = _swa_mixer(sq, sk, sv, swa_sink[i], rel_bias)
        y_g2 = _gqa2d_mixer(aq, ak, av, gqa_q_norm_w[i], gqa_k_norm_w[i])
        mixed = jnp.concatenate([y_ssd, y_gla, y_swa, y_g2], axis=-1)
        h = h + mixed @ w_out[i]
        u = _rmsnorm(h, norm_ffn_w[i])
        if i % 2 == 0:
            j = i // 2
            h = h + _swiglu(u, ffn_w_gate[j], ffn_w_up[j], ffn_w_down[j])
        else:
            j = i // 2
            h = h + _moe(u, moe_router[j], moe_w_gate[j], moe_w_up[j], moe_w_down[j])
    return _rmsnorm(h, final_norm_w)[:, N_META:]
```

```python
import functools
import math

import jax
import jax.numpy as jnp
from jax import lax
from jax.experimental import pallas as pl
from jax.experimental.pallas import tpu as pltpu

f32 = jnp.float32
bf16 = jnp.bfloat16

N_META = 16
HEAD_DIM = 64
GRID_W = 64
EPS = 1e-6
ROPE_THETA = 10000.0
TILE = 128
SSD_HEADS = 4
SSD_HEAD_DIM = 64
SSD_INNER = 256
SSD_STATE = 128
SSD_CONV = 5
SSD_CONV_DIM = 768
GLA_HEADS = 4
GLA_DK = 32
GLA_DV = 64
GLA_KEY = 128
GLA_VAL = 256
GLA_GATE_RANK = 16
GLA_GATE_NORM = 16.0
GLA_CHUNK = 64
WINDOW = 128
REL_BUCKETS = 32
REL_MAX_DIST = 128
N_EXPERTS = 8
NEG = -1e30
CONV_HALO = 8
VMEM_LIMIT = 56 * 1024 * 1024

IN_SIZES = (256, 768, 8, 128, 128, 256, 256, 32, 256, 128, 128, 256, 128, 128)
(_Z, _XBC, _DT, _GQ, _GK, _GV, _GR, _GA, _SQ, _SK, _SV, _AQ, _AK, _AV) = range(14)
PACK_ORDER = (_XBC, _Z, _GQ, _GK, _GV, _GR, _SQ, _SK, _SV, _AQ, _AK, _AV, _DT, _GA)
PACK_WIDTH = 2944
C_SSD, C_GLA, C_SWA, C_AQ, C_AK, C_AV, C_SMALL = 0, 1024, 1792, 2304, 2560, 2688, 2816
SMALL_DT, SMALL_GA = 0, 8


def _cparams(*sem):
    return pltpu.CompilerParams(dimension_semantics=sem, vmem_limit_bytes=VMEM_LIMIT)


def _dot(a, b):
    return jnp.dot(a, b, preferred_element_type=f32)


def _dot_nt(a, b):
    return lax.dot_general(a, b, (((1,), (1,)), ((), ())), preferred_element_type=f32)


def _dot_tn(a, b):
    return lax.dot_general(a, b, (((0,), (0,)), ((), ())), preferred_element_type=f32)


def _split(a):
    hi = a.astype(bf16)
    lo = (a - hi.astype(f32)).astype(bf16)
    return hi, lo


def _dot_split_lhs(a, b):
    hi, lo = _split(a)
    return _dot(hi, b) + _dot(lo, b)


def _dot_split_rhs(t, x):
    hi, lo = _split(x)
    return _dot(t, hi) + _dot(t, lo)


def _rms(x, w):
    return x * lax.rsqrt(jnp.mean(x * x, axis=-1, keepdims=True) + EPS) * w


def _silu(x):
    return x / (1.0 + jnp.exp(-x))


def _softplus(x):
    return jnp.maximum(x, 0.0) + jnp.log(1.0 + jnp.exp(-jnp.abs(x)))


def _log_sigmoid(x):
    return jnp.minimum(x, 0.0) - jnp.log(1.0 + jnp.exp(-jnp.abs(x)))


def _tri(n, rev):
    r = lax.broadcasted_iota(jnp.int32, (n, n), 0)
    c = lax.broadcasted_iota(jnp.int32, (n, n), 1)
    return (r <= c) if rev else (r >= c)


def _valid_rows(tile, n):
    rows = lax.broadcasted_iota(jnp.int32, (n, 1), 0)
    return jnp.logical_or(tile > 0, rows >= TILE - N_META)


def _inproj_body(h_ref, nw_ref, w_ref, cos_ref, sin_ref, rot_ref, gavg_ref, qnw_ref, knw_ref,
                 ssd_ref, small_ref, gla_ref, swa_ref, q_ref, k_ref, v_ref):
    u = _rms(h_ref[...], nw_ref[...]).astype(bf16)

    def mm(lo, hi):
        return _dot(u, w_ref[:, lo:hi])

    ssd_ref[...] = mm(C_SSD, C_GLA)
    gla_ref[...] = mm(C_GLA, C_SWA)
    swa_ref[...] = mm(C_SWA, C_AQ).astype(bf16)
    small_ref[...] = mm(C_SMALL, PACK_WIDTH)
    v_ref[...] = mm(C_AV, C_SMALL).astype(bf16)

    def norm_rope(t, w, width):
        ms = _dot_split_lhs(t * t, gavg_ref[:width, :width])
        tn = t * lax.rsqrt(ms + EPS) * w
        tr = _dot(tn.astype(bf16), rot_ref[:width, :width])
        return tn * cos_ref[:, :width] + tr * sin_ref[:, :width]

    q_ref[...] = (norm_rope(mm(C_AQ, C_AK), qnw_ref[...], 256) * HEAD_DIM ** -0.5).astype(bf16)
    k_ref[...] = norm_rope(mm(C_AK, C_AV), knw_ref[...], 128).astype(bf16)


def _inproj(h, nw, w, cos, sin, rot, gavg, qnw, knw, bsz, lp):
    n = h.shape[0]
    d = h.shape[1]
    tm = TILE * _largest_divisor(lp // TILE, 3)
    per = lp // tm
    row = lambda b, i: (b * per + i, 0)
    const = lambda b, i: (0, 0)
    tab = lambda b, i: (i, 0)
    outs = ((1024, f32), (128, f32), (768, f32), (512, bf16), (256, bf16), (128, bf16), (128, bf16))
    return pl.pallas_call(
        _inproj_body,
        grid=(bsz, per),
        in_specs=[pl.BlockSpec((tm, d), row), pl.BlockSpec((1, d), const),
                  pl.BlockSpec((d, PACK_WIDTH), const),
                  pl.BlockSpec((tm, 256), tab), pl.BlockSpec((tm, 256), tab),
                  pl.BlockSpec((256, 256), const), pl.BlockSpec((256, 256), const),
                  pl.BlockSpec((1, 256), const), pl.BlockSpec((1, 128), const)],
        out_specs=[pl.BlockSpec((tm, c), row) for c, _ in outs],
        out_shape=[jax.ShapeDtypeStruct((n, c), t) for c, t in outs],
        compiler_params=_cparams("parallel", "parallel"),
        name="inproj",
    )(h, nw, w, cos, sin, rot, gavg, qnw, knw)


def _largest_divisor(n, cap):
    return max(k for k in range(1, cap + 1) if n % k == 0)


def _ssd_body(rev, nt, *refs):
    if rev:
        (cur_ref, prev_ref, next_ref, small_ref, convw_ref, convb_ref, dtb_ref, alog_ref,
         yf_ref, z_ref, normw_ref, out_ref, ext_ref, state_ref, y_ref) = refs
    else:
        (cur_ref, prev_ref, next_ref, small_ref, convw_ref, convb_ref, dtb_ref, alog_ref,
         dskip_ref, out_ref, ext_ref, state_ref, y_ref) = refs
    c = pl.program_id(1)
    tile = (nt - 1 - c) if rev else c

    @pl.when(c == 0)
    def _():
        state_ref[...] = jnp.zeros_like(state_ref)

    valid = _valid_rows(tile, TILE)
    ext_ref[0:CONV_HALO, :] = jnp.where(tile > 0, prev_ref[...], 0.0)
    ext_ref[CONV_HALO:CONV_HALO + TILE, :] = jnp.where(valid, cur_ref[...], 0.0)
    ext_ref[CONV_HALO + TILE:, :] = jnp.where(tile < nt - 1, next_ref[...], 0.0)
    acc = jnp.zeros((TILE, SSD_CONV_DIM), f32) + convb_ref[...]
    first = CONV_HALO - (SSD_CONV - 1) // 2
    for k in range(SSD_CONV):
        acc = acc + convw_ref[k:k + 1, :] * ext_ref[first + k:first + k + TILE, :]
    xbc = jnp.where(valid, _silu(acc), 0.0)
    xs = xbc[:, :SSD_INNER]

    dt = jnp.where(valid, _softplus(small_ref[...] + dtb_ref[...]), 0.0)
    dta = dt * (-jnp.exp(alog_ref[...]))
    cum = _dot_split_rhs(_tri(TILE, rev).astype(bf16), dta)
    cum_t = cum.T
    tot = jnp.sum(dta, axis=0, keepdims=True)
    causal = _tri(TILE, rev)

    cb = []
    for g in range(2):
        bg = xbc[:, SSD_INNER + SSD_STATE * g:SSD_INNER + SSD_STATE * (g + 1)].astype(bf16)
        cg = xbc[:, SSD_INNER + 2 * SSD_STATE + SSD_STATE * g:
                 SSD_INNER + 2 * SSD_STATE + SSD_STATE * (g + 1)].astype(bf16)
        cb.append((bg, cg, _dot_nt(cg, bg)))

    for hh in range(SSD_HEADS):
        col = SMALL_DT + hh + (SSD_HEADS if rev else 0)
        bg, cg, cbg = cb[hh // 2]
        a_col = cum[:, col:col + 1]
        a_row = cum_t[col:col + 1, :]
        decay = jnp.where(causal, jnp.exp(a_col - a_row), 0.0)
        xd = (xs[:, SSD_HEAD_DIM * hh:SSD_HEAD_DIM * (hh + 1)] * dt[:, col:col + 1]).astype(bf16)
        s_in = state_ref[hh]
        y = _dot((cbg * decay).astype(bf16), xd) + _dot(cg, s_in.astype(bf16)) * jnp.exp(a_col)
        tot_h = tot[:, col:col + 1]
        bw = (bg.astype(f32) * jnp.exp(tot_h - a_col)).astype(bf16)
        state_ref[hh] = jnp.exp(tot_h) * s_in + _dot_tn(bw, xd)
        y_ref[:, SSD_HEAD_DIM * hh:SSD_HEAD_DIM * (hh + 1)] = y

    if rev:
        y = (yf_ref[...] + y_ref[...]) * _silu(z_ref[...])
        out_ref[...] = jnp.where(valid, _rms(y, normw_ref[...]), 0.0).astype(out_ref.dtype)
    else:
        out_ref[...] = y_ref[...] + dskip_ref[...] * xs


def _ssd(rev, o_ssd, o_small, convw, convb, dtb, alog, extra, bsz, lp):
    n = o_ssd.shape[0]
    nt = lp // TILE
    hb = TILE // CONV_HALO
    n_halo = n // CONV_HALO

    def tile_of(b, c):
        return b * nt + ((nt - 1 - c) if rev else c)

    cur = lambda b, c: (tile_of(b, c), 0)
    prev = lambda b, c: (jnp.maximum(tile_of(b, c) * hb - 1, 0), 0)
    nxt = lambda b, c: (jnp.minimum((tile_of(b, c) + 1) * hb, n_halo - 1), 0)
    zcol = lambda b, c: (tile_of(b, c), SSD_CONV_DIM // SSD_INNER)
    const = lambda b, c: (0, 0)
    in_specs = [pl.BlockSpec((TILE, SSD_CONV_DIM), cur), pl.BlockSpec((CONV_HALO, SSD_CONV_DIM), prev),
                pl.BlockSpec((CONV_HALO, SSD_CONV_DIM), nxt), pl.BlockSpec((TILE, 128), cur),
                pl.BlockSpec((8, SSD_CONV_DIM), const), pl.BlockSpec((1, SSD_CONV_DIM), const),
                pl.BlockSpec((1, 128), const), pl.BlockSpec((1, 128), const)]
    args = [o_ssd, o_ssd, o_ssd, o_small, convw, convb, dtb, alog]
    if rev:
        yf, normw = extra
        in_specs += [pl.BlockSpec((TILE, SSD_INNER), cur), pl.BlockSpec((TILE, SSD_INNER), zcol),
                     pl.BlockSpec((1, SSD_INNER), const)]
        args += [yf, o_ssd, normw]
        out_dtype = bf16
    else:
        in_specs += [pl.BlockSpec((1, SSD_INNER), const)]
        args += [extra]
        out_dtype = f32
    return pl.pallas_call(
        functools.partial(_ssd_body, rev, nt),
        grid=(bsz, nt),
        in_specs=in_specs,
        out_specs=pl.BlockSpec((TILE, SSD_INNER), cur),
        out_shape=jax.ShapeDtypeStruct((n, SSD_INNER), out_dtype),
        scratch_shapes=[pltpu.VMEM((TILE + 2 * CONV_HALO, SSD_CONV_DIM), f32),
                        pltpu.VMEM((SSD_HEADS, SSD_STATE, SSD_HEAD_DIM), f32),
                        pltpu.VMEM((TILE, SSD_INNER), f32)],
        compiler_params=_cparams("parallel", "arbitrary"),
        name="ssd_rev" if rev else "ssd_fwd",
    )(*args)


def _gla_body(rev, nt, *refs):
    if rev:
        (x_ref, small_ref, wg_ref, gb_ref, of_ref, normw_ref, gavg_ref,
         out_ref, st_ref, o_ref) = refs
    else:
        x_ref, small_ref, wg_ref, gb_ref, out_ref, st_ref, o_ref = refs
    c = pl.program_id(1)
    tile = (nt - 1 - c) if rev else c

    @pl.when(c == 0)
    def _():
        st_ref[...] = jnp.zeros_like(st_ref)

    valid = _valid_rows(tile, TILE)
    x = x_ref[...]
    pre = _dot(small_ref[...].astype(bf16), wg_ref[...]) + gb_ref[...]
    g = jnp.where(valid, _log_sigmoid(pre) / GLA_GATE_NORM, 0.0)
    q = jnp.where(valid, x[:, :GLA_KEY], 0.0) * GLA_DK ** -0.5
    k = jnp.where(valid, x[:, GLA_KEY:2 * GLA_KEY], 0.0)
    v = jnp.where(valid, x[:, 2 * GLA_KEY:2 * GLA_KEY + GLA_VAL], 0.0).astype(bf16)

    n = GLA_CHUNK
    tri = _tri(n, rev)
    tri_b = tri.astype(bf16)
    lane_head = lax.broadcasted_iota(jnp.int32, (n, GLA_KEY), 1) // GLA_DK
    blockdiag = (lax.broadcasted_iota(jnp.int32, (GLA_VAL, GLA_KEY), 0) // GLA_DV
                 == lax.broadcasted_iota(jnp.int32, (GLA_VAL, GLA_KEY), 1) // GLA_DK)
    for sub in ((1, 0) if rev else (0, 1)):
        sl = slice(n * sub, n * (sub + 1))
        gs, qs, ks, vs = g[sl], q[sl], k[sl], v[sl]
        bc = _dot_split_rhs(tri_b, gs)
        bl = jnp.sum(gs, axis=0, keepdims=True)
        qt = qs * jnp.exp(bc)
        kt = (ks * jnp.exp(-bc)).astype(bf16)
        kw = (ks * jnp.exp(bl - bc)).astype(bf16)
        st = st_ref[...]
        o_inter = _dot_nt(qt.astype(bf16), st.astype(bf16))
        for hh in range(GLA_HEADS):
            qh = jnp.where(lane_head == hh, qt, 0.0).astype(bf16)
            att = jnp.where(tri, _dot_nt(qh, kt), 0.0)
            lo = GLA_DV * hh
            o_ref[sl, lo:lo + GLA_DV] = (_dot(att.astype(bf16), vs[:, lo:lo + GLA_DV])
                                         + o_inter[:, lo:lo + GLA_DV])
        st_ref[...] = st * jnp.exp(bl) + jnp.where(blockdiag, _dot_tn(vs, kw), 0.0)

    if rev:
        o = of_ref[...] + o_ref[...]
        ms = _dot_split_lhs(o * o, gavg_ref[...])
        on = o * lax.rsqrt(ms + EPS) * normw_ref[...]
        r = x[:, 2 * GLA_KEY + GLA_VAL:]
        out_ref[...] = jnp.where(valid, on * _silu(r), 0.0).astype(out_ref.dtype)
    else:
        out_ref[...] = o_ref[...]


def _gla(rev, o_gla, o_small, wg, gb, extra, bsz, lp):
    n = o_gla.shape[0]
    nt = lp // TILE
    cur = lambda b, c: (b * nt + ((nt - 1 - c) if rev else c), 0)
    const = lambda b, c: (0, 0)
    in_specs = [pl.BlockSpec((TILE, 768), cur), pl.BlockSpec((TILE, 128), cur),
                pl.BlockSpec((128, GLA_KEY), const), pl.BlockSpec((1, GLA_KEY), const)]
    args = [o_gla, o_small, wg, gb]
    if rev:
        of, normw, gavg = extra
        in_specs += [pl.BlockSpec((TILE, GLA_VAL), cur), pl.BlockSpec((1, GLA_VAL), const),
                     pl.BlockSpec((GLA_VAL, GLA_VAL), const)]
        args += [of, normw, gavg]
    return pl.pallas_call(
        functools.partial(_gla_body, rev, nt),
        grid=(bsz, nt),
        in_specs=in_specs,
        out_specs=pl.BlockSpec((TILE, GLA_VAL), cur),
        out_shape=jax.ShapeDtypeStruct((n, GLA_VAL), bf16 if rev else f32),
        scratch_shapes=[pltpu.VMEM((GLA_VAL, GLA_KEY), f32), pltpu.VMEM((TILE, GLA_VAL), f32)],
        compiler_params=_cparams("parallel", "arbitrary"),
        name="gla_rev" if rev else "gla_fwd",
    )(*args)


def _swa_body(nt, q_ref, kp_ref, kc_ref, kn_ref, vp_ref, vc_ref, vn_ref, km_ref, vm_ref,
              bias_ref, bmeta_ref, sink_ref, out_ref, o_ref):
    c = pl.program_id(1)
    kcol = lax.broadcasted_iota(jnp.int32, (1, 3 * TILE), 1)
    pen = jnp.where(kcol < TILE, jnp.where(c >= 2, 0.0, NEG),
                    jnp.where(kcol < 2 * TILE, jnp.where(c >= 1, 0.0, NEG),
                              jnp.where(c <= nt - 2, 0.0, NEG)))
    q = q_ref[...]
    kcat = jnp.concatenate([kp_ref[...], kc_ref[...], kn_ref[...]], axis=0)
    vcat = jnp.concatenate([vp_ref[...], vc_ref[...], vn_ref[...]], axis=0)
    km = km_ref[...]
    vm = vm_ref[...]
    scale = HEAD_DIM ** -0.5
    for hh in range(4):
        lo = HEAD_DIM * (hh // 2)
        qh = q[:, HEAD_DIM * hh:HEAD_DIM * (hh + 1)]
        s = _dot_nt(qh, kcat[:, lo:lo + HEAD_DIM]) * scale + bias_ref[hh] + pen
        sm = _dot_nt(qh, km[:, lo:lo + HEAD_DIM]) * scale + bmeta_ref[hh]
        sk = sink_ref[hh:hh + 1, 0:1]
        m = jnp.maximum(jnp.maximum(jnp.max(s, axis=-1, keepdims=True),
                                    jnp.max(sm, axis=-1, keepdims=True)), sk)
        p = jnp.exp(s - m)
        pm = jnp.exp(sm - m)
        den = (jnp.sum(p, axis=-1, keepdims=True) + jnp.sum(pm, axis=-1, keepdims=True)
               + jnp.exp(sk - m))
        o = _dot(p.astype(bf16), vcat[:, lo:lo + HEAD_DIM]) + _dot(pm.astype(bf16), vm[:, lo:lo + HEAD_DIM])
        o_ref[:, HEAD_DIM * hh:HEAD_DIM * (hh + 1)] = o / den
    out_ref[...] = jnp.where(_valid_rows(c, TILE), o_ref[...], 0.0).astype(out_ref.dtype)


def _swa(o_swa, bias_band, bias_meta, sink, bsz, lp):
    n = o_swa.shape[0]
    nt = lp // TILE
    ntot = n // TILE
    mb = TILE // N_META
    g = lambda b, c: b * nt + c
    cur = lambda col: (lambda b, c: (g(b, c), col))
    prev = lambda col: (lambda b, c: (jnp.maximum(g(b, c) - 1, 0), col))
    nxt = lambda col: (lambda b, c: (jnp.minimum(g(b, c) + 1, ntot - 1), col))
    meta = lambda col: (lambda b, c: (b * nt * mb + mb - 1, col))
    kv = lambda im: pl.BlockSpec((TILE, 128), im)
    return pl.pallas_call(
        functools.partial(_swa_body, nt),
        grid=(bsz, nt),
        in_specs=[pl.BlockSpec((TILE, 256), cur(0)),
                  kv(prev(2)), kv(cur(2)), kv(nxt(2)), kv(prev(3)), kv(cur(3)), kv(nxt(3)),
                  pl.BlockSpec((N_META, 128), meta(2)), pl.BlockSpec((N_META, 128), meta(3)),
                  pl.BlockSpec((4, TILE, 3 * TILE), lambda b, c: (0, 0, 0)),
                  pl.BlockSpec((4, TILE, N_META), lambda b, c: (0, c, 0)),
                  pl.BlockSpec((8, 128), lambda b, c: (0, 0))],
        out_specs=pl.BlockSpec((TILE, 256), cur(0)),
        out_shape=jax.ShapeDtypeStruct((n, 256), bf16),
        scratch_shapes=[pltpu.VMEM((TILE, 256), f32)],
        compiler_params=_cparams("parallel", "parallel"),
        name="swa",
    )(o_swa, o_swa, o_swa, o_swa, o_swa, o_swa, o_swa, o_swa, o_swa, bias_band, bias_meta, sink)


def _flash_body(lp, tq, tk, q_ref, k_ref, v_ref, out_ref, o_ref):
    i = pl.program_id(1)
    nk = lp // tk
    kcol = lax.broadcasted_iota(jnp.int32, (1, tk), 1)
    for hh in range(4):
        lo = HEAD_DIM * (hh // 2)
        qh = q_ref[:, HEAD_DIM * hh:HEAD_DIM * (hh + 1)]

        def step(start, carry, first):
            m, l, acc = carry
            kc = k_ref[pl.ds(start, tk), lo:lo + HEAD_DIM]
            vc = v_ref[pl.ds(start, tk), lo:lo + HEAD_DIM]
            s = _dot_nt(qh, kc)
            if first:
                s = jnp.where(kcol >= TILE - N_META, s, NEG)
            m_new = jnp.maximum(m, jnp.max(s, axis=-1, keepdims=True))
            alpha = jnp.exp(m - m_new)
            p = jnp.exp(s - m_new)
            l = alpha * l + jnp.sum(p, axis=-1, keepdims=True)
            acc = alpha * acc + _dot(p.astype(bf16), vc)
            return m_new, l, acc

        carry = (jnp.full((tq, 1), NEG, f32), jnp.zeros((tq, 1), f32), jnp.zeros((tq, HEAD_DIM), f32))
        carry = step(0, carry, True)
        carry = lax.fori_loop(1, nk, lambda j, cr: step(pl.multiple_of(j * tk, tk), cr, False), carry)
        _, l, acc = carry
        o_ref[:, HEAD_DIM * hh:HEAD_DIM * (hh + 1)] = acc / l
    rows = lax.broadcasted_iota(jnp.int32, (tq, 1), 0)
    valid = jnp.logical_or(i > 0, rows >= TILE - N_META)
    out_ref[...] = jnp.where(valid, o_ref[...], 0.0).astype(out_ref.dtype)


def _flash(q, k, v, bsz, lp):
    n = q.shape[0]
    tq = TILE * _largest_divisor(lp // TILE, 3)
    per = lp // tq
    return pl.pallas_call(
        functools.partial(_flash_body, lp, tq, tq),
        grid=(bsz, per),
        in_specs=[pl.BlockSpec((tq, 256), lambda b, i: (b * per + i, 0)),
                  pl.BlockSpec((lp, 128), lambda b, i: (b, 0)),
                  pl.BlockSpec((lp, 128), lambda b, i: (b, 0))],
        out_specs=pl.BlockSpec((tq, 256), lambda b, i: (b * per + i, 0)),
        out_shape=jax.ShapeDtypeStruct((n, 256), bf16),
        scratch_shapes=[pltpu.VMEM((tq, 256), f32)],
        compiler_params=_cparams("parallel", "parallel"),
        name="gqa_full",
    )(q, k, v)


def _outproj_body(y0_ref, y1_ref, y2_ref, y3_ref, h_ref, w_ref, out_ref):
    acc = h_ref[...]
    for j, y_ref in enumerate((y0_ref, y1_ref, y2_ref, y3_ref)):
        acc = acc + _dot(y_ref[...], w_ref[256 * j:256 * (j + 1), :])
    out_ref[...] = acc


def _outproj(ys, h, w):
    n, d = h.shape
    tm = TILE * _largest_divisor(n // TILE, 4)
    row = lambda i: (i, 0)
    return pl.pallas_call(
        _outproj_body,
        grid=(n // tm,),
        in_specs=[pl.BlockSpec((tm, 256), row)] * 4 + [pl.BlockSpec((tm, d), row),
                                                      pl.BlockSpec((d, d), lambda i: (0, 0))],
        out_specs=pl.BlockSpec((tm, d), row),
        out_shape=jax.ShapeDtypeStruct((n, d), f32),
        compiler_params=_cparams("parallel"),
        name="outproj",
    )(*ys, h, w)


def _ffn_body(h_ref, nw_ref, wg_ref, wu_ref, wd_ref, out_ref, u_ref, acc_ref):
    j = pl.program_id(1)

    @pl.when(j == 0)
    def _():
        u_ref[...] = _rms(h_ref[...], nw_ref[...]).astype(bf16)
        acc_ref[...] = jnp.zeros_like(acc_ref)

    u = u_ref[...]
    t = _silu(_dot(u, wg_ref[...])) * _dot(u, wu_ref[...])
    acc_ref[...] += _dot(t.astype(bf16), wd_ref[...])

    @pl.when(j == pl.num_programs(1) - 1)
    def _():
        out_ref[...] = h_ref[...] + acc_ref[...]


def _ffn(h, nw, wg, wu, wd):
    n, d = h.shape
    ff = wg.shape[1]
    tm = TILE * _largest_divisor(n // TILE, 4)
    tf = 128 * _largest_divisor(ff // 128, 11)
    row = lambda i, j: (i, 0)
    return pl.pallas_call(
        _ffn_body,
        grid=(n // tm, ff // tf),
        in_specs=[pl.BlockSpec((tm, d), row), pl.BlockSpec((1, d), lambda i, j: (0, 0)),
                  pl.BlockSpec((d, tf), lambda i, j: (0, j)), pl.BlockSpec((d, tf), lambda i, j: (0, j)),
                  pl.BlockSpec((tf, d), lambda i, j: (j, 0))],
        out_specs=pl.BlockSpec((tm, d), row),
        out_shape=jax.ShapeDtypeStruct((n, d), f32),
        scratch_shapes=[pltpu.VMEM((tm, d), bf16), pltpu.VMEM((tm, d), f32)],
        compiler_params=_cparams("parallel", "arbitrary"),
        name="ffn",
    )(h, nw, wg, wu, wd)


def _route(u, r_ref):
    u_hi, u_lo = _split(u)
    r_hi, r_lo = _split(r_ref[...])
    logits = _dot(u_hi, r_hi) + _dot(u_lo, r_hi) + _dot(u_hi, r_lo)
    lane = lax.broadcasted_iota(jnp.int32, logits.shape, 1)
    logits = jnp.where(lane < N_EXPERTS, logits, NEG)
    m1 = jnp.max(logits, axis=-1, keepdims=True)
    i1 = jnp.min(jnp.where(logits == m1, lane, 128), axis=-1, keepdims=True)
    rest = jnp.where(lane == i1, NEG, logits)
    m2 = jnp.max(rest, axis=-1, keepdims=True)
    i2 = jnp.min(jnp.where(rest == m2, lane, 128), axis=-1, keepdims=True)
    e2 = jnp.exp(m2 - m1)
    g1 = 1.0 / (1.0 + e2)
    return jnp.where(lane == i1, g1, 0.0) + jnp.where(lane == i2, e2 * g1, 0.0)


def _moe_body(final, h_ref, nw_ref, r_ref, wg_ref, wu_ref, wd_ref, fnw_ref, out_ref,
              u_ref, comb_ref, acc_ref):
    e = pl.program_id(1)
    j = pl.program_id(2)

    @pl.when(jnp.logical_and(e == 0, j == 0))
    def _():
        u = _rms(h_ref[...], nw_ref[...])
        u_ref[...] = u.astype(bf16)
        comb_ref[...] = _route(u, r_ref)
        acc_ref[...] = jnp.zeros_like(acc_ref)

    u = u_ref[...]
    comb = comb_ref[...]
    lane = lax.broadcasted_iota(jnp.int32, comb.shape, 1)
    ce = jnp.sum(jnp.where(lane == e, comb, 0.0), axis=-1, keepdims=True)
    t = _silu(_dot(u, wg_ref[...])) * _dot(u, wu_ref[...]) * ce
    acc_ref[...] += _dot(t.astype(bf16), wd_ref[...])

    @pl.when(jnp.logical_and(e == pl.num_programs(1) - 1, j == pl.num_programs(2) - 1))
    def _():
        y = h_ref[...] + acc_ref[...]
        out_ref[...] = _rms(y, fnw_ref[...]) if final else y


def _moe(h, nw, router, wg, wu, wd, fnw, final):
    n, d = h.shape
    ne, _, ff = wg.shape
    tm = TILE * _largest_divisor(n // TILE, 4)
    tf = 128 * _largest_divisor(ff // 128, 7)
    row = lambda i, e, j: (i, 0)
    const = lambda i, e, j: (0, 0)
    return pl.pallas_call(
        functools.partial(_moe_body, final),
        grid=(n // tm, ne, ff // tf),
        in_specs=[pl.BlockSpec((tm, d), row), pl.BlockSpec((1, d), const),
                  pl.BlockSpec((d, 128), const),
                  pl.BlockSpec((None, d, tf), lambda i, e, j: (e, 0, j)),
                  pl.BlockSpec((None, d, tf), lambda i, e, j: (e, 0, j)),
                  pl.BlockSpec((None, tf, d), lambda i, e, j: (e, j, 0)),
                  pl.BlockSpec((1, d), const)],
        out_specs=pl.BlockSpec((tm, d), row),
        out_shape=jax.ShapeDtypeStruct((n, d), f32),
        scratch_shapes=[pltpu.VMEM((tm, d), bf16), pltpu.VMEM((tm, 128), f32), pltpu.VMEM((tm, d), f32)],
        compiler_params=_cparams("parallel", "arbitrary", "arbitrary"),
        name="moe",
    )(h, nw, router, wg, wu, wd, fnw)


def _final_norm_body(h_ref, w_ref, out_ref):
    out_ref[...] = _rms(h_ref[...], w_ref[...])


def _final_norm(h, w):
    n, d = h.shape
    tm = TILE * _largest_divisor(n // TILE, 8)
    return pl.pallas_call(
        _final_norm_body,
        grid=(n // tm,),
        in_specs=[pl.BlockSpec((tm, d), lambda i: (i, 0)), pl.BlockSpec((1, d), lambda i: (0, 0))],
        out_specs=pl.BlockSpec((tm, d), lambda i: (i, 0)),
        out_shape=jax.ShapeDtypeStruct((n, d), f32),
        compiler_params=_cparams("parallel"),
        name="final_norm",
    )(h, w)


def _rope_tables(seq, pad):
    t = jnp.arange(seq)
    meta_pos = jnp.arange(N_META) - N_META
    row = jnp.concatenate([meta_pos, t // GRID_W]).astype(f32)
    col = jnp.concatenate([meta_pos, t % GRID_W]).astype(f32)
    half = HEAD_DIM // 2
    inv = ROPE_THETA ** (-jnp.arange(0, half, 2, dtype=f32) / half)
    ang = jnp.concatenate([row[:, None] * inv, col[:, None] * inv], axis=-1)
    ang = jnp.tile(jnp.repeat(ang, 2, axis=-1), (1, 4))
    ang = jnp.pad(ang, ((pad, 0), (0, 0)))
    return jnp.cos(ang), jnp.sin(ang)


def _pair_swap_matrix(width):
    i = jnp.arange(width)
    p = jnp.zeros((width, width), f32)
    p = p.at[i[1::2], i[0::2]].set(-1.0)
    p = p.at[i[0::2], i[1::2]].set(1.0)
    return p.astype(bf16)


def _group_mean_matrix(width, group):
    i = jnp.arange(width)
    return ((i[:, None] // group == i[None, :] // group).astype(f32) / group).astype(bf16)


def _t5_bucket(rel):
    nb = REL_BUCKETS // 2
    max_exact = nb // 2
    ret = (rel > 0).astype(jnp.int32) * nb
    n = jnp.abs(rel)
    nf = jnp.maximum(n, 1).astype(f32)
    large = max_exact + (jnp.log(nf / max_exact) / math.log(REL_MAX_DIST / max_exact)
                         * (nb - max_exact)).astype(jnp.int32)
    large = jnp.minimum(large, nb - 1)
    return ret + jnp.where(n < max_exact, n, large)


def _swa_bias_tables(rel_bias, lp):
    qi = jnp.arange(TILE)
    ki = jnp.arange(3 * TILE)
    rel = ki[None, :] - TILE - qi[:, None]
    band = jnp.moveaxis(rel_bias[_t5_bucket(rel)].astype(f32), -1, 0)
    band = jnp.where((jnp.abs(rel) <= WINDOW)[None], band, NEG)
    pos = jnp.arange(lp) - (TILE - N_META)
    rel_m = jnp.arange(N_META)[None, :] - pos[:, None]
    meta = jnp.moveaxis(rel_bias[_t5_bucket(rel_m)].astype(f32), -1, 0)
    return band, meta


def _row(v, width=None):
    v = v.astype(f32).reshape(1, -1)
    if width is not None and v.shape[1] < width:
        v = jnp.pad(v, ((0, 0), (0, width - v.shape[1])))
    return v


def kernel(x, meta_tokens, rel_bias, norm_mix_w, norm_ffn_w, w_in, ssd_conv_w, ssd_conv_b, ssd_dt_bias, ssd_a_log, ssd_d, ssd_norm_w, gla_gate_w2, gla_gate_b, gla_norm_w, swa_sink, gqa_q_norm_w, gqa_k_norm_w, w_out, ffn_w_gate, ffn_w_up, ffn_w_down, moe_router, moe_w_gate, moe_w_up, moe_w_down, final_norm_w):
    bsz, seq, d = x.shape
    depth = w_in.shape[0]
    pad = (-(seq + N_META)) % TILE
    assert pad == TILE - N_META and seq % TILE == 0
    lp = pad + N_META + seq
    n = bsz * lp

    meta = jnp.broadcast_to(meta_tokens[None].astype(x.dtype), (bsz, N_META, d))
    h = jnp.concatenate([jnp.zeros((bsz, pad, d), x.dtype), meta, x], axis=1).reshape(n, d)

    cos, sin = _rope_tables(seq, pad)
    rot = _pair_swap_matrix(256)
    gavg = _group_mean_matrix(256, HEAD_DIM)
    bias_band, bias_meta = _swa_bias_tables(rel_bias, lp)
    offs = [0]
    for s in IN_SIZES:
        offs.append(offs[-1] + s)

    for i in range(depth):
        wi = w_in[i]
        cols = [wi[:, offs[j]:offs[j + 1]] for j in PACK_ORDER]
        w_pack = jnp.concatenate(cols + [jnp.zeros((d, PACK_WIDTH - offs[-1]), wi.dtype)], axis=1).astype(bf16)
        o_ssd, o_small, o_gla, o_swa, aq, ak, av = _inproj(
            h, _row(norm_mix_w[i]), w_pack, cos, sin, rot, gavg,
            _row(jnp.tile(gqa_q_norm_w[i], 4)), _row(jnp.tile(gqa_k_norm_w[i], 2)), bsz, lp)

        convw = jnp.pad(ssd_conv_w[i].astype(f32), ((0, 8 - SSD_CONV), (0, 0)))
        convb = _row(ssd_conv_b[i])
        dtb = _row(ssd_dt_bias[i].reshape(-1), 128)
        alog = _row(ssd_a_log[i].reshape(-1), 128)
        yf = _ssd(False, o_ssd, o_small, convw, convb, dtb, alog,
                  _row(jnp.repeat(ssd_d[i], SSD_HEAD_DIM)), bsz, lp)
        y_ssd = _ssd(True, o_ssd, o_small, convw, convb, dtb, alog, (yf, _row(ssd_norm_w[i])), bsz, lp)

        def gate_w(direction):
            lo = SMALL_GA + GLA_GATE_RANK * direction
            full = jnp.zeros((128, GLA_KEY), f32).at[lo:lo + GLA_GATE_RANK].set(gla_gate_w2[i, direction].astype(f32))
            return full.astype(bf16)

        of = _gla(False, o_gla, o_small, gate_w(0), _row(gla_gate_b[i, 0]), None, bsz, lp)
        y_gla = _gla(True, o_gla, o_small, gate_w(1), _row(gla_gate_b[i, 1]),
                     (of, _row(jnp.tile(gla_norm_w[i], GLA_HEADS)), gavg), bsz, lp)

        sink = jnp.pad(jnp.broadcast_to(swa_sink[i].astype(f32)[:, None], (4, 128)), ((0, 4), (0, 0)))
        y_swa = _swa(o_swa, bias_band, bias_meta, sink, bsz, lp)
        y_g2 = _flash(aq, ak, av, bsz, lp)

        h = _outproj((y_ssd, y_gla, y_swa, y_g2), h, w_out[i].astype(bf16))

        j = i // 2
        if i % 2 == 0:
            h = _ffn(h, _row(norm_ffn_w[i]), ffn_w_gate[j].astype(bf16), ffn_w_up[j].astype(bf16),
                     ffn_w_down[j].astype(bf16))
            if i == depth - 1:
                h = _final_norm(h, _row(final_norm_w))
        else:
            router = jnp.pad(moe_router[j].astype(f32), ((0, 0), (0, 128 - N_EXPERTS)))
            h = _moe(h, _row(norm_ffn_w[i]), router, moe_w_gate[j].astype(bf16), moe_w_up[j].astype(bf16),
                     moe_w_down[j].astype(bf16), _row(final_norm_w), i == depth - 1)
    return h.reshape(bsz, lp, d)[:, pad + N_META:]
```

```python
import functools
import math

import jax
import jax.numpy as jnp
from jax import lax
from jax.experimental import pallas as pl
from jax.experimental.pallas import tpu as pltpu

f32 = jnp.float32
bf16 = jnp.bfloat16

N_META = 16
HEAD_DIM = 64
GRID_W = 64
EPS = 1e-6
ROPE_THETA = 10000.0
TILE = 128
SSD_HEADS = 4
SSD_HEAD_DIM = 64
SSD_INNER = 256
SSD_STATE = 128
SSD_CONV = 5
SSD_CONV_DIM = 768
GLA_HEADS = 4
GLA_DK = 32
GLA_DV = 64
GLA_KEY = 128
GLA_VAL = 256
GLA_GATE_RANK = 16
GLA_GATE_NORM = 16.0
GLA_CHUNK = 64
WINDOW = 128
REL_BUCKETS = 32
REL_MAX_DIST = 128
N_EXPERTS = 8
NEG = -1e30
LOG2E = math.log2(math.e)
CONV_HALO = 8
VMEM_LIMIT = 56 * 1024 * 1024

IN_SIZES = (256, 768, 8, 128, 128, 256, 256, 32, 256, 128, 128, 256, 128, 128)
(_Z, _XBC, _DT, _GQ, _GK, _GV, _GR, _GA, _SQ, _SK, _SV, _AQ, _AK, _AV) = range(14)
PACK_ORDER = (_XBC, _Z, _GQ, _GK, _GV, _GR, _SQ, _SK, _SV, _AQ, _AK, _AV, _DT, _GA)
PACK_WIDTH = 2944
C_SSD, C_GLA, C_SWA, C_AQ, C_AK, C_AV, C_SMALL = 0, 1024, 1792, 2304, 2560, 2688, 2816
SMALL_DT, SMALL_GA = 0, 8


def _cparams(*sem):
    return pltpu.CompilerParams(dimension_semantics=sem, vmem_limit_bytes=VMEM_LIMIT)


def _dot(a, b):
    return jnp.dot(a, b, preferred_element_type=f32)


def _dot_nt(a, b):
    return lax.dot_general(a, b, (((1,), (1,)), ((), ())), preferred_element_type=f32)


def _dot_tn(a, b):
    return lax.dot_general(a, b, (((0,), (0,)), ((), ())), preferred_element_type=f32)


def _split(a):
    hi = a.astype(bf16)
    lo = (a - hi.astype(f32)).astype(bf16)
    return hi, lo


def _dot_split_lhs(a, b):
    hi, lo = _split(a)
    return _dot(hi, b) + _dot(lo, b)


def _dot_split_rhs(t, x):
    hi, lo = _split(x)
    return _dot(t, hi) + _dot(t, lo)


def _rms(x, w):
    return x * lax.rsqrt(jnp.mean(x * x, axis=-1, keepdims=True) + EPS) * w


def _silu(x):
    return x / (1.0 + jnp.exp(-x))


def _softplus(x):
    return jnp.maximum(x, 0.0) + jnp.log(1.0 + jnp.exp(-jnp.abs(x)))


def _log_sigmoid(x):
    return jnp.minimum(x, 0.0) - jnp.log(1.0 + jnp.exp(-jnp.abs(x)))


def _tri(n, rev):
    r = lax.broadcasted_iota(jnp.int32, (n, n), 0)
    c = lax.broadcasted_iota(jnp.int32, (n, n), 1)
    return (r <= c) if rev else (r >= c)


def _valid_rows(tile, n):
    rows = lax.broadcasted_iota(jnp.int32, (n, 1), 0)
    return jnp.logical_or(tile > 0, rows >= TILE - N_META)


def _inproj_body(h_ref, nw_ref, w_ref, cos_ref, sin_ref, rot_ref, gavg_ref, qnw_ref, knw_ref,
                 ssd_ref, small_ref, gla_ref, swa_ref, k_ref, q_ref, v_ref):
    u = _rms(h_ref[...], nw_ref[...]).astype(bf16)

    def mm(lo, hi):
        return _dot(u, w_ref[:, lo:hi])

    ssd_ref[...] = mm(C_SSD, C_GLA)
    gla_ref[...] = mm(C_GLA, C_SWA)
    swa_ref[...] = mm(C_SWA, C_AQ).astype(bf16)
    small_ref[...] = mm(C_SMALL, PACK_WIDTH)
    v_ref[...] = mm(C_AV, C_SMALL).T.astype(bf16)

    def norm_rope(t, w, width):
        ms = _dot_split_lhs(t * t, gavg_ref[:width, :width])
        tn = t * lax.rsqrt(ms + EPS) * w
        tr = _dot(tn.astype(bf16), rot_ref[:width, :width])
        return tn * cos_ref[:, :width] + tr * sin_ref[:, :width]

    q = norm_rope(mm(C_AQ, C_AK), qnw_ref[...], 256) * (HEAD_DIM ** -0.5 * LOG2E)
    q_ref[...] = q.T.astype(bf16)
    k_ref[...] = norm_rope(mm(C_AK, C_AV), knw_ref[...], 128).astype(bf16)


def _inproj(h, nw, w, cos, sin, rot, gavg, qnw, knw, bsz, lp):
    n = h.shape[0]
    d = h.shape[1]
    tm = TILE * _largest_divisor(lp // TILE, 3)
    per = lp // tm
    row = lambda b, i: (b * per + i, 0)
    const = lambda b, i: (0, 0)
    tab = lambda b, i: (i, 0)
    outs = ((1024, f32), (128, f32), (768, f32), (512, bf16), (128, bf16))
    slab = lambda b, i: (b * per + i, 0, 0)
    return pl.pallas_call(
        _inproj_body,
        grid=(bsz, per),
        in_specs=[pl.BlockSpec((tm, d), row), pl.BlockSpec((1, d), const),
                  pl.BlockSpec((d, PACK_WIDTH), const),
                  pl.BlockSpec((tm, 256), tab), pl.BlockSpec((tm, 256), tab),
                  pl.BlockSpec((256, 256), const), pl.BlockSpec((256, 256), const),
                  pl.BlockSpec((1, 256), const), pl.BlockSpec((1, 128), const)],
        out_specs=([pl.BlockSpec((tm, c), row) for c, _ in outs]
                   + [pl.BlockSpec((None, 256, tm), slab), pl.BlockSpec((None, 128, tm), slab)]),
        out_shape=([jax.ShapeDtypeStruct((n, c), t) for c, t in outs]
                   + [jax.ShapeDtypeStruct((n // tm, 256, tm), bf16),
                      jax.ShapeDtypeStruct((n // tm, 128, tm), bf16)]),
        compiler_params=_cparams("parallel", "parallel"),
        name="inproj",
    )(h, nw, w, cos, sin, rot, gavg, qnw, knw)


def _largest_divisor(n, cap):
    return max(k for k in range(1, cap + 1) if n % k == 0)


def _ssd_body(rev, nt, *refs):
    if rev:
        (cur_ref, prev_ref, next_ref, small_ref, convw_ref, convb_ref, dtb_ref, alog_ref,
         yf_ref, z_ref, normw_ref, out_ref, ext_ref, state_ref, y_ref) = refs
    else:
        (cur_ref, prev_ref, next_ref, small_ref, convw_ref, convb_ref, dtb_ref, alog_ref,
         dskip_ref, out_ref, ext_ref, state_ref, y_ref) = refs
    c = pl.program_id(1)
    tile = (nt - 1 - c) if rev else c

    @pl.when(c == 0)
    def _():
        state_ref[...] = jnp.zeros_like(state_ref)

    valid = _valid_rows(tile, TILE)
    ext_ref[0:CONV_HALO, :] = jnp.where(tile > 0, prev_ref[...], 0.0)
    ext_ref[CONV_HALO:CONV_HALO + TILE, :] = jnp.where(valid, cur_ref[...], 0.0)
    ext_ref[CONV_HALO + TILE:, :] = jnp.where(tile < nt - 1, next_ref[...], 0.0)
    acc = jnp.zeros((TILE, SSD_CONV_DIM), f32) + convb_ref[...]
    first = CONV_HALO - (SSD_CONV - 1) // 2
    for k in range(SSD_CONV):
        acc = acc + convw_ref[k:k + 1, :] * ext_ref[first + k:first + k + TILE, :]
    xbc = jnp.where(valid, _silu(acc), 0.0)
    xs = xbc[:, :SSD_INNER]

    dt = jnp.where(valid, _softplus(small_ref[...] + dtb_ref[...]), 0.0)
    dta = dt * (-jnp.exp(alog_ref[...]))
    cum = _dot_split_rhs(_tri(TILE, rev).astype(bf16), dta)
    cum_t = cum.T
    tot = jnp.sum(dta, axis=0, keepdims=True)
    causal = _tri(TILE, rev)

    cb = []
    for g in range(2):
        bg = xbc[:, SSD_INNER + SSD_STATE * g:SSD_INNER + SSD_STATE * (g + 1)].astype(bf16)
        cg = xbc[:, SSD_INNER + 2 * SSD_STATE + SSD_STATE * g:
                 SSD_INNER + 2 * SSD_STATE + SSD_STATE * (g + 1)].astype(bf16)
        cb.append((bg, cg, _dot_nt(cg, bg)))

    for hh in range(SSD_HEADS):
        col = SMALL_DT + hh + (SSD_HEADS if rev else 0)
        bg, cg, cbg = cb[hh // 2]
        a_col = cum[:, col:col + 1]
        a_row = cum_t[col:col + 1, :]
        decay = jnp.where(causal, jnp.exp(a_col - a_row), 0.0)
        xd = (xs[:, SSD_HEAD_DIM * hh:SSD_HEAD_DIM * (hh + 1)] * dt[:, col:col + 1]).astype(bf16)
        s_in = state_ref[hh]
        y = _dot((cbg * decay).astype(bf16), xd) + _dot(cg, s_in.astype(bf16)) * jnp.exp(a_col)
        tot_h = tot[:, col:col + 1]
        bw = (bg.astype(f32) * jnp.exp(tot_h - a_col)).astype(bf16)
        state_ref[hh] = jnp.exp(tot_h) * s_in + _dot_tn(bw, xd)
        y_ref[:, SSD_HEAD_DIM * hh:SSD_HEAD_DIM * (hh + 1)] = y

    if rev:
        y = (yf_ref[...] + y_ref[...]) * _silu(z_ref[...])
        out_ref[...] = jnp.where(valid, _rms(y, normw_ref[...]), 0.0).astype(out_ref.dtype)
    else:
        out_ref[...] = y_ref[...] + dskip_ref[...] * xs


def _ssd(rev, o_ssd, o_small, convw, convb, dtb, alog, extra, bsz, lp):
    n = o_ssd.shape[0]
    nt = lp // TILE
    hb = TILE // CONV_HALO
    n_halo = n // CONV_HALO

    def tile_of(b, c):
        return b * nt + ((nt - 1 - c) if rev else c)

    cur = lambda b, c: (tile_of(b, c), 0)
    prev = lambda b, c: (jnp.maximum(tile_of(b, c) * hb - 1, 0), 0)
    nxt = lambda b, c: (jnp.minimum((tile_of(b, c) + 1) * hb, n_halo - 1), 0)
    zcol = lambda b, c: (tile_of(b, c), SSD_CONV_DIM // SSD_INNER)
    const = lambda b, c: (0, 0)
    in_specs = [pl.BlockSpec((TILE, SSD_CONV_DIM), cur), pl.BlockSpec((CONV_HALO, SSD_CONV_DIM), prev),
                pl.BlockSpec((CONV_HALO, SSD_CONV_DIM), nxt), pl.BlockSpec((TILE, 128), cur),
                pl.BlockSpec((8, SSD_CONV_DIM), const), pl.BlockSpec((1, SSD_CONV_DIM), const),
                pl.BlockSpec((1, 128), const), pl.BlockSpec((1, 128), const)]
    args = [o_ssd, o_ssd, o_ssd, o_small, convw, convb, dtb, alog]
    if rev:
        yf, normw = extra
        in_specs += [pl.BlockSpec((TILE, SSD_INNER), cur), pl.BlockSpec((TILE, SSD_INNER), zcol),
                     pl.BlockSpec((1, SSD_INNER), const)]
        args += [yf, o_ssd, normw]
        out_dtype = bf16
    else:
        in_specs += [pl.BlockSpec((1, SSD_INNER), const)]
        args += [extra]
        out_dtype = f32
    return pl.pallas_call(
        functools.partial(_ssd_body, rev, nt),
        grid=(bsz, nt),
        in_specs=in_specs,
        out_specs=pl.BlockSpec((TILE, SSD_INNER), cur),
        out_shape=jax.ShapeDtypeStruct((n, SSD_INNER), out_dtype),
        scratch_shapes=[pltpu.VMEM((TILE + 2 * CONV_HALO, SSD_CONV_DIM), f32),
                        pltpu.VMEM((SSD_HEADS, SSD_STATE, SSD_HEAD_DIM), f32),
                        pltpu.VMEM((TILE, SSD_INNER), f32)],
        compiler_params=_cparams("parallel", "arbitrary"),
        name="ssd_rev" if rev else "ssd_fwd",
    )(*args)


def _gla_body(rev, nt, *refs):
    if rev:
        (x_ref, small_ref, wg_ref, gb_ref, of_ref, normw_ref, gavg_ref,
         out_ref, st_ref, o_ref) = refs
    else:
        x_ref, small_ref, wg_ref, gb_ref, out_ref, st_ref, o_ref = refs
    c = pl.program_id(1)
    tile = (nt - 1 - c) if rev else c

    @pl.when(c == 0)
    def _():
        st_ref[...] = jnp.zeros_like(st_ref)

    valid = _valid_rows(tile, TILE)
    x = x_ref[...]
    pre = _dot(small_ref[...].astype(bf16), wg_ref[...]) + gb_ref[...]
    g = jnp.where(valid, _log_sigmoid(pre) / GLA_GATE_NORM, 0.0)
    q = jnp.where(valid, x[:, :GLA_KEY], 0.0) * GLA_DK ** -0.5
    k = jnp.where(valid, x[:, GLA_KEY:2 * GLA_KEY], 0.0)
    v = jnp.where(valid, x[:, 2 * GLA_KEY:2 * GLA_KEY + GLA_VAL], 0.0).astype(bf16)

    n = GLA_CHUNK
    tri = _tri(n, rev)
    tri_b = tri.astype(bf16)
    lane_head = lax.broadcasted_iota(jnp.int32, (n, GLA_KEY), 1) // GLA_DK
    blockdiag = (lax.broadcasted_iota(jnp.int32, (GLA_VAL, GLA_KEY), 0) // GLA_DV
                 == lax.broadcasted_iota(jnp.int32, (GLA_VAL, GLA_KEY), 1) // GLA_DK)
    for sub in ((1, 0) if rev else (0, 1)):
        sl = slice(n * sub, n * (sub + 1))
        gs, qs, ks, vs = g[sl], q[sl], k[sl], v[sl]
        bc = _dot_split_rhs(tri_b, gs)
        bl = jnp.sum(gs, axis=0, keepdims=True)
        qt = qs * jnp.exp(bc)
        kt = (ks * jnp.exp(-bc)).astype(bf16)
        kw = (ks * jnp.exp(bl - bc)).astype(bf16)
        st = st_ref[...]
        o_inter = _dot_nt(qt.astype(bf16), st.astype(bf16))
        for hh in range(GLA_HEADS):
            qh = jnp.where(lane_head == hh, qt, 0.0).astype(bf16)
            att = jnp.where(tri, _dot_nt(qh, kt), 0.0)
            lo = GLA_DV * hh
            o_ref[sl, lo:lo + GLA_DV] = (_dot(att.astype(bf16), vs[:, lo:lo + GLA_DV])
                                         + o_inter[:, lo:lo + GLA_DV])
        st_ref[...] = st * jnp.exp(bl) + jnp.where(blockdiag, _dot_tn(vs, kw), 0.0)

    if rev:
        o = of_ref[...] + o_ref[...]
        ms = _dot_split_lhs(o * o, gavg_ref[...])
        on = o * lax.rsqrt(ms + EPS) * normw_ref[...]
        r = x[:, 2 * GLA_KEY + GLA_VAL:]
        out_ref[...] = jnp.where(valid, on * _silu(r), 0.0).astype(out_ref.dtype)
    else:
        out_ref[...] = o_ref[...]


def _gla(rev, o_gla, o_small, wg, gb, extra, bsz, lp):
    n = o_gla.shape[0]
    nt = lp // TILE
    cur = lambda b, c: (b * nt + ((nt - 1 - c) if rev else c), 0)
    const = lambda b, c: (0, 0)
    in_specs = [pl.BlockSpec((TILE, 768), cur), pl.BlockSpec((TILE, 128), cur),
                pl.BlockSpec((128, GLA_KEY), const), pl.BlockSpec((1, GLA_KEY), const)]
    args = [o_gla, o_small, wg, gb]
    if rev:
        of, normw, gavg = extra
        in_specs += [pl.BlockSpec((TILE, GLA_VAL), cur), pl.BlockSpec((1, GLA_VAL), const),
                     pl.BlockSpec((GLA_VAL, GLA_VAL), const)]
        args += [of, normw, gavg]
    return pl.pallas_call(
        functools.partial(_gla_body, rev, nt),
        grid=(bsz, nt),
        in_specs=in_specs,
        out_specs=pl.BlockSpec((TILE, GLA_VAL), cur),
        out_shape=jax.ShapeDtypeStruct((n, GLA_VAL), bf16 if rev else f32),
        scratch_shapes=[pltpu.VMEM((GLA_VAL, GLA_KEY), f32), pltpu.VMEM((TILE, GLA_VAL), f32)],
        compiler_params=_cparams("parallel", "arbitrary"),
        name="gla_rev" if rev else "gla_fwd",
    )(*args)


def _swa_body(nt, q_ref, kp_ref, kc_ref, kn_ref, vp_ref, vc_ref, vn_ref, km_ref, vm_ref,
              bias_ref, bmeta_ref, sink_ref, out_ref, o_ref):
    c = pl.program_id(1)
    kcol = lax.broadcasted_iota(jnp.int32, (1, 3 * TILE), 1)
    pen = jnp.where(kcol < TILE, jnp.where(c >= 2, 0.0, NEG),
                    jnp.where(kcol < 2 * TILE, jnp.where(c >= 1, 0.0, NEG),
                              jnp.where(c <= nt - 2, 0.0, NEG)))
    q = q_ref[...]
    kcat = jnp.concatenate([kp_ref[...], kc_ref[...], kn_ref[...]], axis=0)
    vcat = jnp.concatenate([vp_ref[...], vc_ref[...], vn_ref[...]], axis=0)
    km = km_ref[...]
    vm = vm_ref[...]
    scale = HEAD_DIM ** -0.5
    for hh in range(4):
        lo = HEAD_DIM * (hh // 2)
        qh = q[:, HEAD_DIM * hh:HEAD_DIM * (hh + 1)]
        s = _dot_nt(qh, kcat[:, lo:lo + HEAD_DIM]) * scale + bias_ref[hh] + pen
        sm = _dot_nt(qh, km[:, lo:lo + HEAD_DIM]) * scale + bmeta_ref[hh]
        sk = sink_ref[hh:hh + 1, 0:1]
        m = jnp.maximum(jnp.maximum(jnp.max(s, axis=-1, keepdims=True),
                                    jnp.max(sm, axis=-1, keepdims=True)), sk)
        p = jnp.exp(s - m)
        pm = jnp.exp(sm - m)
        den = (jnp.sum(p, axis=-1, keepdims=True) + jnp.sum(pm, axis=-1, keepdims=True)
               + jnp.exp(sk - m))
        o = _dot(p.astype(bf16), vcat[:, lo:lo + HEAD_DIM]) + _dot(pm.astype(bf16), vm[:, lo:lo + HEAD_DIM])
        o_ref[:, HEAD_DIM * hh:HEAD_DIM * (hh + 1)] = o / den
    out_ref[...] = jnp.where(_valid_rows(c, TILE), o_ref[...], 0.0).astype(out_ref.dtype)


def _swa(o_swa, bias_band, bias_meta, sink, bsz, lp):
    n = o_swa.shape[0]
    nt = lp // TILE
    ntot = n // TILE
    mb = TILE // N_META
    g = lambda b, c: b * nt + c
    cur = lambda col: (lambda b, c: (g(b, c), col))
    prev = lambda col: (lambda b, c: (jnp.maximum(g(b, c) - 1, 0), col))
    nxt = lambda col: (lambda b, c: (jnp.minimum(g(b, c) + 1, ntot - 1), col))
    meta = lambda col: (lambda b, c: (b * nt * mb + mb - 1, col))
    kv = lambda im: pl.BlockSpec((TILE, 128), im)
    return pl.pallas_call(
        functools.partial(_swa_body, nt),
        grid=(bsz, nt),
        in_specs=[pl.BlockSpec((TILE, 256), cur(0)),
                  kv(prev(2)), kv(cur(2)), kv(nxt(2)), kv(prev(3)), kv(cur(3)), kv(nxt(3)),
                  pl.BlockSpec((N_META, 128), meta(2)), pl.BlockSpec((N_META, 128), meta(3)),
                  pl.BlockSpec((4, TILE, 3 * TILE), lambda b, c: (0, 0, 0)),
                  pl.BlockSpec((4, TILE, N_META), lambda b, c: (0, c, 0)),
                  pl.BlockSpec((8, 128), lambda b, c: (0, 0))],
        out_specs=pl.BlockSpec((TILE, 256), cur(0)),
        out_shape=jax.ShapeDtypeStruct((n, 256), bf16),
        scratch_shapes=[pltpu.VMEM((TILE, 256), f32)],
        compiler_params=_cparams("parallel", "parallel"),
        name="swa",
    )(o_swa, o_swa, o_swa, o_swa, o_swa, o_swa, o_swa, o_swa, o_swa, bias_band, bias_meta, sink)


def _flash_body(nk, tq, tk, qt_ref, k_ref, vt_ref, out_ref, qpad_ref, m_ref, l_ref, acc_ref):
    i = pl.program_id(1)
    krow = lax.broadcasted_iota(jnp.int32, (tk, 1), 0)
    qpad_ref[...] = jnp.zeros_like(qpad_ref)
    for hh in range(4):
        lo = HEAD_DIM * (hh // 2)
        qpad_ref[lo:lo + HEAD_DIM, tq * hh:tq * (hh + 1)] = qt_ref[HEAD_DIM * hh:HEAD_DIM * (hh + 1), :]
    m_ref[...] = jnp.full_like(m_ref, NEG)
    l_ref[...] = jnp.zeros_like(l_ref)
    acc_ref[...] = jnp.zeros_like(acc_ref)

    def step(j, first):
        s = _dot(k_ref[pl.ds(pl.multiple_of(j * tk, tk), tk), :], qpad_ref[...])
        if first:
            s = jnp.where(krow >= TILE - N_META, s, NEG)
        m_old = m_ref[...]
        m_new = jnp.maximum(m_old, jnp.max(s, axis=0, keepdims=True))
        alpha = jnp.exp2(m_old - m_new)
        p = jnp.exp2(s - m_new)
        l_ref[...] = alpha * l_ref[...] + jnp.sum(p, axis=0, keepdims=True)
        m_ref[...] = m_new
        pb = p.astype(bf16)
        for g in range(2):
            pv = _dot(vt_ref[j, HEAD_DIM * g:HEAD_DIM * (g + 1), :], pb[:, 2 * g * tq:(2 * g + 2) * tq])
            for r in range(2):
                hh = 2 * g + r
                rows = slice(HEAD_DIM * hh, HEAD_DIM * (hh + 1))
                acc_ref[rows, :] = (alpha[:, tq * hh:tq * (hh + 1)] * acc_ref[rows, :]
                                    + pv[:, tq * r:tq * (r + 1)])

    step(0, True)

    def body(j, carry):
        step(j, False)
        return carry

    lax.fori_loop(1, nk, body, 0)
    linv = 1.0 / l_ref[...]
    for hh in range(4):
        rows = slice(HEAD_DIM * hh, HEAD_DIM * (hh + 1))
        acc_ref[rows, :] = acc_ref[rows, :] * linv[:, tq * hh:tq * (hh + 1)]
    rows = lax.broadcasted_iota(jnp.int32, (tq, 1), 0)
    valid = jnp.logical_or(i > 0, rows >= TILE - N_META)
    out_ref[...] = jnp.where(valid, acc_ref[...].T, 0.0).astype(out_ref.dtype)


def _flash(qt, k, vt, bsz, lp):
    n = k.shape[0]
    tq = qt.shape[2]
    per = lp // tq
    return pl.pallas_call(
        functools.partial(_flash_body, per, tq, tq),
        grid=(bsz, per),
        in_specs=[pl.BlockSpec((None, 256, tq), lambda b, i: (b * per + i, 0, 0)),
                  pl.BlockSpec((lp, 128), lambda b, i: (b, 0)),
                  pl.BlockSpec((per, 128, tq), lambda b, i: (b, 0, 0))],
        out_specs=pl.BlockSpec((tq, 256), lambda b, i: (b * per + i, 0)),
        out_shape=jax.ShapeDtypeStruct((n, 256), bf16),
        scratch_shapes=[pltpu.VMEM((128, 4 * tq), bf16), pltpu.VMEM((1, 4 * tq), f32),
                        pltpu.VMEM((1, 4 * tq), f32), pltpu.VMEM((256, tq), f32)],
        compiler_params=_cparams("parallel", "parallel"),
        name="gqa_full",
    )(qt, k, vt)


def _outproj_body(y0_ref, y1_ref, y2_ref, y3_ref, h_ref, w_ref, out_ref):
    acc = h_ref[...]
    for j, y_ref in enumerate((y0_ref, y1_ref, y2_ref, y3_ref)):
        acc = acc + _dot(y_ref[...], w_ref[256 * j:256 * (j + 1), :])
    out_ref[...] = acc


def _outproj(ys, h, w):
    n, d = h.shape
    tm = TILE * _largest_divisor(n // TILE, 4)
    row = lambda i: (i, 0)
    return pl.pallas_call(
        _outproj_body,
        grid=(n // tm,),
        in_specs=[pl.BlockSpec((tm, 256), row)] * 4 + [pl.BlockSpec((tm, d), row),
                                                      pl.BlockSpec((d, d), lambda i: (0, 0))],
        out_specs=pl.BlockSpec((tm, d), row),
        out_shape=jax.ShapeDtypeStruct((n, d), f32),
        compiler_params=_cparams("parallel"),
        name="outproj",
    )(*ys, h, w)


def _ffn_body(h_ref, nw_ref, wg_ref, wu_ref, wd_ref, out_ref, u_ref, acc_ref):
    j = pl.program_id(1)

    @pl.when(j == 0)
    def _():
        u_ref[...] = _rms(h_ref[...], nw_ref[...]).astype(bf16)
        acc_ref[...] = jnp.zeros_like(acc_ref)

    u = u_ref[...]
    t = _silu(_dot(u, wg_ref[...])) * _dot(u, wu_ref[...])
    acc_ref[...] += _dot(t.astype(bf16), wd_ref[...])

    @pl.when(j == pl.num_programs(1) - 1)
    def _():
        out_ref[...] = h_ref[...] + acc_ref[...]


def _ffn(h, nw, wg, wu, wd):
    n, d = h.shape
    ff = wg.shape[1]
    tm = TILE * _largest_divisor(n // TILE, 4)
    tf = 128 * _largest_divisor(ff // 128, 11)
    row = lambda i, j: (i, 0)
    return pl.pallas_call(
        _ffn_body,
        grid=(n // tm, ff // tf),
        in_specs=[pl.BlockSpec((tm, d), row), pl.BlockSpec((1, d), lambda i, j: (0, 0)),
                  pl.BlockSpec((d, tf), lambda i, j: (0, j)), pl.BlockSpec((d, tf), lambda i, j: (0, j)),
                  pl.BlockSpec((tf, d), lambda i, j: (j, 0))],
        out_specs=pl.BlockSpec((tm, d), row),
        out_shape=jax.ShapeDtypeStruct((n, d), f32),
        scratch_shapes=[pltpu.VMEM((tm, d), bf16), pltpu.VMEM((tm, d), f32)],
        compiler_params=_cparams("parallel", "arbitrary"),
        name="ffn",
    )(h, nw, wg, wu, wd)


MOE_TM = 1024
ROUTE_E, ROUTE_RANK, ROUTE_GATE, ROUTE_W = 0, 2, 4, 8


def _router_body(h_ref, nw_ref, r_ref, route_ref, cnt_ref, base_ref):
    @pl.when(pl.program_id(0) == 0)
    def _():
        base_ref[...] = jnp.zeros_like(base_ref)

    u = _rms(h_ref[...], nw_ref[...])
    u_hi, u_lo = _split(u)
    r_hi, r_lo = _split(r_ref[...])
    logits = _dot(u_hi, r_hi) + _dot(u_lo, r_hi) + _dot(u_hi, r_lo)
    tm = logits.shape[0]
    lane = lax.broadcasted_iota(jnp.int32, logits.shape, 1)
    logits = jnp.where(lane < N_EXPERTS, logits, NEG)
    m1 = jnp.max(logits, axis=-1, keepdims=True)
    i1 = jnp.min(jnp.where(logits == m1, lane, 128), axis=-1, keepdims=True)
    rest = jnp.where(lane == i1, NEG, logits)
    m2 = jnp.max(rest, axis=-1, keepdims=True)
    i2 = jnp.min(jnp.where(rest == m2, lane, 128), axis=-1, keepdims=True)
    e2 = jnp.exp(m2 - m1)
    g1 = 1.0 / (1.0 + e2)
    g2 = e2 * g1

    sel1 = lane == i1
    sel2 = lane == i2
    onehot = jnp.where(sel1, 1.0, jnp.where(sel2, 1.0, 0.0))
    strict = jnp.where(lax.broadcasted_iota(jnp.int32, (tm, tm), 0)
                       > lax.broadcasted_iota(jnp.int32, (tm, tm), 1), 1.0, 0.0).astype(bf16)
    before = _dot(strict, onehot.astype(bf16)) + base_ref[0:1, :]
    r1 = jnp.sum(jnp.where(sel1, before, 0.0), axis=-1, keepdims=True)
    r2 = jnp.sum(jnp.where(sel2, before, 0.0), axis=-1, keepdims=True)
    route = jnp.zeros(logits.shape, f32)
    for k, val in enumerate((i1.astype(f32), i2.astype(f32), r1, r2, g1, g2)):
        route = jnp.where(lane == k, val, route)
    route_ref[...] = route[:, :ROUTE_W]
    base_ref[0:1, :] = base_ref[0:1, :] + jnp.sum(onehot, axis=0, keepdims=True)
    cnt_ref[...] = base_ref[...]


def _router(h, nw, router):
    n, d = h.shape
    tm = TILE * _largest_divisor(n // TILE, 8)
    return pl.pallas_call(
        _router_body,
        grid=(n // tm,),
        in_specs=[pl.BlockSpec((tm, d), lambda i: (i, 0)), pl.BlockSpec((1, d), lambda i: (0, 0)),
                  pl.BlockSpec((d, 128), lambda i: (0, 0))],
        out_specs=[pl.BlockSpec((tm, ROUTE_W), lambda i: (i, 0)), pl.BlockSpec((8, 128), lambda i: (0, 0))],
        out_shape=[jax.ShapeDtypeStruct((n, ROUTE_W), f32), jax.ShapeDtypeStruct((8, 128), f32)],
        scratch_shapes=[pltpu.VMEM((8, 128), f32)],
        compiler_params=_cparams("arbitrary"),
        name="moe_router",
    )(h, nw, router)


LANES = 128


def _row_copy(src_ref, src_row, dst_ref, dst_row, sem):
    return pltpu.make_async_copy(src_ref.at[src_row], dst_ref.at[dst_row], sem)


def _to_slabs(dst_ref, val):
    for k in range(val.shape[1] // LANES):
        dst_ref[:, k, :] = val[:, LANES * k:LANES * (k + 1)]


def _from_slabs(src_ref, k):
    return src_ref[:, k, :]


def _dispatch_body(fill_ref, h_ref, nw_ref, dest_ref, xs_ref, u_ref, idx_ref, zero_ref,
                   sem_idx, sem_row, sem_fill):
    i = pl.program_id(0)
    tt = u_ref.shape[0]

    @pl.when(i == 0)
    def _():
        zero_ref[...] = jnp.zeros_like(zero_ref)
        fills = [pltpu.make_async_copy(zero_ref, xs_ref.at[pl.ds(fill_ref[e], MOE_TM)], sem_fill)
                 for e in range(N_EXPERTS)]
        for cp in fills:
            cp.start()
        for cp in fills:
            cp.wait()
        last = xs_ref.shape[0] // MOE_TM - 1
        for j in range(last - N_EXPERTS, last + 1):
            @pl.when(j >= fill_ref[N_EXPERTS])
            def _():
                cp = pltpu.make_async_copy(zero_ref, xs_ref.at[pl.ds(j * MOE_TM, MOE_TM)], sem_fill)
                cp.start()
                cp.wait()

    idx_copy = pltpu.make_async_copy(dest_ref.at[pl.ds(i * 2 * tt, 2 * tt)], idx_ref, sem_idx)
    idx_copy.start()
    _to_slabs(u_ref, _rms(h_ref[...], nw_ref[...]))
    idx_copy.wait()

    def issue(r, carry):
        _row_copy(u_ref, r, xs_ref, idx_ref[r], sem_row).start()
        _row_copy(u_ref, r, xs_ref, idx_ref[tt + r], sem_row).start()
        return carry

    def drain(r, carry):
        _row_copy(u_ref, 0, xs_ref, 0, sem_row).wait()
        _row_copy(u_ref, 0, xs_ref, 0, sem_row).wait()
        return carry

    lax.fori_loop(0, tt, issue, 0)
    lax.fori_loop(0, tt, drain, 0)


def _dispatch(h, nw, dest_flat, fill_rows, rows, tt):
    n, d = h.shape
    return pl.pallas_call(
        _dispatch_body,
        grid_spec=pltpu.PrefetchScalarGridSpec(
            num_scalar_prefetch=1,
            grid=(n // tt,),
            in_specs=[pl.BlockSpec((tt, d), lambda i, fr: (i, 0)), pl.BlockSpec((1, d), lambda i, fr: (0, 0)),
                      pl.BlockSpec(memory_space=pl.ANY)],
            out_specs=pl.BlockSpec(memory_space=pl.ANY),
            scratch_shapes=[pltpu.VMEM((tt, d // LANES, LANES), f32), pltpu.SMEM((2 * tt,), jnp.int32),
                            pltpu.VMEM((MOE_TM, d // LANES, LANES), f32), pltpu.SemaphoreType.DMA(()),
                            pltpu.SemaphoreType.DMA(()), pltpu.SemaphoreType.DMA(())]),
        out_shape=jax.ShapeDtypeStruct((rows, d // LANES, LANES), f32),
        compiler_params=_cparams("arbitrary"),
        name="moe_dispatch",
    )(fill_rows, h, nw, dest_flat)


def _experts_body(te_ref, nu_ref, x_ref, wg_ref, wu_ref, wd_ref, y_ref, xb_ref, acc_ref):
    j = pl.program_id(0)
    f = pl.program_id(1)

    @pl.when(j < nu_ref[0])
    def _():
        @pl.when(f == 0)
        def _():
            for k in range(x_ref.shape[1]):
                xb_ref[:, LANES * k:LANES * (k + 1)] = _from_slabs(x_ref, k).astype(bf16)
            acc_ref[...] = jnp.zeros_like(acc_ref)

        x = xb_ref[...]
        t = _silu(_dot(x, wg_ref[...])) * _dot(x, wu_ref[...])
        acc_ref[...] += _dot(t.astype(bf16), wd_ref[...])

        @pl.when(f == pl.num_programs(1) - 1)
        def _():
            _to_slabs(y_ref, acc_ref[...])

    @pl.when(jnp.logical_and(j >= nu_ref[0], f == pl.num_programs(1) - 1))
    def _():
        y_ref[...] = jnp.zeros_like(y_ref)


def _experts(xs, tile_expert, n_used, wg, wu, wd, n_tiles):
    d = wg.shape[1]
    slab = (MOE_TM, d // LANES, LANES)
    ff = wg.shape[2]
    tf = 128 * _largest_divisor(ff // 128, 4)
    nf = ff // tf
    tile = lambda j, nu: jnp.minimum(j, nu[0] - 1)
    chunk = lambda j, f, nu: jnp.where(j < nu[0], f, nf - 1)
    return pl.pallas_call(
        _experts_body,
        grid_spec=pltpu.PrefetchScalarGridSpec(
            num_scalar_prefetch=2,
            grid=(n_tiles, nf),
            in_specs=[pl.BlockSpec(slab, lambda j, f, te, nu: (tile(j, nu), 0, 0)),
                      pl.BlockSpec((None, d, tf), lambda j, f, te, nu: (te[tile(j, nu)], 0, chunk(j, f, nu))),
                      pl.BlockSpec((None, d, tf), lambda j, f, te, nu: (te[tile(j, nu)], 0, chunk(j, f, nu))),
                      pl.BlockSpec((None, tf, d), lambda j, f, te, nu: (te[tile(j, nu)], chunk(j, f, nu), 0))],
            out_specs=pl.BlockSpec(slab, lambda j, f, te, nu: (j, 0, 0)),
            scratch_shapes=[pltpu.VMEM((MOE_TM, d), bf16), pltpu.VMEM((MOE_TM, d), f32)]),
        out_shape=jax.ShapeDtypeStruct((n_tiles * MOE_TM, d // LANES, LANES), f32),
        compiler_params=_cparams("arbitrary", "arbitrary"),
        name="moe_experts",
    )(tile_expert, n_used, xs, wg, wu, wd)


def _combine_body(final, h_ref, route_ref, fnw_ref, dest_ref, ys_ref, out_ref, buf_ref, idx_ref,
                  sem_idx, sem_row):
    i = pl.program_id(0)
    tt = h_ref.shape[0]
    idx_copy = pltpu.make_async_copy(dest_ref.at[pl.ds(i * 2 * tt, 2 * tt)], idx_ref, sem_idx)
    idx_copy.start()
    idx_copy.wait()

    def issue(r, carry):
        _row_copy(ys_ref, idx_ref[r], buf_ref.at[0], r, sem_row).start()
        _row_copy(ys_ref, idx_ref[tt + r], buf_ref.at[1], r, sem_row).start()
        return carry

    def drain(r, carry):
        _row_copy(ys_ref, 0, buf_ref.at[0], 0, sem_row).wait()
        _row_copy(ys_ref, 0, buf_ref.at[0], 0, sem_row).wait()
        return carry

    lax.fori_loop(0, tt, issue, 0)
    lax.fori_loop(0, tt, drain, 0)
    route = route_ref[...]
    g1 = route[:, ROUTE_GATE:ROUTE_GATE + 1]
    g2 = route[:, ROUTE_GATE + 1:ROUTE_GATE + 2]
    for k in range(buf_ref.shape[2]):
        cols = slice(LANES * k, LANES * (k + 1))
        out_ref[:, cols] = (h_ref[:, cols] + g1 * _from_slabs(buf_ref.at[0], k)
                            + g2 * _from_slabs(buf_ref.at[1], k))
    if final:
        out_ref[...] = _rms(out_ref[...], fnw_ref[...])


def _combine(h, route, fnw, dest_flat, ys, tt, final):
    n, d = h.shape
    return pl.pallas_call(
        functools.partial(_combine_body, final),
        grid=(n // tt,),
        in_specs=[pl.BlockSpec((tt, d), lambda i: (i, 0)), pl.BlockSpec((tt, ROUTE_W), lambda i: (i, 0)),
                  pl.BlockSpec((1, d), lambda i: (0, 0)),
                  pl.BlockSpec(memory_space=pl.ANY), pl.BlockSpec(memory_space=pl.ANY)],
        out_specs=pl.BlockSpec((tt, d), lambda i: (i, 0)),
        out_shape=jax.ShapeDtypeStruct((n, d), f32),
        scratch_shapes=[pltpu.VMEM((2, tt, d // LANES, LANES), f32), pltpu.SMEM((2 * tt,), jnp.int32),
                        pltpu.SemaphoreType.DMA(()), pltpu.SemaphoreType.DMA(())],
        compiler_params=_cparams("arbitrary"),
        name="moe_combine",
    )(h, route, fnw, dest_flat, ys)


def _moe(h, nw, router, wg, wu, wd, fnw, final):
    n, d = h.shape
    tt = TILE * _largest_divisor(n // TILE, 4)
    route, counts = _router(h, nw, router)

    cnt = counts[0, :N_EXPERTS].astype(jnp.int32)
    padded = (cnt + MOE_TM - 1) // MOE_TM * MOE_TM
    ends = jnp.cumsum(padded)
    off = ends - padded
    n_tiles = -(-2 * n // MOE_TM) + N_EXPERTS
    tile_expert = jnp.minimum(
        jnp.sum((jnp.arange(n_tiles)[:, None] * MOE_TM >= ends[None, :]).astype(jnp.int32), axis=1),
        N_EXPERTS - 1).astype(jnp.int32)
    n_used = (ends[-1:] // MOE_TM).astype(jnp.int32)
    sel = route[:, ROUTE_E:ROUTE_E + 2].astype(jnp.int32)
    rank = route[:, ROUTE_RANK:ROUTE_RANK + 2].astype(jnp.int32)
    dest = jnp.sum(jnp.where(sel[..., None] == jnp.arange(N_EXPERTS), off, 0), axis=-1) + rank
    dest_flat = dest.reshape(n // tt, tt, 2).transpose(0, 2, 1).reshape(-1)

    fill_rows = jnp.concatenate([off + cnt, n_used]).astype(jnp.int32)
    xs = _dispatch(h, nw, dest_flat, fill_rows, (n_tiles + 1) * MOE_TM, tt)
    ys = _experts(xs, tile_expert, n_used, wg, wu, wd, n_tiles)
    return _combine(h, route, fnw, dest_flat, ys, tt, final)


def _final_norm_body(h_ref, w_ref, out_ref):
    out_ref[...] = _rms(h_ref[...], w_ref[...])


def _final_norm(h, w):
    n, d = h.shape
    tm = TILE * _largest_divisor(n // TILE, 8)
    return pl.pallas_call(
        _final_norm_body,
        grid=(n // tm,),
        in_specs=[pl.BlockSpec((tm, d), lambda i: (i, 0)), pl.BlockSpec((1, d), lambda i: (0, 0))],
        out_specs=pl.BlockSpec((tm, d), lambda i: (i, 0)),
        out_shape=jax.ShapeDtypeStruct((n, d), f32),
        compiler_params=_cparams("parallel"),
        name="final_norm",
    )(h, w)


def _rope_tables(seq, pad):
    t = jnp.arange(seq)
    meta_pos = jnp.arange(N_META) - N_META
    row = jnp.concatenate([meta_pos, t // GRID_W]).astype(f32)
    col = jnp.concatenate([meta_pos, t % GRID_W]).astype(f32)
    half = HEAD_DIM // 2
    inv = ROPE_THETA ** (-jnp.arange(0, half, 2, dtype=f32) / half)
    ang = jnp.concatenate([row[:, None] * inv, col[:, None] * inv], axis=-1)
    ang = jnp.tile(jnp.repeat(ang, 2, axis=-1), (1, 4))
    ang = jnp.pad(ang, ((pad, 0), (0, 0)))
    return jnp.cos(ang), jnp.sin(ang)


def _pair_swap_matrix(width):
    i = jnp.arange(width)
    p = jnp.zeros((width, width), f32)
    p = p.at[i[1::2], i[0::2]].set(-1.0)
    p = p.at[i[0::2], i[1::2]].set(1.0)
    return p.astype(bf16)


def _group_mean_matrix(width, group):
    i = jnp.arange(width)
    return ((i[:, None] // group == i[None, :] // group).astype(f32) / group).astype(bf16)


def _t5_bucket(rel):
    nb = REL_BUCKETS // 2
    max_exact = nb // 2
    ret = (rel > 0).astype(jnp.int32) * nb
    n = jnp.abs(rel)
    nf = jnp.maximum(n, 1).astype(f32)
    large = max_exact + (jnp.log(nf / max_exact) / math.log(REL_MAX_DIST / max_exact)
                         * (nb - max_exact)).astype(jnp.int32)
    large = jnp.minimum(large, nb - 1)
    return ret + jnp.where(n < max_exact, n, large)


def _swa_bias_tables(rel_bias, lp):
    def lookup(bucket):
        out = jnp.zeros((rel_bias.shape[1],) + bucket.shape, f32)
        for b in range(REL_BUCKETS):
            out = jnp.where((bucket == b)[None], rel_bias[b].astype(f32)[:, None, None], out)
        return out

    qi = jnp.arange(TILE)
    ki = jnp.arange(3 * TILE)
    rel = ki[None, :] - TILE - qi[:, None]
    band = jnp.where((jnp.abs(rel) <= WINDOW)[None], lookup(_t5_bucket(rel)), NEG)
    pos = jnp.arange(lp) - (TILE - N_META)
    rel_m = jnp.arange(N_META)[None, :] - pos[:, None]
    return band, lookup(_t5_bucket(rel_m))


def _row(v, width=None):
    v = v.astype(f32).reshape(1, -1)
    if width is not None and v.shape[1] < width:
        v = jnp.pad(v, ((0, 0), (0, width - v.shape[1])))
    return v


def kernel(x, meta_tokens, rel_bias, norm_mix_w, norm_ffn_w, w_in, ssd_conv_w, ssd_conv_b, ssd_dt_bias, ssd_a_log, ssd_d, ssd_norm_w, gla_gate_w2, gla_gate_b, gla_norm_w, swa_sink, gqa_q_norm_w, gqa_k_norm_w, w_out, ffn_w_gate, ffn_w_up, ffn_w_down, moe_router, moe_w_gate, moe_w_up, moe_w_down, final_norm_w):
    bsz, seq, d = x.shape
    depth = w_in.shape[0]
    pad = (-(seq + N_META)) % TILE
    assert pad == TILE - N_META and seq % TILE == 0
    lp = pad + N_META + seq
    n = bsz * lp

    meta = jnp.broadcast_to(meta_tokens[None].astype(x.dtype), (bsz, N_META, d))
    h = jnp.concatenate([jnp.zeros((bsz, pad, d), x.dtype), meta, x], axis=1).reshape(n, d)

    cos, sin = _rope_tables(seq, pad)
    rot = _pair_swap_matrix(256)
    gavg = _group_mean_matrix(256, HEAD_DIM)
    bias_band, bias_meta = _swa_bias_tables(rel_bias, lp)
    offs = [0]
    for s in IN_SIZES:
        offs.append(offs[-1] + s)

    for i in range(depth):
        wi = w_in[i]
        cols = [wi[:, offs[j]:offs[j + 1]] for j in PACK_ORDER]
        w_pack = jnp.concatenate(cols + [jnp.zeros((d, PACK_WIDTH - offs[-1]), wi.dtype)], axis=1).astype(bf16)
        o_ssd, o_small, o_gla, o_swa, ak, aq, av = _inproj(
            h, _row(norm_mix_w[i]), w_pack, cos, sin, rot, gavg,
            _row(jnp.tile(gqa_q_norm_w[i], 4)), _row(jnp.tile(gqa_k_norm_w[i], 2)), bsz, lp)

        convw = jnp.pad(ssd_conv_w[i].astype(f32), ((0, 8 - SSD_CONV), (0, 0)))
        convb = _row(ssd_conv_b[i])
        dtb = _row(ssd_dt_bias[i].reshape(-1), 128)
        alog = _row(ssd_a_log[i].reshape(-1), 128)
        yf = _ssd(False, o_ssd, o_small, convw, convb, dtb, alog,
                  _row(jnp.repeat(ssd_d[i], SSD_HEAD_DIM)), bsz, lp)
        y_ssd = _ssd(True, o_ssd, o_small, convw, convb, dtb, alog, (yf, _row(ssd_norm_w[i])), bsz, lp)

        def gate_w(direction):
            lo = SMALL_GA + GLA_GATE_RANK * direction
            full = jnp.zeros((128, GLA_KEY), f32).at[lo:lo + GLA_GATE_RANK].set(gla_gate_w2[i, direction].astype(f32))
            return full.astype(bf16)

        of = _gla(False, o_gla, o_small, gate_w(0), _row(gla_gate_b[i, 0]), None, bsz, lp)
        y_gla = _gla(True, o_gla, o_small, gate_w(1), _row(gla_gate_b[i, 1]),
                     (of, _row(jnp.tile(gla_norm_w[i], GLA_HEADS)), gavg), bsz, lp)

        sink = jnp.pad(jnp.broadcast_to(swa_sink[i].astype(f32)[:, None], (4, 128)), ((0, 4), (0, 0)))
        y_swa = _swa(o_swa, bias_band, bias_meta, sink, bsz, lp)
        y_g2 = _flash(aq, ak, av, bsz, lp)

        h = _outproj((y_ssd, y_gla, y_swa, y_g2), h, w_out[i].astype(bf16))

        j = i // 2
        if i % 2 == 0:
            h = _ffn(h, _row(norm_ffn_w[i]), ffn_w_gate[j].astype(bf16), ffn_w_up[j].astype(bf16),
                     ffn_w_down[j].astype(bf16))
            if i == depth - 1:
                h = _final_norm(h, _row(final_norm_w))
        else:
            router = jnp.pad(moe_router[j].astype(f32), ((0, 0), (0, 128 - N_EXPERTS)))
            h = _moe(h, _row(norm_ffn_w[i]), router, moe_w_gate[j].astype(bf16), moe_w_up[j].astype(bf16),
                     moe_w_down[j].astype(bf16), _row(final_norm_w), i == depth - 1)
    return h.reshape(bsz, lp, d)[:, pad + N_META:]
```

```python
import functools
import math

import jax
import jax.numpy as jnp
from jax import lax
from jax.experimental import pallas as pl
from jax.experimental.pallas import tpu as pltpu

f32 = jnp.float32
bf16 = jnp.bfloat16

N_META = 16
HEAD_DIM = 64
GRID_W = 64
EPS = 1e-6
ROPE_THETA = 10000.0
TILE = 128
SSD_HEADS = 4
SSD_HEAD_DIM = 64
SSD_INNER = 256
SSD_STATE = 128
SSD_CONV = 5
SSD_CONV_DIM = 768
GLA_HEADS = 4
GLA_DK = 32
GLA_DV = 64
GLA_KEY = 128
GLA_VAL = 256
GLA_GATE_RANK = 16
GLA_GATE_NORM = 16.0
GLA_CHUNK = 64
WINDOW = 128
REL_BUCKETS = 32
REL_MAX_DIST = 128
N_EXPERTS = 8
NEG = -1e30
LOG2E = math.log2(math.e)
CONV_HALO = 8
VMEM_LIMIT = 56 * 1024 * 1024

IN_SIZES = (256, 768, 8, 128, 128, 256, 256, 32, 256, 128, 128, 256, 128, 128)
(_Z, _XBC, _DT, _GQ, _GK, _GV, _GR, _GA, _SQ, _SK, _SV, _AQ, _AK, _AV) = range(14)
PACK_ORDER = (_XBC, _Z, _GQ, _GK, _GV, _GR, _SQ, _SK, _SV, _AQ, _AK, _AV, _DT, _GA)
PACK_WIDTH = 2944
C_SSD, C_GLA, C_SWA, C_AQ, C_AK, C_AV, C_SMALL = 0, 1024, 1792, 2304, 2560, 2688, 2816
SMALL_DT, SMALL_GA = 0, 8


def _cparams(*sem):
    return pltpu.CompilerParams(dimension_semantics=sem, vmem_limit_bytes=VMEM_LIMIT)


def _dot(a, b):
    return jnp.dot(a, b, preferred_element_type=f32)


def _dot_nt(a, b):
    return lax.dot_general(a, b, (((1,), (1,)), ((), ())), preferred_element_type=f32)


def _dot_tn(a, b):
    return lax.dot_general(a, b, (((0,), (0,)), ((), ())), preferred_element_type=f32)


def _split(a):
    hi = a.astype(bf16)
    lo = (a - hi.astype(f32)).astype(bf16)
    return hi, lo


def _dot_split_lhs(a, b):
    hi, lo = _split(a)
    return _dot(hi, b) + _dot(lo, b)


def _dot_split_rhs(t, x):
    hi, lo = _split(x)
    return _dot(t, hi) + _dot(t, lo)


def _rms(x, w):
    return x * lax.rsqrt(jnp.mean(x * x, axis=-1, keepdims=True) + EPS) * w


def _silu(x):
    return x / (1.0 + jnp.exp(-x))


def _softplus(x):
    return jnp.maximum(x, 0.0) + jnp.log(1.0 + jnp.exp(-jnp.abs(x)))


def _log_sigmoid(x):
    return jnp.minimum(x, 0.0) - jnp.log(1.0 + jnp.exp(-jnp.abs(x)))


def _tri(n, rev):
    r = lax.broadcasted_iota(jnp.int32, (n, n), 0)
    c = lax.broadcasted_iota(jnp.int32, (n, n), 1)
    return (r <= c) if rev else (r >= c)


def _valid_rows(tile, n):
    rows = lax.broadcasted_iota(jnp.int32, (n, 1), 0)
    return jnp.logical_or(tile > 0, rows >= TILE - N_META)


def _inproj_body(h_ref, nw_ref, w_ref, cos_ref, sin_ref, rot_ref, gavg_ref, qnw_ref, knw_ref,
                 ssd_ref, small_ref, gla_ref, sk_ref, k_ref, q_ref, v_ref, sq_ref, sv_ref):
    u = _rms(h_ref[...], nw_ref[...]).astype(bf16)

    def mm(lo, hi):
        return _dot(u, w_ref[:, lo:hi])

    ssd_ref[...] = mm(C_SSD, C_GLA)
    gla_ref[...] = mm(C_GLA, C_SWA)
    small_ref[...] = mm(C_SMALL, PACK_WIDTH)
    v_ref[...] = mm(C_AV, C_SMALL).T.astype(bf16)

    sk_ref[...] = mm(C_SWA + 256, C_SWA + 384).astype(bf16)
    sv_ref[...] = mm(C_SWA + 384, C_AQ).T.astype(bf16)
    sqt = (mm(C_SWA, C_SWA + 256) * (HEAD_DIM ** -0.5 * LOG2E)).T.astype(bf16)
    sq_ref[...] = jnp.zeros_like(sq_ref)
    for t in range(sq_ref.shape[0]):
        for hh in range(4):
            lo = HEAD_DIM * (hh // 2)
            sq_ref[t, lo:lo + HEAD_DIM, TILE * hh:TILE * (hh + 1)] = (
                sqt[HEAD_DIM * hh:HEAD_DIM * (hh + 1), TILE * t:TILE * (t + 1)])

    def norm_rope(t, w, width):
        ms = _dot_split_lhs(t * t, gavg_ref[:width, :width])
        tn = t * lax.rsqrt(ms + EPS) * w
        tr = _dot(tn.astype(bf16), rot_ref[:width, :width])
        return tn * cos_ref[:, :width] + tr * sin_ref[:, :width]

    q = norm_rope(mm(C_AQ, C_AK), qnw_ref[...], 256) * (HEAD_DIM ** -0.5 * LOG2E)
    q_ref[...] = q.T.astype(bf16)
    k_ref[...] = norm_rope(mm(C_AK, C_AV), knw_ref[...], 128).astype(bf16)


def _inproj(h, nw, w, cos, sin, rot, gavg, qnw, knw, bsz, lp):
    n = h.shape[0]
    d = h.shape[1]
    tm = TILE * _largest_divisor(lp // TILE, 3)
    per = lp // tm
    row = lambda b, i: (b * per + i, 0)
    const = lambda b, i: (0, 0)
    tab = lambda b, i: (i, 0)
    outs = ((1024, f32), (128, f32), (768, f32), (128, bf16), (128, bf16))
    slab = lambda b, i: (b * per + i, 0, 0)
    sub = tm // TILE
    return pl.pallas_call(
        _inproj_body,
        grid=(bsz, per),
        in_specs=[pl.BlockSpec((tm, d), row), pl.BlockSpec((1, d), const),
                  pl.BlockSpec((d, PACK_WIDTH), const),
                  pl.BlockSpec((tm, 256), tab), pl.BlockSpec((tm, 256), tab),
                  pl.BlockSpec((256, 256), const), pl.BlockSpec((256, 256), const),
                  pl.BlockSpec((1, 256), const), pl.BlockSpec((1, 128), const)],
        out_specs=([pl.BlockSpec((tm, c), row) for c, _ in outs]
                   + [pl.BlockSpec((None, 256, tm), slab), pl.BlockSpec((None, 128, tm), slab),
                      pl.BlockSpec((sub, 128, 4 * TILE), slab), pl.BlockSpec((None, 128, tm), slab)]),
        out_shape=([jax.ShapeDtypeStruct((n, c), t) for c, t in outs]
                   + [jax.ShapeDtypeStruct((n // tm, 256, tm), bf16),
                      jax.ShapeDtypeStruct((n // tm, 128, tm), bf16),
                      jax.ShapeDtypeStruct((n // TILE, 128, 4 * TILE), bf16),
                      jax.ShapeDtypeStruct((n // tm, 128, tm), bf16)]),
        compiler_params=_cparams("parallel", "parallel"),
        name="inproj",
    )(h, nw, w, cos, sin, rot, gavg, qnw, knw)


def _largest_divisor(n, cap):
    return max(k for k in range(1, cap + 1) if n % k == 0)


def _for_each_batch(chain, batched, n_scratch):
    def body(*refs):
        nb = next(r.shape[0] for r, flag in zip(refs, batched) if flag)
        io, scratch = refs[:len(batched)], refs[len(batched):]
        assert len(scratch) == n_scratch * nb
        for b in range(nb):
            chain(*[r.at[b] if flag else r for r, flag in zip(io, batched)],
                  *[scratch[k * nb + b] for k in range(n_scratch)])
    return body


def _per_batch_scratch(nb, *shapes):
    return [pltpu.VMEM(shape, dtype) for shape, dtype in shapes for _ in range(nb)]


def _ssd_body(rev, nt, *refs):
    if rev:
        (cur_ref, prev_ref, next_ref, small_ref, convw_ref, convb_ref, dtb_ref, alog_ref,
         yf_ref, z_ref, normw_ref, out_ref, ext_ref, state_ref, y_ref) = refs
    else:
        (cur_ref, prev_ref, next_ref, small_ref, convw_ref, convb_ref, dtb_ref, alog_ref,
         dskip_ref, out_ref, ext_ref, state_ref, y_ref) = refs
    c = pl.program_id(0)
    tile = (nt - 1 - c) if rev else c
    valid = _valid_rows(tile, TILE)
    ext_ref[0:CONV_HALO, :] = jnp.where(tile > 0, prev_ref[...], 0.0)
    ext_ref[CONV_HALO:CONV_HALO + TILE, :] = jnp.where(valid, cur_ref[...], 0.0)
    ext_ref[CONV_HALO + TILE:, :] = jnp.where(tile < nt - 1, next_ref[...], 0.0)
    acc = jnp.zeros((TILE, SSD_CONV_DIM), f32) + convb_ref[...]
    first = CONV_HALO - (SSD_CONV - 1) // 2
    for k in range(SSD_CONV):
        acc = acc + convw_ref[k:k + 1, :] * ext_ref[first + k:first + k + TILE, :]
    xbc = jnp.where(valid, _silu(acc), 0.0)
    xs = xbc[:, :SSD_INNER]

    dt = jnp.where(valid, _softplus(small_ref[...] + dtb_ref[...]), 0.0)
    dta = dt * (-jnp.exp(alog_ref[...]))
    cum = _dot_split_rhs(_tri(TILE, rev).astype(bf16), dta)
    cum_t = cum.T
    tot = jnp.sum(dta, axis=0, keepdims=True)
    causal = _tri(TILE, rev)

    cb = []
    for g in range(2):
        bg = xbc[:, SSD_INNER + SSD_STATE * g:SSD_INNER + SSD_STATE * (g + 1)].astype(bf16)
        cg = xbc[:, SSD_INNER + 2 * SSD_STATE + SSD_STATE * g:
                 SSD_INNER + 2 * SSD_STATE + SSD_STATE * (g + 1)].astype(bf16)
        cb.append((bg, cg, _dot_nt(cg, bg)))

    for hh in range(SSD_HEADS):
        col = SMALL_DT + hh + (SSD_HEADS if rev else 0)
        bg, cg, cbg = cb[hh // 2]
        a_col = cum[:, col:col + 1]
        a_row = cum_t[col:col + 1, :]
        decay = jnp.where(causal, jnp.exp(a_col - a_row), 0.0)
        xd = (xs[:, SSD_HEAD_DIM * hh:SSD_HEAD_DIM * (hh + 1)] * dt[:, col:col + 1]).astype(bf16)
        s_in = jnp.where(c == 0, 0.0, state_ref[hh])
        y = _dot((cbg * decay).astype(bf16), xd) + _dot(cg, s_in.astype(bf16)) * jnp.exp(a_col)
        tot_h = tot[:, col:col + 1]
        bw = (bg.astype(f32) * jnp.exp(tot_h - a_col)).astype(bf16)
        state_ref[hh] = jnp.exp(tot_h) * s_in + _dot_tn(bw, xd)
        y_ref[:, SSD_HEAD_DIM * hh:SSD_HEAD_DIM * (hh + 1)] = y

    if rev:
        y = (yf_ref[...] + y_ref[...]) * _silu(z_ref[...])
        out_ref[...] = jnp.where(valid, _rms(y, normw_ref[...]), 0.0).astype(out_ref.dtype)
    else:
        out_ref[...] = y_ref[...] + dskip_ref[...] * xs


def _ssd(rev, o_ssd, o_small, convw, convb, dtb, alog, extra, bsz, lp):
    n = o_ssd.shape[0]
    nt = lp // TILE
    hb = TILE // CONV_HALO
    o_ssd = o_ssd.reshape(bsz, lp, -1)
    o_small = o_small.reshape(bsz, lp, -1)

    def tile_of(c):
        return (nt - 1 - c) if rev else c

    cur = lambda c: (0, tile_of(c), 0)
    prev = lambda c: (0, jnp.maximum(tile_of(c) * hb - 1, 0), 0)
    nxt = lambda c: (0, jnp.minimum((tile_of(c) + 1) * hb, lp // CONV_HALO - 1), 0)
    zcol = lambda c: (0, tile_of(c), SSD_CONV_DIM // SSD_INNER)
    const = lambda c: (0, 0)
    in_specs = [pl.BlockSpec((bsz, TILE, SSD_CONV_DIM), cur), pl.BlockSpec((bsz, CONV_HALO, SSD_CONV_DIM), prev),
                pl.BlockSpec((bsz, CONV_HALO, SSD_CONV_DIM), nxt), pl.BlockSpec((bsz, TILE, 128), cur),
                pl.BlockSpec((8, SSD_CONV_DIM), const), pl.BlockSpec((1, SSD_CONV_DIM), const),
                pl.BlockSpec((1, 128), const), pl.BlockSpec((1, 128), const)]
    args = [o_ssd, o_ssd, o_ssd, o_small, convw, convb, dtb, alog]
    batched = [True] * 4 + [False] * 4
    if rev:
        yf, normw = extra
        in_specs += [pl.BlockSpec((bsz, TILE, SSD_INNER), cur), pl.BlockSpec((bsz, TILE, SSD_INNER), zcol),
                     pl.BlockSpec((1, SSD_INNER), const)]
        args += [yf.reshape(bsz, lp, -1), o_ssd, normw]
        batched += [True, True, False]
        out_dtype = bf16
    else:
        in_specs += [pl.BlockSpec((1, SSD_INNER), const)]
        args += [extra]
        batched += [False]
        out_dtype = f32
    batched += [True]
    out = pl.pallas_call(
        _for_each_batch(functools.partial(_ssd_body, rev, nt), batched, 3),
        grid=(nt,),
        in_specs=in_specs,
        out_specs=pl.BlockSpec((bsz, TILE, SSD_INNER), cur),
        out_shape=jax.ShapeDtypeStruct((bsz, lp, SSD_INNER), out_dtype),
        scratch_shapes=_per_batch_scratch(bsz, ((TILE + 2 * CONV_HALO, SSD_CONV_DIM), f32),
                                          ((SSD_HEADS, SSD_STATE, SSD_HEAD_DIM), f32),
                                          ((TILE, SSD_INNER), f32)),
        compiler_params=_cparams("arbitrary"),
        name="ssd_rev" if rev else "ssd_fwd",
    )(*args)
    return out.reshape(n, SSD_INNER)


def _gla_body(rev, nt, *refs):
    if rev:
        x_ref, small_ref, wg_ref, gb_ref, of_ref, normw_ref, gavg_ref, out_ref, st_ref = refs
    else:
        x_ref, small_ref, wg_ref, gb_ref, out_ref, st_ref = refs
    c = pl.program_id(0)
    tile = (nt - 1 - c) if rev else c
    nb = x_ref.shape[0]
    n = GLA_CHUNK
    valid = _valid_rows(tile, TILE)

    pre = _dot(small_ref[...].reshape(nb * TILE, 128).astype(bf16), wg_ref[...]) + gb_ref[...]
    g_all = _log_sigmoid(pre) / GLA_GATE_NORM
    gs = [jnp.where(valid, g_all[TILE * b:TILE * (b + 1)], 0.0) for b in range(nb)]
    row = lax.broadcasted_iota(jnp.int32, (TILE, TILE), 0)
    col = lax.broadcasted_iota(jnp.int32, (TILE, TILE), 1)
    same_chunk = (row // n) == (col // n)
    cum_mat = jnp.where(jnp.logical_and(same_chunk, (row <= col) if rev else (row >= col)), 1.0, 0.0)
    bc_all = _dot_split_rhs(cum_mat.astype(bf16), jnp.concatenate(gs, axis=1))

    first = slice(n, 2 * n) if rev else slice(0, n)
    second = slice(0, n) if rev else slice(n, 2 * n)
    rows = lax.broadcasted_iota(jnp.int32, (TILE, 1), 0)
    in_first = (rows >= n) if rev else (rows < n)
    tri = _tri(n, rev)
    tri4 = jnp.concatenate([tri] * GLA_HEADS, axis=0)
    lane_head = lax.broadcasted_iota(jnp.int32, (n, GLA_KEY), 1) // GLA_DK
    out_head = lax.broadcasted_iota(jnp.int32, (n, GLA_VAL), 1) // GLA_DV
    blockdiag = (lax.broadcasted_iota(jnp.int32, (GLA_VAL, GLA_KEY), 0) // GLA_DV
                 == lax.broadcasted_iota(jnp.int32, (GLA_VAL, GLA_KEY), 1) // GLA_DK)

    def stack_heads(a):
        return jnp.concatenate([jnp.where(lane_head == hh, a, 0.0) for hh in range(GLA_HEADS)],
                               axis=0).astype(bf16)

    def own_head(r):
        out = jnp.zeros((n, GLA_VAL), f32)
        for hh in range(GLA_HEADS):
            out = jnp.where(out_head == hh, r[n * hh:n * (hh + 1)], out)
        return out

    for b in range(nb):
        x = x_ref[b]
        g = gs[b]
        bc = bc_all[:, 128 * b:128 * (b + 1)]
        q = jnp.where(valid, x[:, :GLA_KEY], 0.0) * GLA_DK ** -0.5
        k = jnp.where(valid, x[:, GLA_KEY:2 * GLA_KEY], 0.0)
        v = jnp.where(valid, x[:, 2 * GLA_KEY:2 * GLA_KEY + GLA_VAL], 0.0).astype(bf16)
        bl_first = jnp.sum(g[first], axis=0, keepdims=True)
        bl_second = jnp.sum(g[second], axis=0, keepdims=True)
        qt = q * jnp.exp(bc)
        kt = k * jnp.exp(-bc)
        kw = k * jnp.exp(jnp.where(in_first, bl_first, bl_second) - bc)

        att_f = jnp.where(tri4, _dot_nt(stack_heads(qt[first]), kt[first].astype(bf16)), 0.0)
        o_f = own_head(_dot(att_f.astype(bf16), v[first]))
        keys = jnp.concatenate([kw[first], kt[second]], axis=0).astype(bf16)
        vals = jnp.concatenate([v[first], v[second]], axis=0)
        att_s = _dot_nt(stack_heads(qt[second]), keys)
        att_s = jnp.concatenate([att_s[:, :n], jnp.where(tri4, att_s[:, n:], 0.0)], axis=1)
        o_s = own_head(_dot(att_s.astype(bf16), vals))

        st = jnp.where(c == 0, 0.0, st_ref[b])
        q_in = qt * jnp.exp(jnp.where(in_first, 0.0, bl_first))
        o = _dot_nt(q_in.astype(bf16), st.astype(bf16)) + jnp.concatenate(
            [o_s, o_f] if rev else [o_f, o_s], axis=0)
        k_out = (kw * jnp.exp(jnp.where(in_first, bl_second, 0.0))).astype(bf16)
        st_ref[b] = st * jnp.exp(bl_first + bl_second) + jnp.where(blockdiag, _dot_tn(v, k_out), 0.0)

        if rev:
            o = of_ref[b] + o
            ms = _dot_split_lhs(o * o, gavg_ref[...])
            on = o * lax.rsqrt(ms + EPS) * normw_ref[...]
            r = x[:, 2 * GLA_KEY + GLA_VAL:]
            out_ref[b] = jnp.where(valid, on * _silu(r), 0.0).astype(out_ref.dtype)
        else:
            out_ref[b] = o


def _gla(rev, o_gla, o_small, wg, gb, extra, bsz, lp):
    n = o_gla.shape[0]
    nt = lp // TILE
    cur = lambda c: (0, (nt - 1 - c) if rev else c, 0)
    const = lambda c: (0, 0)
    in_specs = [pl.BlockSpec((bsz, TILE, 768), cur), pl.BlockSpec((bsz, TILE, 128), cur),
                pl.BlockSpec((128, GLA_KEY), const), pl.BlockSpec((1, GLA_KEY), const)]
    args = [o_gla.reshape(bsz, lp, -1), o_small.reshape(bsz, lp, -1), wg, gb]
    if rev:
        of, normw, gavg = extra
        in_specs += [pl.BlockSpec((bsz, TILE, GLA_VAL), cur), pl.BlockSpec((1, GLA_VAL), const),
                     pl.BlockSpec((GLA_VAL, GLA_VAL), const)]
        args += [of.reshape(bsz, lp, -1), normw, gavg]
    out = pl.pallas_call(
        functools.partial(_gla_body, rev, nt),
        grid=(nt,),
        in_specs=in_specs,
        out_specs=pl.BlockSpec((bsz, TILE, GLA_VAL), cur),
        out_shape=jax.ShapeDtypeStruct((bsz, lp, GLA_VAL), bf16 if rev else f32),
        scratch_shapes=[pltpu.VMEM((bsz, GLA_VAL, GLA_KEY), f32)],
        compiler_params=_cparams("arbitrary"),
        name="gla_rev" if rev else "gla_fwd",
    )(*args)
    return out.reshape(n, GLA_VAL)


def _swa_body(nt, qp_ref, kp_ref, kc_ref, kn_ref, km_ref, vp_ref, vc_ref, vn_ref, vm_ref,
              bias_ref, bmeta_ref, sink_ref, out_ref):
    c = pl.program_id(0)
    krow = lax.broadcasted_iota(jnp.int32, (3 * TILE, 1), 0)
    pen = jnp.where(krow < TILE, jnp.where(c >= 2, 0.0, NEG),
                    jnp.where(krow < 2 * TILE, jnp.where(c >= 1, 0.0, NEG),
                              jnp.where(c <= nt - 2, 0.0, NEG)))
    qp = qp_ref[...]
    kcat = jnp.concatenate([kp_ref[...], kc_ref[...], kn_ref[...]], axis=0)
    s = _dot(kcat, qp) + bias_ref[...] + pen
    sm = _dot(km_ref[...], qp) + bmeta_ref[...]
    sk = sink_ref[...]
    m = jnp.maximum(jnp.maximum(jnp.max(s, axis=0, keepdims=True),
                                jnp.max(sm, axis=0, keepdims=True)), sk)
    p = jnp.exp2(s - m)
    pm = jnp.exp2(sm - m)
    inv = 1.0 / (jnp.sum(p, axis=0, keepdims=True) + jnp.sum(pm, axis=0, keepdims=True)
                 + jnp.exp2(sk - m))
    pb = p.astype(bf16)
    pmb = jnp.concatenate([jnp.zeros((TILE - N_META, 4 * TILE), bf16), pm.astype(bf16)], axis=0)
    vcat = jnp.concatenate([vp_ref[...], vc_ref[...], vn_ref[...]], axis=1)
    heads = []
    for g in range(2):
        rows = slice(HEAD_DIM * g, HEAD_DIM * (g + 1))
        cols = slice(2 * TILE * g, 2 * TILE * (g + 1))
        pv = (_dot(vcat[rows, :], pb[:, cols]) + _dot(vm_ref[rows, :], pmb[:, cols])) * inv[:, cols]
        heads += [pv[:, :TILE], pv[:, TILE:]]
    o = jnp.concatenate(heads, axis=0).T
    out_ref[...] = jnp.where(_valid_rows(c, TILE), o, 0.0).astype(out_ref.dtype)


def _swa(qp, k, vt, bias_band, bias_meta, sink, bsz, lp):
    n = k.shape[0]
    nt = lp // TILE
    per, _, tm = vt.shape[0] // bsz, vt.shape[1], vt.shape[2]
    sub = tm // TILE
    qp = qp.reshape(bsz, nt, 128, 4 * TILE)
    k = k.reshape(bsz, lp, 128)
    vt = vt.reshape(bsz, per, 128, tm)
    prev = lambda c: jnp.maximum(c - 1, 0)
    nxt = lambda c: jnp.minimum(c + 1, nt - 1)
    kspec = lambda tile: pl.BlockSpec((bsz, TILE, 128), lambda c: (0, tile(c), 0))
    vspec = lambda tile: pl.BlockSpec((bsz, None, 128, TILE), lambda c: (0, tile(c) // sub, 0, tile(c) % sub))
    same = lambda c: c
    first = lambda c: 0
    out = pl.pallas_call(
        _for_each_batch(functools.partial(_swa_body, nt), [True] * 9 + [False] * 3 + [True], 0),
        grid=(nt,),
        in_specs=[pl.BlockSpec((bsz, None, 128, 4 * TILE), lambda c: (0, c, 0, 0)),
                  kspec(prev), kspec(same), kspec(nxt),
                  pl.BlockSpec((bsz, N_META, 128), lambda c: (0, TILE // N_META - 1, 0)),
                  vspec(prev), vspec(same), vspec(nxt), vspec(first),
                  pl.BlockSpec((3 * TILE, 4 * TILE), lambda c: (0, 0)),
                  pl.BlockSpec((None, N_META, 4 * TILE), lambda c: (c, 0, 0)),
                  pl.BlockSpec((1, 4 * TILE), lambda c: (0, 0))],
        out_specs=pl.BlockSpec((bsz, TILE, 256), lambda c: (0, c, 0)),
        out_shape=jax.ShapeDtypeStruct((bsz, lp, 256), bf16),
        compiler_params=_cparams("arbitrary"),
        name="swa",
    )(qp, k, k, k, k, vt, vt, vt, vt, bias_band, bias_meta, sink)
    return out.reshape(n, 256)


def _flash_body(nk, tq, tk, qt_ref, k_ref, vt_ref, out_ref, qpad_ref, m_ref, l_ref, acc_ref):
    i = pl.program_id(1)
    krow = lax.broadcasted_iota(jnp.int32, (tk, 1), 0)
    qpad_ref[...] = jnp.zeros_like(qpad_ref)
    for hh in range(4):
        lo = HEAD_DIM * (hh // 2)
        qpad_ref[lo:lo + HEAD_DIM, tq * hh:tq * (hh + 1)] = qt_ref[HEAD_DIM * hh:HEAD_DIM * (hh + 1), :]
    m_ref[...] = jnp.full_like(m_ref, NEG)
    l_ref[...] = jnp.zeros_like(l_ref)
    acc_ref[...] = jnp.zeros_like(acc_ref)

    def step(j, first):
        s = _dot(k_ref[pl.ds(pl.multiple_of(j * tk, tk), tk), :], qpad_ref[...])
        if first:
            s = jnp.where(krow >= TILE - N_META, s, NEG)
        m_old = m_ref[...]
        m_new = jnp.maximum(m_old, jnp.max(s, axis=0, keepdims=True))
        alpha = jnp.exp2(m_old - m_new)
        p = jnp.exp2(s - m_new)
        l_ref[...] = alpha * l_ref[...] + jnp.sum(p, axis=0, keepdims=True)
        m_ref[...] = m_new
        pb = p.astype(bf16)
        for g in range(2):
            pv = _dot(vt_ref[j, HEAD_DIM * g:HEAD_DIM * (g + 1), :], pb[:, 2 * g * tq:(2 * g + 2) * tq])
            for r in range(2):
                hh = 2 * g + r
                rows = slice(HEAD_DIM * hh, HEAD_DIM * (hh + 1))
                acc_ref[rows, :] = (alpha[:, tq * hh:tq * (hh + 1)] * acc_ref[rows, :]
                                    + pv[:, tq * r:tq * (r + 1)])

    step(0, True)

    def body(j, carry):
        step(j, False)
        return carry

    lax.fori_loop(1, nk, body, 0)
    linv = 1.0 / l_ref[...]
    for hh in range(4):
        rows = slice(HEAD_DIM * hh, HEAD_DIM * (hh + 1))
        acc_ref[rows, :] = acc_ref[rows, :] * linv[:, tq * hh:tq * (hh + 1)]
    rows = lax.broadcasted_iota(jnp.int32, (tq, 1), 0)
    valid = jnp.logical_or(i > 0, rows >= TILE - N_META)
    out_ref[...] = jnp.where(valid, acc_ref[...].T, 0.0).astype(out_ref.dtype)


def _flash(qt, k, vt, bsz, lp):
    n = k.shape[0]
    tq = qt.shape[2]
    per = lp // tq
    return pl.pallas_call(
        functools.partial(_flash_body, per, tq, tq),
        grid=(bsz, per),
        in_specs=[pl.BlockSpec((None, 256, tq), lambda b, i: (b * per + i, 0, 0)),
                  pl.BlockSpec((lp, 128), lambda b, i: (b, 0)),
                  pl.BlockSpec((per, 128, tq), lambda b, i: (b, 0, 0))],
        out_specs=pl.BlockSpec((tq, 256), lambda b, i: (b * per + i, 0)),
        out_shape=jax.ShapeDtypeStruct((n, 256), bf16),
        scratch_shapes=[pltpu.VMEM((128, 4 * tq), bf16), pltpu.VMEM((1, 4 * tq), f32),
                        pltpu.VMEM((1, 4 * tq), f32), pltpu.VMEM((256, tq), f32)],
        compiler_params=_cparams("parallel", "parallel"),
        name="gqa_full",
    )(qt, k, vt)


def _outproj_body(y0_ref, y1_ref, y2_ref, y3_ref, h_ref, w_ref, out_ref):
    acc = h_ref[...]
    for j, y_ref in enumerate((y0_ref, y1_ref, y2_ref, y3_ref)):
        acc = acc + _dot(y_ref[...], w_ref[256 * j:256 * (j + 1), :])
    out_ref[...] = acc


def _outproj(ys, h, w):
    n, d = h.shape
    tm = TILE * _largest_divisor(n // TILE, 4)
    row = lambda i: (i, 0)
    return pl.pallas_call(
        _outproj_body,
        grid=(n // tm,),
        in_specs=[pl.BlockSpec((tm, 256), row)] * 4 + [pl.BlockSpec((tm, d), row),
                                                      pl.BlockSpec((d, d), lambda i: (0, 0))],
        out_specs=pl.BlockSpec((tm, d), row),
        out_shape=jax.ShapeDtypeStruct((n, d), f32),
        compiler_params=_cparams("parallel"),
        name="outproj",
    )(*ys, h, w)


def _ffn_body(h_ref, nw_ref, wg_ref, wu_ref, wd_ref, out_ref, u_ref, acc_ref):
    j = pl.program_id(1)

    @pl.when(j == 0)
    def _():
        u_ref[...] = _rms(h_ref[...], nw_ref[...]).astype(bf16)
        acc_ref[...] = jnp.zeros_like(acc_ref)

    u = u_ref[...]
    t = _silu(_dot(u, wg_ref[...])) * _dot(u, wu_ref[...])
    acc_ref[...] += _dot(t.astype(bf16), wd_ref[...])

    @pl.when(j == pl.num_programs(1) - 1)
    def _():
        out_ref[...] = h_ref[...] + acc_ref[...]


def _ffn(h, nw, wg, wu, wd):
    n, d = h.shape
    ff = wg.shape[1]
    tm = TILE * _largest_divisor(n // TILE, 4)
    tf = 128 * _largest_divisor(ff // 128, 11)
    row = lambda i, j: (i, 0)
    return pl.pallas_call(
        _ffn_body,
        grid=(n // tm, ff // tf),
        in_specs=[pl.BlockSpec((tm, d), row), pl.BlockSpec((1, d), lambda i, j: (0, 0)),
                  pl.BlockSpec((d, tf), lambda i, j: (0, j)), pl.BlockSpec((d, tf), lambda i, j: (0, j)),
                  pl.BlockSpec((tf, d), lambda i, j: (j, 0))],
        out_specs=pl.BlockSpec((tm, d), row),
        out_shape=jax.ShapeDtypeStruct((n, d), f32),
        scratch_shapes=[pltpu.VMEM((tm, d), bf16), pltpu.VMEM((tm, d), f32)],
        compiler_params=_cparams("parallel", "arbitrary"),
        name="ffn",
    )(h, nw, wg, wu, wd)


MOE_TM = 1024
ROUTE_E, ROUTE_RANK, ROUTE_GATE, ROUTE_W = 0, 2, 4, 8


def _router_body(h_ref, nw_ref, r_ref, route_ref, cnt_ref, base_ref):
    @pl.when(pl.program_id(0) == 0)
    def _():
        base_ref[...] = jnp.zeros_like(base_ref)

    u = _rms(h_ref[...], nw_ref[...])
    u_hi, u_lo = _split(u)
    r_hi, r_lo = _split(r_ref[...])
    logits = _dot(u_hi, r_hi) + _dot(u_lo, r_hi) + _dot(u_hi, r_lo)
    tm = logits.shape[0]
    lane = lax.broadcasted_iota(jnp.int32, logits.shape, 1)
    logits = jnp.where(lane < N_EXPERTS, logits, NEG)
    m1 = jnp.max(logits, axis=-1, keepdims=True)
    i1 = jnp.min(jnp.where(logits == m1, lane, 128), axis=-1, keepdims=True)
    rest = jnp.where(lane == i1, NEG, logits)
    m2 = jnp.max(rest, axis=-1, keepdims=True)
    i2 = jnp.min(jnp.where(rest == m2, lane, 128), axis=-1, keepdims=True)
    e2 = jnp.exp(m2 - m1)
    g1 = 1.0 / (1.0 + e2)
    g2 = e2 * g1

    sel1 = lane == i1
    sel2 = lane == i2
    onehot = jnp.where(sel1, 1.0, jnp.where(sel2, 1.0, 0.0))
    strict = jnp.where(lax.broadcasted_iota(jnp.int32, (tm, tm), 0)
                       > lax.broadcasted_iota(jnp.int32, (tm, tm), 1), 1.0, 0.0).astype(bf16)
    before = _dot(strict, onehot.astype(bf16)) + base_ref[0:1, :]
    r1 = jnp.sum(jnp.where(sel1, before, 0.0), axis=-1, keepdims=True)
    r2 = jnp.sum(jnp.where(sel2, before, 0.0), axis=-1, keepdims=True)
    route = jnp.zeros(logits.shape, f32)
    for k, val in enumerate((i1.astype(f32), i2.astype(f32), r1, r2, g1, g2)):
        route = jnp.where(lane == k, val, route)
    route_ref[...] = route[:, :ROUTE_W]
    base_ref[0:1, :] = base_ref[0:1, :] + jnp.sum(onehot, axis=0, keepdims=True)
    cnt_ref[...] = base_ref[...]


def _router(h, nw, router):
    n, d = h.shape
    tm = TILE * _largest_divisor(n // TILE, 8)
    return pl.pallas_call(
        _router_body,
        grid=(n // tm,),
        in_specs=[pl.BlockSpec((tm, d), lambda i: (i, 0)), pl.BlockSpec((1, d), lambda i: (0, 0)),
                  pl.BlockSpec((d, 128), lambda i: (0, 0))],
        out_specs=[pl.BlockSpec((tm, ROUTE_W), lambda i: (i, 0)), pl.BlockSpec((8, 128), lambda i: (0, 0))],
        out_shape=[jax.ShapeDtypeStruct((n, ROUTE_W), f32), jax.ShapeDtypeStruct((8, 128), f32)],
        scratch_shapes=[pltpu.VMEM((8, 128), f32)],
        compiler_params=_cparams("arbitrary"),
        name="moe_router",
    )(h, nw, router)


LANES = 128
ROW_DMA_UNROLL = 8


def _row_copy(src_ref, src_row, dst_ref, dst_row, sem):
    return pltpu.make_async_copy(src_ref.at[src_row], dst_ref.at[dst_row], sem)


def _to_slabs(dst_ref, val):
    for k in range(val.shape[1] // LANES):
        dst_ref[:, k, :] = val[:, LANES * k:LANES * (k + 1)]


def _from_slabs(src_ref, k):
    return src_ref[:, k, :]


def _dispatch_body(fill_ref, h_ref, nw_ref, dest_ref, xs_ref, u_ref, idx_ref, zero_ref,
                   sem_idx, sem_row, sem_fill):
    i = pl.program_id(0)
    tt = u_ref.shape[0]

    @pl.when(i == 0)
    def _():
        zero_ref[...] = jnp.zeros_like(zero_ref)
        fills = [pltpu.make_async_copy(zero_ref, xs_ref.at[pl.ds(fill_ref[e], MOE_TM)], sem_fill)
                 for e in range(N_EXPERTS)]
        for cp in fills:
            cp.start()
        for cp in fills:
            cp.wait()
        last = xs_ref.shape[0] // MOE_TM - 1
        for j in range(last - N_EXPERTS, last + 1):
            @pl.when(j >= fill_ref[N_EXPERTS])
            def _():
                cp = pltpu.make_async_copy(zero_ref, xs_ref.at[pl.ds(j * MOE_TM, MOE_TM)], sem_fill)
                cp.start()
                cp.wait()

    idx_copy = pltpu.make_async_copy(dest_ref.at[pl.ds(i * 2 * tt, 2 * tt)], idx_ref, sem_idx)
    idx_copy.start()
    _to_slabs(u_ref, _rms(h_ref[...], nw_ref[...]))
    idx_copy.wait()

    def issue(r, carry):
        _row_copy(u_ref, r, xs_ref, idx_ref[r], sem_row).start()
        _row_copy(u_ref, r, xs_ref, idx_ref[tt + r], sem_row).start()
        return carry

    lax.fori_loop(0, tt, issue, 0, unroll=ROW_DMA_UNROLL)
    for _ in range(2):
        pltpu.make_async_copy(u_ref, xs_ref.at[pl.ds(0, tt)], sem_row).wait()


def _dispatch(h, nw, dest_flat, fill_rows, rows, tt):
    n, d = h.shape
    return pl.pallas_call(
        _dispatch_body,
        grid_spec=pltpu.PrefetchScalarGridSpec(
            num_scalar_prefetch=1,
            grid=(n // tt,),
            in_specs=[pl.BlockSpec((tt, d), lambda i, fr: (i, 0)), pl.BlockSpec((1, d), lambda i, fr: (0, 0)),
                      pl.BlockSpec(memory_space=pl.ANY)],
            out_specs=pl.BlockSpec(memory_space=pl.ANY),
            scratch_shapes=[pltpu.VMEM((tt, d // LANES, LANES), f32), pltpu.SMEM((2 * tt,), jnp.int32),
                            pltpu.VMEM((MOE_TM, d // LANES, LANES), f32), pltpu.SemaphoreType.DMA(()),
                            pltpu.SemaphoreType.DMA(()), pltpu.SemaphoreType.DMA(())]),
        out_shape=jax.ShapeDtypeStruct((rows, d // LANES, LANES), f32),
        compiler_params=_cparams("arbitrary"),
        name="moe_dispatch",
    )(fill_rows, h, nw, dest_flat)


def _experts_body(te_ref, nu_ref, x_ref, wg_ref, wu_ref, wd_ref, y_ref, xb_ref, acc_ref):
    j = pl.program_id(0)
    f = pl.program_id(1)

    @pl.when(j < nu_ref[0])
    def _():
        @pl.when(f == 0)
        def _():
            for k in range(x_ref.shape[1]):
                xb_ref[:, LANES * k:LANES * (k + 1)] = _from_slabs(x_ref, k).astype(bf16)
            acc_ref[...] = jnp.zeros_like(acc_ref)

        x = xb_ref[...]
        t = _silu(_dot(x, wg_ref[...])) * _dot(x, wu_ref[...])
        acc_ref[...] += _dot(t.astype(bf16), wd_ref[...])

        @pl.when(f == pl.num_programs(1) - 1)
        def _():
            _to_slabs(y_ref, acc_ref[...])

    @pl.when(jnp.logical_and(j >= nu_ref[0], f == pl.num_programs(1) - 1))
    def _():
        y_ref[...] = jnp.zeros_like(y_ref)


def _experts(xs, tile_expert, n_used, wg, wu, wd, n_tiles):
    d = wg.shape[1]
    slab = (MOE_TM, d // LANES, LANES)
    ff = wg.shape[2]
    tf = 128 * _largest_divisor(ff // 128, 4)
    nf = ff // tf
    tile = lambda j, nu: jnp.minimum(j, nu[0] - 1)
    chunk = lambda j, f, nu: jnp.where(j < nu[0], f, nf - 1)
    return pl.pallas_call(
        _experts_body,
        grid_spec=pltpu.PrefetchScalarGridSpec(
            num_scalar_prefetch=2,
            grid=(n_tiles, nf),
            in_specs=[pl.BlockSpec(slab, lambda j, f, te, nu: (tile(j, nu), 0, 0)),
                      pl.BlockSpec((None, d, tf), lambda j, f, te, nu: (te[tile(j, nu)], 0, chunk(j, f, nu))),
                      pl.BlockSpec((None, d, tf), lambda j, f, te, nu: (te[tile(j, nu)], 0, chunk(j, f, nu))),
                      pl.BlockSpec((None, tf, d), lambda j, f, te, nu: (te[tile(j, nu)], chunk(j, f, nu), 0))],
            out_specs=pl.BlockSpec(slab, lambda j, f, te, nu: (j, 0, 0)),
            scratch_shapes=[pltpu.VMEM((MOE_TM, d), bf16), pltpu.VMEM((MOE_TM, d), f32)]),
        out_shape=jax.ShapeDtypeStruct((n_tiles * MOE_TM, d // LANES, LANES), f32),
        compiler_params=_cparams("arbitrary", "arbitrary"),
        name="moe_experts",
    )(tile_expert, n_used, xs, wg, wu, wd)


def _combine_body(final, h_ref, route_ref, fnw_ref, dest_ref, ys_ref, out_ref, buf_ref, idx_ref,
                  sem_idx, sem_row):
    i = pl.program_id(0)
    tt = h_ref.shape[0]
    idx_copy = pltpu.make_async_copy(dest_ref.at[pl.ds(i * 2 * tt, 2 * tt)], idx_ref, sem_idx)
    idx_copy.start()
    idx_copy.wait()

    def issue(r, carry):
        _row_copy(ys_ref, idx_ref[r], buf_ref.at[0], r, sem_row).start()
        _row_copy(ys_ref, idx_ref[tt + r], buf_ref.at[1], r, sem_row).start()
        return carry

    lax.fori_loop(0, tt, issue, 0, unroll=ROW_DMA_UNROLL)
    for slot in range(2):
        pltpu.make_async_copy(ys_ref.at[pl.ds(0, tt)], buf_ref.at[slot], sem_row).wait()
    route = route_ref[...]
    g1 = route[:, ROUTE_GATE:ROUTE_GATE + 1]
    g2 = route[:, ROUTE_GATE + 1:ROUTE_GATE + 2]
    for k in range(buf_ref.shape[2]):
        cols = slice(LANES * k, LANES * (k + 1))
        out_ref[:, cols] = (h_ref[:, cols] + g1 * _from_slabs(buf_ref.at[0], k)
                            + g2 * _from_slabs(buf_ref.at[1], k))
    if final:
        out_ref[...] = _rms(out_ref[...], fnw_ref[...])


def _combine(h, route, fnw, dest_flat, ys, tt, final):
    n, d = h.shape
    return pl.pallas_call(
        functools.partial(_combine_body, final),
        grid=(n // tt,),
        in_specs=[pl.BlockSpec((tt, d), lambda i: (i, 0)), pl.BlockSpec((tt, ROUTE_W), lambda i: (i, 0)),
                  pl.BlockSpec((1, d), lambda i: (0, 0)),
                  pl.BlockSpec(memory_space=pl.ANY), pl.BlockSpec(memory_space=pl.ANY)],
        out_specs=pl.BlockSpec((tt, d), lambda i: (i, 0)),
        out_shape=jax.ShapeDtypeStruct((n, d), f32),
        scratch_shapes=[pltpu.VMEM((2, tt, d // LANES, LANES), f32), pltpu.SMEM((2 * tt,), jnp.int32),
                        pltpu.SemaphoreType.DMA(()), pltpu.SemaphoreType.DMA(())],
        compiler_params=_cparams("arbitrary"),
        name="moe_combine",
    )(h, route, fnw, dest_flat, ys)


def _moe(h, nw, router, wg, wu, wd, fnw, final):
    n, d = h.shape
    tt = TILE * _largest_divisor(n // TILE, 4)
    route, counts = _router(h, nw, router)

    cnt = counts[0, :N_EXPERTS].astype(jnp.int32)
    padded = (cnt + MOE_TM - 1) // MOE_TM * MOE_TM
    ends = jnp.cumsum(padded)
    off = ends - padded
    n_tiles = -(-2 * n // MOE_TM) + N_EXPERTS
    tile_expert = jnp.minimum(
        jnp.sum((jnp.arange(n_tiles)[:, None] * MOE_TM >= ends[None, :]).astype(jnp.int32), axis=1),
        N_EXPERTS - 1).astype(jnp.int32)
    n_used = (ends[-1:] // MOE_TM).astype(jnp.int32)
    sel = route[:, ROUTE_E:ROUTE_E + 2].astype(jnp.int32)
    rank = route[:, ROUTE_RANK:ROUTE_RANK + 2].astype(jnp.int32)
    dest = jnp.sum(jnp.where(sel[..., None] == jnp.arange(N_EXPERTS), off, 0), axis=-1) + rank
    dest_flat = dest.reshape(n // tt, tt, 2).transpose(0, 2, 1).reshape(-1)

    fill_rows = jnp.concatenate([off + cnt, n_used]).astype(jnp.int32)
    xs = _dispatch(h, nw, dest_flat, fill_rows, (n_tiles + 1) * MOE_TM, tt)
    ys = _experts(xs, tile_expert, n_used, wg, wu, wd, n_tiles)
    return _combine(h, route, fnw, dest_flat, ys, tt, final)


def _final_norm_body(h_ref, w_ref, out_ref):
    out_ref[...] = _rms(h_ref[...], w_ref[...])


def _final_norm(h, w):
    n, d = h.shape
    tm = TILE * _largest_divisor(n // TILE, 8)
    return pl.pallas_call(
        _final_norm_body,
        grid=(n // tm,),
        in_specs=[pl.BlockSpec((tm, d), lambda i: (i, 0)), pl.BlockSpec((1, d), lambda i: (0, 0))],
        out_specs=pl.BlockSpec((tm, d), lambda i: (i, 0)),
        out_shape=jax.ShapeDtypeStruct((n, d), f32),
        compiler_params=_cparams("parallel"),
        name="final_norm",
    )(h, w)


def _rope_tables(seq, pad):
    t = jnp.arange(seq)
    meta_pos = jnp.arange(N_META) - N_META
    row = jnp.concatenate([meta_pos, t // GRID_W]).astype(f32)
    col = jnp.concatenate([meta_pos, t % GRID_W]).astype(f32)
    half = HEAD_DIM // 2
    inv = ROPE_THETA ** (-jnp.arange(0, half, 2, dtype=f32) / half)
    ang = jnp.concatenate([row[:, None] * inv, col[:, None] * inv], axis=-1)
    ang = jnp.tile(jnp.repeat(ang, 2, axis=-1), (1, 4))
    ang = jnp.pad(ang, ((pad, 0), (0, 0)))
    return jnp.cos(ang), jnp.sin(ang)


def _pair_swap_matrix(width):
    i = jnp.arange(width)
    p = jnp.zeros((width, width), f32)
    p = p.at[i[1::2], i[0::2]].set(-1.0)
    p = p.at[i[0::2], i[1::2]].set(1.0)
    return p.astype(bf16)


def _group_mean_matrix(width, group):
    i = jnp.arange(width)
    return ((i[:, None] // group == i[None, :] // group).astype(f32) / group).astype(bf16)


def _t5_bucket(rel):
    nb = REL_BUCKETS // 2
    max_exact = nb // 2
    ret = (rel > 0).astype(jnp.int32) * nb
    n = jnp.abs(rel)
    nf = jnp.maximum(n, 1).astype(f32)
    large = max_exact + (jnp.log(nf / max_exact) / math.log(REL_MAX_DIST / max_exact)
                         * (nb - max_exact)).astype(jnp.int32)
    large = jnp.minimum(large, nb - 1)
    return ret + jnp.where(n < max_exact, n, large)


def _swa_bias_tables(rel_bias, lp):
    def lookup(bucket):
        out = jnp.zeros((rel_bias.shape[1],) + bucket.shape, f32)
        for b in range(REL_BUCKETS):
            out = jnp.where((bucket == b)[None], rel_bias[b].astype(f32)[:, None, None], out)
        return out

    qi = jnp.arange(TILE)
    ki = jnp.arange(3 * TILE)
    rel = ki[None, :] - TILE - qi[:, None]
    band = jnp.where((jnp.abs(rel) <= WINDOW)[None], lookup(_t5_bucket(rel)), NEG)
    pos = jnp.arange(lp) - (TILE - N_META)
    rel_m = jnp.arange(N_META)[None, :] - pos[:, None]
    meta = lookup(_t5_bucket(rel_m))
    nt = lp // TILE
    band_t = jnp.transpose(band, (2, 0, 1)).reshape(3 * TILE, 4 * TILE) * LOG2E
    meta_t = (meta.reshape(4, nt, TILE, N_META).transpose(1, 3, 0, 2).reshape(nt, N_META, 4 * TILE)
              * LOG2E)
    return band_t, meta_t


def _row(v, width=None):
    v = v.astype(f32).reshape(1, -1)
    if width is not None and v.shape[1] < width:
        v = jnp.pad(v, ((0, 0), (0, width - v.shape[1])))
    return v


def kernel(x, meta_tokens, rel_bias, norm_mix_w, norm_ffn_w, w_in, ssd_conv_w, ssd_conv_b, ssd_dt_bias, ssd_a_log, ssd_d, ssd_norm_w, gla_gate_w2, gla_gate_b, gla_norm_w, swa_sink, gqa_q_norm_w, gqa_k_norm_w, w_out, ffn_w_gate, ffn_w_up, ffn_w_down, moe_router, moe_w_gate, moe_w_up, moe_w_down, final_norm_w):
    bsz, seq, d = x.shape
    depth = w_in.shape[0]
    pad = (-(seq + N_META)) % TILE
    assert pad == TILE - N_META and seq % TILE == 0
    lp = pad + N_META + seq
    n = bsz * lp

    meta = jnp.broadcast_to(meta_tokens[None].astype(x.dtype), (bsz, N_META, d))
    h = jnp.concatenate([jnp.zeros((bsz, pad, d), x.dtype), meta, x], axis=1).reshape(n, d)

    cos, sin = _rope_tables(seq, pad)
    rot = _pair_swap_matrix(256)
    gavg = _group_mean_matrix(256, HEAD_DIM)
    bias_band, bias_meta = _swa_bias_tables(rel_bias, lp)
    offs = [0]
    for s in IN_SIZES:
        offs.append(offs[-1] + s)

    for i in range(depth):
        wi = w_in[i]
        cols = [wi[:, offs[j]:offs[j + 1]] for j in PACK_ORDER]
        w_pack = jnp.concatenate(cols + [jnp.zeros((d, PACK_WIDTH - offs[-1]), wi.dtype)], axis=1).astype(bf16)
        o_ssd, o_small, o_gla, sk, ak, aq, av, sq, sv = _inproj(
            h, _row(norm_mix_w[i]), w_pack, cos, sin, rot, gavg,
            _row(jnp.tile(gqa_q_norm_w[i], 4)), _row(jnp.tile(gqa_k_norm_w[i], 2)), bsz, lp)

        convw = jnp.pad(ssd_conv_w[i].astype(f32), ((0, 8 - SSD_CONV), (0, 0)))
        convb = _row(ssd_conv_b[i])
        dtb = _row(ssd_dt_bias[i].reshape(-1), 128)
        alog = _row(ssd_a_log[i].reshape(-1), 128)
        yf = _ssd(False, o_ssd, o_small, convw, convb, dtb, alog,
                  _row(jnp.repeat(ssd_d[i], SSD_HEAD_DIM)), bsz, lp)
        y_ssd = _ssd(True, o_ssd, o_small, convw, convb, dtb, alog, (yf, _row(ssd_norm_w[i])), bsz, lp)

        def gate_w(direction):
            lo = SMALL_GA + GLA_GATE_RANK * direction
            full = jnp.zeros((128, GLA_KEY), f32).at[lo:lo + GLA_GATE_RANK].set(gla_gate_w2[i, direction].astype(f32))
            return full.astype(bf16)

        of = _gla(False, o_gla, o_small, gate_w(0), _row(gla_gate_b[i, 0]), None, bsz, lp)
        y_gla = _gla(True, o_gla, o_small, gate_w(1), _row(gla_gate_b[i, 1]),
                     (of, _row(jnp.tile(gla_norm_w[i], GLA_HEADS)), gavg), bsz, lp)

        sink = _row(jnp.repeat(swa_sink[i].astype(f32), TILE)) * LOG2E
        y_swa = _swa(sq, sk, sv, bias_band, bias_meta, sink, bsz, lp)
        y_g2 = _flash(aq, ak, av, bsz, lp)

        h = _outproj((y_ssd, y_gla, y_swa, y_g2), h, w_out[i].astype(bf16))

        j = i // 2
        if i % 2 == 0:
            h = _ffn(h, _row(norm_ffn_w[i]), ffn_w_gate[j].astype(bf16), ffn_w_up[j].astype(bf16),
                     ffn_w_down[j].astype(bf16))
            if i == depth - 1:
                h = _final_norm(h, _row(final_norm_w))
        else:
            router = jnp.pad(moe_router[j].astype(f32), ((0, 0), (0, 128 - N_EXPERTS)))
            h = _moe(h, _row(norm_ffn_w[i]), router, moe_w_gate[j].astype(bf16), moe_w_up[j].astype(bf16),
                     moe_w_down[j].astype(bf16), _row(final_norm_w), i == depth - 1)
    return h.reshape(bsz, lp, d)[:, pad + N_META:]
```

```python
import functools
import math

import jax
import jax.numpy as jnp
from jax import lax
from jax.experimental import pallas as pl
from jax.experimental.pallas import tpu as pltpu

f32 = jnp.float32
bf16 = jnp.bfloat16

N_META = 16
HEAD_DIM = 64
GRID_W = 64
EPS = 1e-6
ROPE_THETA = 10000.0
TILE = 128
SSD_HEADS = 4
SSD_HEAD_DIM = 64
SSD_INNER = 256
SSD_STATE = 128
SSD_CONV = 5
SSD_CONV_DIM = 768
GLA_HEADS = 4
GLA_DK = 32
GLA_DV = 64
GLA_KEY = 128
GLA_VAL = 256
GLA_GATE_RANK = 16
GLA_GATE_NORM = 16.0
GLA_CHUNK = 64
WINDOW = 128
REL_BUCKETS = 32
REL_MAX_DIST = 128
N_EXPERTS = 8
NEG = -1e30
LOG2E = math.log2(math.e)
CONV_HALO = 8
VMEM_LIMIT = 56 * 1024 * 1024

IN_SIZES = (256, 768, 8, 128, 128, 256, 256, 32, 256, 128, 128, 256, 128, 128)
(_Z, _XBC, _DT, _GQ, _GK, _GV, _GR, _GA, _SQ, _SK, _SV, _AQ, _AK, _AV) = range(14)
PACK_ORDER = (_XBC, _Z, _GQ, _GK, _GV, _GR, _SQ, _SK, _SV, _AQ, _AK, _AV, _DT, _GA)
PACK_WIDTH = 2944
C_SSD, C_GLA, C_SWA, C_AQ, C_AK, C_AV, C_SMALL = 0, 1024, 1792, 2304, 2560, 2688, 2816
SMALL_DT, SMALL_GA = 0, 8


def _cparams(*sem):
    return pltpu.CompilerParams(dimension_semantics=sem, vmem_limit_bytes=VMEM_LIMIT)


def _dot(a, b):
    return jnp.dot(a, b, preferred_element_type=f32)


def _dot_nt(a, b):
    return lax.dot_general(a, b, (((1,), (1,)), ((), ())), preferred_element_type=f32)


def _dot_tn(a, b):
    return lax.dot_general(a, b, (((0,), (0,)), ((), ())), preferred_element_type=f32)


def _split(a):
    hi = a.astype(bf16)
    lo = (a - hi.astype(f32)).astype(bf16)
    return hi, lo


def _dot_split_lhs(a, b):
    hi, lo = _split(a)
    return _dot(hi, b) + _dot(lo, b)


def _dot_split_rhs(t, x):
    hi, lo = _split(x)
    return _dot(t, hi) + _dot(t, lo)


def _rms(x, w):
    return x * lax.rsqrt(jnp.mean(x * x, axis=-1, keepdims=True) + EPS) * w


def _silu(x):
    return x / (1.0 + jnp.exp(-x))


def _softplus(x):
    return jnp.maximum(x, 0.0) + jnp.log(1.0 + jnp.exp(-jnp.abs(x)))


def _log_sigmoid(x):
    return jnp.minimum(x, 0.0) - jnp.log(1.0 + jnp.exp(-jnp.abs(x)))


def _tri(n, rev):
    r = lax.broadcasted_iota(jnp.int32, (n, n), 0)
    c = lax.broadcasted_iota(jnp.int32, (n, n), 1)
    return (r <= c) if rev else (r >= c)


def _valid_rows(tile, n):
    rows = lax.broadcasted_iota(jnp.int32, (n, 1), 0)
    return jnp.logical_or(tile > 0, rows >= TILE - N_META)


def _inproj_body(h_ref, nw_ref, w_ref, cos_ref, sin_ref, rot_ref, gavg_ref, qnw_ref, knw_ref,
                 ssd_ref, small_ref, gla_ref, sk_ref, k_ref, q_ref, v_ref, sq_ref, sv_ref):
    u = _rms(h_ref[...], nw_ref[...]).astype(bf16)

    def mm(lo, hi):
        return _dot(u, w_ref[:, lo:hi])

    ssd_ref[...] = mm(C_SSD, C_GLA)
    gla_ref[...] = mm(C_GLA, C_SWA)
    small_ref[...] = mm(C_SMALL, PACK_WIDTH)
    akv = mm(C_AK, C_SMALL)
    v_ref[...] = akv[:, 128:].T.astype(bf16)

    skv = mm(C_SWA + 256, C_AQ)
    sk_ref[...] = skv[:, :128].astype(bf16)
    sv_ref[...] = skv[:, 128:].T.astype(bf16)
    sqt = (mm(C_SWA, C_SWA + 256) * (HEAD_DIM ** -0.5 * LOG2E)).T.astype(bf16)
    sq_ref[...] = jnp.zeros_like(sq_ref)
    for t in range(sq_ref.shape[0]):
        for hh in range(4):
            lo = HEAD_DIM * (hh // 2)
            sq_ref[t, lo:lo + HEAD_DIM, TILE * hh:TILE * (hh + 1)] = (
                sqt[HEAD_DIM * hh:HEAD_DIM * (hh + 1), TILE * t:TILE * (t + 1)])

    def norm_rope(t, w, width):
        ms = _dot_split_lhs(t * t, gavg_ref[:width, :width])
        tn = t * lax.rsqrt(ms + EPS) * w
        tr = _dot(tn.astype(bf16), rot_ref[:width, :width])
        return tn * cos_ref[:, :width] + tr * sin_ref[:, :width]

    q = norm_rope(mm(C_AQ, C_AK), qnw_ref[...], 256) * (HEAD_DIM ** -0.5 * LOG2E)
    q_ref[...] = q.T.astype(bf16)
    k_ref[...] = norm_rope(akv[:, :128], knw_ref[...], 128).astype(bf16)


def _inproj(h, nw, w, cos, sin, rot, gavg, qnw, knw, bsz, lp):
    n = h.shape[0]
    d = h.shape[1]
    tm = TILE * _largest_divisor(lp // TILE, 3)
    per = lp // tm
    row = lambda b, i: (b * per + i, 0)
    const = lambda b, i: (0, 0)
    tab = lambda b, i: (i, 0)
    outs = ((1024, f32), (128, f32), (768, f32), (128, bf16), (128, bf16))
    slab = lambda b, i: (b * per + i, 0, 0)
    sub = tm // TILE
    return pl.pallas_call(
        _inproj_body,
        grid=(bsz, per),
        in_specs=[pl.BlockSpec((tm, d), row), pl.BlockSpec((1, d), const),
                  pl.BlockSpec((d, PACK_WIDTH), const),
                  pl.BlockSpec((tm, 256), tab), pl.BlockSpec((tm, 256), tab),
                  pl.BlockSpec((256, 256), const), pl.BlockSpec((256, 256), const),
                  pl.BlockSpec((1, 256), const), pl.BlockSpec((1, 128), const)],
        out_specs=([pl.BlockSpec((tm, c), row) for c, _ in outs]
                   + [pl.BlockSpec((None, 256, tm), slab), pl.BlockSpec((None, 128, tm), slab),
                      pl.BlockSpec((sub, 128, 4 * TILE), slab), pl.BlockSpec((None, 128, tm), slab)]),
        out_shape=([jax.ShapeDtypeStruct((n, c), t) for c, t in outs]
                   + [jax.ShapeDtypeStruct((n // tm, 256, tm), bf16),
                      jax.ShapeDtypeStruct((n // tm, 128, tm), bf16),
                      jax.ShapeDtypeStruct((n // TILE, 128, 4 * TILE), bf16),
                      jax.ShapeDtypeStruct((n // tm, 128, tm), bf16)]),
        compiler_params=_cparams("parallel", "parallel"),
        name="inproj",
    )(h, nw, w, cos, sin, rot, gavg, qnw, knw)


def _largest_divisor(n, cap):
    return max(k for k in range(1, cap + 1) if n % k == 0)


def _for_each_batch(chain, batched, n_scratch):
    def body(*refs):
        nb = next(r.shape[0] for r, flag in zip(refs, batched) if flag)
        io, scratch = refs[:len(batched)], refs[len(batched):]
        assert len(scratch) == n_scratch * nb
        for b in range(nb):
            chain(*[r.at[b] if flag else r for r, flag in zip(io, batched)],
                  *[scratch[k * nb + b] for k in range(n_scratch)])
    return body


def _per_batch_scratch(nb, *shapes):
    return [pltpu.VMEM(shape, dtype) for shape, dtype in shapes for _ in range(nb)]


def _ssd_scan_tile(rev, xs, bmat, cmat, dt, cum, tot, st):
    col0 = SMALL_DT + (SSD_HEADS if rev else 0)
    causal = _tri(TILE, rev)
    cum_t = cum.T
    e_cum = jnp.exp(cum)
    e_tot = jnp.exp(tot)
    head_of = lax.broadcasted_iota(jnp.int32, (TILE, SSD_INNER), 1) // SSD_HEAD_DIM
    head_of_row = lax.broadcasted_iota(jnp.int32, (1, SSD_INNER), 1) // SSD_HEAD_DIM

    def widen(cols, like):
        out = jnp.zeros(like.shape, f32)
        for hh in range(SSD_HEADS):
            out = jnp.where(like == hh, cols[:, col0 + hh:col0 + hh + 1], out)
        return out

    def own_head(r):
        out = jnp.zeros((TILE, SSD_INNER), f32)
        for hh in range(SSD_HEADS):
            out = jnp.where(head_of == hh, r[TILE * hh:TILE * (hh + 1)], out)
        return out

    xd = (xs * widen(dt, head_of)).astype(bf16)
    scores, bws = [], []
    for g in range(2):
        bg = bmat[:, SSD_STATE * g:SSD_STATE * (g + 1)]
        cbg = _dot_nt(cmat[:, SSD_STATE * g:SSD_STATE * (g + 1)], bg)
        for hh in (2 * g, 2 * g + 1):
            col = col0 + hh
            a_col = cum[:, col:col + 1]
            decay = jnp.where(causal, jnp.exp(a_col - cum_t[col:col + 1, :]), 0.0)
            scores.append((cbg * decay).astype(bf16))
            bws.append((bg.astype(f32) * jnp.exp(tot[:, col:col + 1] - a_col)).astype(bf16))
    y = own_head(_dot(jnp.concatenate(scores, axis=0), xd))
    y = y + _dot(cmat, st.astype(bf16)) * widen(e_cum, head_of)
    upd = _dot_tn(jnp.concatenate(bws, axis=1), xd)
    rows = [jnp.where(head_of == 2 * g, upd[SSD_STATE * 2 * g:SSD_STATE * (2 * g + 1)],
                      jnp.where(head_of == 2 * g + 1,
                                upd[SSD_STATE * (2 * g + 1):SSD_STATE * (2 * g + 2)], 0.0))
            for g in range(2)]
    return y, st * widen(e_tot, head_of_row) + jnp.concatenate(rows, axis=0)


def _ssd_body(rev, nt, *refs):
    if rev:
        xbc_ref, small_ref, dtb_ref, alog_ref, yf_ref, z_ref, normw_ref, out_ref, state_ref = refs
    else:
        (cur_ref, prev_ref, next_ref, small_ref, convw_ref, convb_ref, dtb_ref, alog_ref, dskip_ref,
         out_ref, xbc_out_ref, state_ref) = refs[:12]
        ext_refs = refs[12:]
    c = pl.program_id(0)
    tile = (nt - 1 - c) if rev else c
    nb = small_ref.shape[0]
    valid = _valid_rows(tile, TILE)

    a = -jnp.exp(alog_ref[...])
    dts = [jnp.where(valid, _softplus(small_ref[b] + dtb_ref[...]), 0.0) for b in range(nb)]
    cum_all = _dot_split_rhs(_tri(TILE, rev).astype(bf16),
                             jnp.concatenate([dt * a for dt in dts], axis=1))
    for b in range(nb):
        dt = dts[b]
        if rev:
            xbc = xbc_ref[b]
            xs = xbc[:, :SSD_INNER].astype(f32)
            bc = xbc[:, SSD_INNER:]
        else:
            ext_ref = ext_refs[b]
            ext_ref[0:CONV_HALO, :] = jnp.where(tile > 0, prev_ref[b], 0.0)
            ext_ref[CONV_HALO:CONV_HALO + TILE, :] = jnp.where(valid, cur_ref[b], 0.0)
            ext_ref[CONV_HALO + TILE:, :] = jnp.where(tile < nt - 1, next_ref[b], 0.0)
            acc = jnp.zeros((TILE, SSD_CONV_DIM), f32) + convb_ref[...]
            first = CONV_HALO - (SSD_CONV - 1) // 2
            for k in range(SSD_CONV):
                acc = acc + convw_ref[k:k + 1, :] * ext_ref[first + k:first + k + TILE, :]
            xbc = jnp.where(valid, _silu(acc), 0.0)
            xbc_out_ref[b] = xbc.astype(bf16)
            xs = xbc[:, :SSD_INNER]
            bc = xbc[:, SSD_INNER:].astype(bf16)
        st = jnp.where(c == 0, 0.0, state_ref[b])
        y, st = _ssd_scan_tile(rev, xs, bc[:, :2 * SSD_STATE], bc[:, 2 * SSD_STATE:], dt,
                               cum_all[:, 128 * b:128 * (b + 1)],
                               jnp.sum(dt * a, axis=0, keepdims=True), st)
        state_ref[b] = st
        if rev:
            y = (yf_ref[b] + y) * _silu(z_ref[b])
            out_ref[b] = jnp.where(valid, _rms(y, normw_ref[...]), 0.0).astype(out_ref.dtype)
        else:
            out_ref[b] = y + dskip_ref[...] * xs


def _ssd(rev, o_ssd, o_small, convw, convb, dtb, alog, extra, bsz, lp):
    n = o_ssd.shape[0]
    nt = lp // TILE
    hb = TILE // CONV_HALO
    o_ssd = o_ssd.reshape(bsz, lp, -1)
    o_small = o_small.reshape(bsz, lp, -1)

    def tile_of(c):
        return (nt - 1 - c) if rev else c

    cur = lambda c: (0, tile_of(c), 0)
    prev = lambda c: (0, jnp.maximum(tile_of(c) * hb - 1, 0), 0)
    nxt = lambda c: (0, jnp.minimum((tile_of(c) + 1) * hb, lp // CONV_HALO - 1), 0)
    zcol = lambda c: (0, tile_of(c), SSD_CONV_DIM // SSD_INNER)
    const = lambda c: (0, 0)
    tile3 = lambda width: pl.BlockSpec((bsz, TILE, width), cur)
    state = pltpu.VMEM((bsz, 2 * SSD_STATE, SSD_INNER), f32)
    if rev:
        yf, xbc, normw = extra
        out = pl.pallas_call(
            functools.partial(_ssd_body, rev, nt),
            grid=(nt,),
            in_specs=[tile3(SSD_CONV_DIM), tile3(128), pl.BlockSpec((1, 128), const),
                      pl.BlockSpec((1, 128), const), tile3(SSD_INNER),
                      pl.BlockSpec((bsz, TILE, SSD_INNER), zcol), pl.BlockSpec((1, SSD_INNER), const)],
            out_specs=tile3(SSD_INNER),
            out_shape=jax.ShapeDtypeStruct((bsz, lp, SSD_INNER), bf16),
            scratch_shapes=[state],
            compiler_params=_cparams("arbitrary"),
            name="ssd_rev",
        )(xbc, o_small, dtb, alog, yf, o_ssd, normw)
        return out.reshape(n, SSD_INNER)
    halo = lambda im: pl.BlockSpec((bsz, CONV_HALO, SSD_CONV_DIM), im)
    return pl.pallas_call(
        functools.partial(_ssd_body, rev, nt),
        grid=(nt,),
        in_specs=[tile3(SSD_CONV_DIM), halo(prev), halo(nxt), tile3(128),
                  pl.BlockSpec((8, SSD_CONV_DIM), const), pl.BlockSpec((1, SSD_CONV_DIM), const),
                  pl.BlockSpec((1, 128), const), pl.BlockSpec((1, 128), const),
                  pl.BlockSpec((1, SSD_INNER), const)],
        out_specs=[tile3(SSD_INNER), tile3(SSD_CONV_DIM)],
        out_shape=[jax.ShapeDtypeStruct((bsz, lp, SSD_INNER), f32),
                   jax.ShapeDtypeStruct((bsz, lp, SSD_CONV_DIM), bf16)],
        scratch_shapes=[state] + _per_batch_scratch(bsz, ((TILE + 2 * CONV_HALO, SSD_CONV_DIM), f32)),
        compiler_params=_cparams("arbitrary"),
        name="ssd_fwd",
    )(o_ssd, o_ssd, o_ssd, o_small, convw, convb, dtb, alog, extra)


def _gla_body(rev, nt, *refs):
    if rev:
        x_ref, small_ref, wg_ref, gb_ref, of_ref, normw_ref, gavg_ref, out_ref, st_ref = refs
    else:
        x_ref, small_ref, wg_ref, gb_ref, out_ref, st_ref = refs
    c = pl.program_id(0)
    tile = (nt - 1 - c) if rev else c
    nb = x_ref.shape[0]
    n = GLA_CHUNK
    valid = _valid_rows(tile, TILE)

    pre = _dot(small_ref[...].reshape(nb * TILE, 128).astype(bf16), wg_ref[...]) + gb_ref[...]
    g_all = _log_sigmoid(pre) / GLA_GATE_NORM
    gs = [jnp.where(valid, g_all[TILE * b:TILE * (b + 1)], 0.0) for b in range(nb)]
    row = lax.broadcasted_iota(jnp.int32, (TILE, TILE), 0)
    col = lax.broadcasted_iota(jnp.int32, (TILE, TILE), 1)
    same_chunk = (row // n) == (col // n)
    cum_mat = jnp.where(jnp.logical_and(same_chunk, (row <= col) if rev else (row >= col)), 1.0, 0.0)
    bc_all = _dot_split_rhs(cum_mat.astype(bf16), jnp.concatenate(gs, axis=1))

    first = slice(n, 2 * n) if rev else slice(0, n)
    second = slice(0, n) if rev else slice(n, 2 * n)
    rows = lax.broadcasted_iota(jnp.int32, (TILE, 1), 0)
    in_first = (rows >= n) if rev else (rows < n)
    tri = _tri(n, rev)
    tri4 = jnp.concatenate([tri] * GLA_HEADS, axis=0)
    lane_head = lax.broadcasted_iota(jnp.int32, (n, GLA_KEY), 1) // GLA_DK
    out_head = lax.broadcasted_iota(jnp.int32, (n, GLA_VAL), 1) // GLA_DV
    blockdiag = (lax.broadcasted_iota(jnp.int32, (GLA_VAL, GLA_KEY), 0) // GLA_DV
                 == lax.broadcasted_iota(jnp.int32, (GLA_VAL, GLA_KEY), 1) // GLA_DK)

    def stack_heads(a):
        return jnp.concatenate([jnp.where(lane_head == hh, a, 0.0) for hh in range(GLA_HEADS)],
                               axis=0).astype(bf16)

    def own_head(r):
        out = jnp.zeros((n, GLA_VAL), f32)
        for hh in range(GLA_HEADS):
            out = jnp.where(out_head == hh, r[n * hh:n * (hh + 1)], out)
        return out

    for b in range(nb):
        x = x_ref[b]
        g = gs[b]
        bc = bc_all[:, 128 * b:128 * (b + 1)]
        q = jnp.where(valid, x[:, :GLA_KEY], 0.0) * GLA_DK ** -0.5
        k = jnp.where(valid, x[:, GLA_KEY:2 * GLA_KEY], 0.0)
        v = jnp.where(valid, x[:, 2 * GLA_KEY:2 * GLA_KEY + GLA_VAL], 0.0).astype(bf16)
        bl_first = jnp.sum(g[first], axis=0, keepdims=True)
        bl_second = jnp.sum(g[second], axis=0, keepdims=True)
        qt = q * jnp.exp(bc)
        kt = k * jnp.exp(-bc)
        kw = k * jnp.exp(jnp.where(in_first, bl_first, bl_second) - bc)

        att_f = jnp.where(tri4, _dot_nt(stack_heads(qt[first]), kt[first].astype(bf16)), 0.0)
        o_f = own_head(_dot(att_f.astype(bf16), v[first]))
        keys = jnp.concatenate([kw[first], kt[second]], axis=0).astype(bf16)
        vals = jnp.concatenate([v[first], v[second]], axis=0)
        att_s = _dot_nt(stack_heads(qt[second]), keys)
        att_s = jnp.concatenate([att_s[:, :n], jnp.where(tri4, att_s[:, n:], 0.0)], axis=1)
        o_s = own_head(_dot(att_s.astype(bf16), vals))

        st = jnp.where(c == 0, 0.0, st_ref[b])
        q_in = qt * jnp.exp(jnp.where(in_first, 0.0, bl_first))
        o = _dot_nt(q_in.astype(bf16), st.astype(bf16)) + jnp.concatenate(
            [o_s, o_f] if rev else [o_f, o_s], axis=0)
        k_out = (kw * jnp.exp(jnp.where(in_first, bl_second, 0.0))).astype(bf16)
        st_ref[b] = st * jnp.exp(bl_first + bl_second) + jnp.where(blockdiag, _dot_tn(v, k_out), 0.0)

        if rev:
            o = of_ref[b] + o
            ms = _dot_split_lhs(o * o, gavg_ref[...])
            on = o * lax.rsqrt(ms + EPS) * normw_ref[...]
            r = x[:, 2 * GLA_KEY + GLA_VAL:]
            out_ref[b] = jnp.where(valid, on * _silu(r), 0.0).astype(out_ref.dtype)
        else:
            out_ref[b] = o


def _gla(rev, o_gla, o_small, wg, gb, extra, bsz, lp):
    n = o_gla.shape[0]
    nt = lp // TILE
    cur = lambda c: (0, (nt - 1 - c) if rev else c, 0)
    const = lambda c: (0, 0)
    in_specs = [pl.BlockSpec((bsz, TILE, 768), cur), pl.BlockSpec((bsz, TILE, 128), cur),
                pl.BlockSpec((128, GLA_KEY), const), pl.BlockSpec((1, GLA_KEY), const)]
    args = [o_gla.reshape(bsz, lp, -1), o_small.reshape(bsz, lp, -1), wg, gb]
    if rev:
        of, normw, gavg = extra
        in_specs += [pl.BlockSpec((bsz, TILE, GLA_VAL), cur), pl.BlockSpec((1, GLA_VAL), const),
                     pl.BlockSpec((GLA_VAL, GLA_VAL), const)]
        args += [of.reshape(bsz, lp, -1), normw, gavg]
    out = pl.pallas_call(
        functools.partial(_gla_body, rev, nt),
        grid=(nt,),
        in_specs=in_specs,
        out_specs=pl.BlockSpec((bsz, TILE, GLA_VAL), cur),
        out_shape=jax.ShapeDtypeStruct((bsz, lp, GLA_VAL), bf16 if rev else f32),
        scratch_shapes=[pltpu.VMEM((bsz, GLA_VAL, GLA_KEY), f32)],
        compiler_params=_cparams("arbitrary"),
        name="gla_rev" if rev else "gla_fwd",
    )(*args)
    return out.reshape(n, GLA_VAL)


def _swa_body(nt, qp_ref, kp_ref, kc_ref, kn_ref, km_ref, vp_ref, vc_ref, vn_ref, vm_ref,
              bias_ref, bmeta_ref, sink_ref, out_ref):
    c = pl.program_id(0)
    krow = lax.broadcasted_iota(jnp.int32, (3 * TILE, 1), 0)
    pen = jnp.where(krow < TILE, jnp.where(c >= 2, 0.0, NEG),
                    jnp.where(krow < 2 * TILE, jnp.where(c >= 1, 0.0, NEG),
                              jnp.where(c <= nt - 2, 0.0, NEG)))
    qp = qp_ref[...]
    kcat = jnp.concatenate([kp_ref[...], kc_ref[...], kn_ref[...]], axis=0)
    s = _dot(kcat, qp) + bias_ref[...] + pen
    sm = _dot(km_ref[...], qp) + bmeta_ref[...]
    sk = sink_ref[...]
    m = jnp.maximum(jnp.maximum(jnp.max(s, axis=0, keepdims=True),
                                jnp.max(sm, axis=0, keepdims=True)), sk)
    p = jnp.exp2(s - m)
    pm = jnp.exp2(sm - m)
    inv = 1.0 / (jnp.sum(p, axis=0, keepdims=True) + jnp.sum(pm, axis=0, keepdims=True)
                 + jnp.exp2(sk - m))
    pb = p.astype(bf16)
    pmb = jnp.concatenate([jnp.zeros((TILE - N_META, 4 * TILE), bf16), pm.astype(bf16)], axis=0)
    vcat = jnp.concatenate([vp_ref[...], vc_ref[...], vn_ref[...]], axis=1)
    heads = []
    for g in range(2):
        rows = slice(HEAD_DIM * g, HEAD_DIM * (g + 1))
        cols = slice(2 * TILE * g, 2 * TILE * (g + 1))
        pv = (_dot(vcat[rows, :], pb[:, cols]) + _dot(vm_ref[rows, :], pmb[:, cols])) * inv[:, cols]
        heads += [pv[:, :TILE], pv[:, TILE:]]
    o = jnp.concatenate(heads, axis=0).T
    out_ref[...] = jnp.where(_valid_rows(c, TILE), o, 0.0).astype(out_ref.dtype)


def _swa(qp, k, vt, bias_band, bias_meta, sink, bsz, lp):
    n = k.shape[0]
    nt = lp // TILE
    per, _, tm = vt.shape[0] // bsz, vt.shape[1], vt.shape[2]
    sub = tm // TILE
    qp = qp.reshape(bsz, nt, 128, 4 * TILE)
    k = k.reshape(bsz, lp, 128)
    vt = vt.reshape(bsz, per, 128, tm)
    prev = lambda c: jnp.maximum(c - 1, 0)
    nxt = lambda c: jnp.minimum(c + 1, nt - 1)
    kspec = lambda tile: pl.BlockSpec((bsz, TILE, 128), lambda c: (0, tile(c), 0))
    vspec = lambda tile: pl.BlockSpec((bsz, None, 128, TILE), lambda c: (0, tile(c) // sub, 0, tile(c) % sub))
    same = lambda c: c
    first = lambda c: 0
    out = pl.pallas_call(
        _for_each_batch(functools.partial(_swa_body, nt), [True] * 9 + [False] * 3 + [True], 0),
        grid=(nt,),
        in_specs=[pl.BlockSpec((bsz, None, 128, 4 * TILE), lambda c: (0, c, 0, 0)),
                  kspec(prev), kspec(same), kspec(nxt),
                  pl.BlockSpec((bsz, N_META, 128), lambda c: (0, TILE // N_META - 1, 0)),
                  vspec(prev), vspec(same), vspec(nxt), vspec(first),
                  pl.BlockSpec((3 * TILE, 4 * TILE), lambda c: (0, 0)),
                  pl.BlockSpec((None, N_META, 4 * TILE), lambda c: (c, 0, 0)),
                  pl.BlockSpec((1, 4 * TILE), lambda c: (0, 0))],
        out_specs=pl.BlockSpec((bsz, TILE, 256), lambda c: (0, c, 0)),
        out_shape=jax.ShapeDtypeStruct((bsz, lp, 256), bf16),
        compiler_params=_cparams("arbitrary"),
        name="swa",
    )(qp, k, k, k, k, vt, vt, vt, vt, bias_band, bias_meta, sink)
    return out.reshape(n, 256)


def _flash_body(nk, tq, tk, qt_ref, k_ref, vt_ref, out_ref, qpad_ref, m_ref, l_ref, acc_ref):
    i = pl.program_id(1)
    krow = lax.broadcasted_iota(jnp.int32, (tk, 1), 0)
    qpad_ref[...] = jnp.zeros_like(qpad_ref)
    for hh in range(4):
        lo = HEAD_DIM * (hh // 2)
        qpad_ref[lo:lo + HEAD_DIM, tq * hh:tq * (hh + 1)] = qt_ref[HEAD_DIM * hh:HEAD_DIM * (hh + 1), :]
    m_ref[...] = jnp.full_like(m_ref, NEG)
    l_ref[...] = jnp.zeros_like(l_ref)
    acc_ref[...] = jnp.zeros_like(acc_ref)

    def step(j, first):
        s = _dot(k_ref[pl.ds(pl.multiple_of(j * tk, tk), tk), :], qpad_ref[...])
        if first:
            s = jnp.where(krow >= TILE - N_META, s, NEG)
        m_old = m_ref[...]
        m_new = jnp.maximum(m_old, jnp.max(s, axis=0, keepdims=True))
        alpha = jnp.exp2(m_old - m_new)
        p = jnp.exp2(s - m_new)
        l_ref[...] = alpha * l_ref[...] + jnp.sum(p, axis=0, keepdims=True)
        m_ref[...] = m_new
        pb = p.astype(bf16)
        for g in range(2):
            pv = _dot(vt_ref[j, HEAD_DIM * g:HEAD_DIM * (g + 1), :], pb[:, 2 * g * tq:(2 * g + 2) * tq])
            for r in range(2):
                hh = 2 * g + r
                rows = slice(HEAD_DIM * hh, HEAD_DIM * (hh + 1))
                acc_ref[rows, :] = (alpha[:, tq * hh:tq * (hh + 1)] * acc_ref[rows, :]
                                    + pv[:, tq * r:tq * (r + 1)])

    step(0, True)

    def body(j, carry):
        step(j, False)
        return carry

    lax.fori_loop(1, nk, body, 0, unroll=2 if (nk - 1) % 2 == 0 else 1)
    linv = 1.0 / l_ref[...]
    for hh in range(4):
        rows = slice(HEAD_DIM * hh, HEAD_DIM * (hh + 1))
        acc_ref[rows, :] = acc_ref[rows, :] * linv[:, tq * hh:tq * (hh + 1)]
    rows = lax.broadcasted_iota(jnp.int32, (tq, 1), 0)
    valid = jnp.logical_or(i > 0, rows >= TILE - N_META)
    out_ref[...] = jnp.where(valid, acc_ref[...].T, 0.0).astype(out_ref.dtype)


def _flash(qt, k, vt, bsz, lp):
    n = k.shape[0]
    tq = qt.shape[2]
    per = lp // tq
    return pl.pallas_call(
        functools.partial(_flash_body, per, tq, tq),
        grid=(bsz, per),
        in_specs=[pl.BlockSpec((None, 256, tq), lambda b, i: (b * per + i, 0, 0)),
                  pl.BlockSpec((lp, 128), lambda b, i: (b, 0)),
                  pl.BlockSpec((per, 128, tq), lambda b, i: (b, 0, 0))],
        out_specs=pl.BlockSpec((tq, 256), lambda b, i: (b * per + i, 0)),
        out_shape=jax.ShapeDtypeStruct((n, 256), bf16),
        scratch_shapes=[pltpu.VMEM((128, 4 * tq), bf16), pltpu.VMEM((1, 4 * tq), f32),
                        pltpu.VMEM((1, 4 * tq), f32), pltpu.VMEM((256, tq), f32)],
        compiler_params=_cparams("parallel", "parallel"),
        name="gqa_full",
    )(qt, k, vt)


def _outproj_body(y0_ref, y1_ref, y2_ref, y3_ref, h_ref, w_ref, out_ref):
    acc = h_ref[...]
    for j, y_ref in enumerate((y0_ref, y1_ref, y2_ref, y3_ref)):
        acc = acc + _dot(y_ref[...], w_ref[256 * j:256 * (j + 1), :])
    out_ref[...] = acc


def _outproj(ys, h, w):
    n, d = h.shape
    tm = TILE * _largest_divisor(n // TILE, 4)
    row = lambda i: (i, 0)
    return pl.pallas_call(
        _outproj_body,
        grid=(n // tm,),
        in_specs=[pl.BlockSpec((tm, 256), row)] * 4 + [pl.BlockSpec((tm, d), row),
                                                      pl.BlockSpec((d, d), lambda i: (0, 0))],
        out_specs=pl.BlockSpec((tm, d), row),
        out_shape=jax.ShapeDtypeStruct((n, d), f32),
        compiler_params=_cparams("parallel"),
        name="outproj",
    )(*ys, h, w)


def _ffn_body(h_ref, nw_ref, wg_ref, wu_ref, wd_ref, out_ref, u_ref, acc_ref):
    j = pl.program_id(1)

    @pl.when(j == 0)
    def _():
        u_ref[...] = _rms(h_ref[...], nw_ref[...]).astype(bf16)
        acc_ref[...] = jnp.zeros_like(acc_ref)

    u = u_ref[...]
    t = _silu(_dot(u, wg_ref[...])) * _dot(u, wu_ref[...])
    acc_ref[...] += _dot(t.astype(bf16), wd_ref[...])

    @pl.when(j == pl.num_programs(1) - 1)
    def _():
        out_ref[...] = h_ref[...] + acc_ref[...]


def _ffn(h, nw, wg, wu, wd):
    n, d = h.shape
    ff = wg.shape[1]
    tm = TILE * _largest_divisor(n // TILE, 4)
    tf = 128 * _largest_divisor(ff // 128, 11)
    row = lambda i, j: (i, 0)
    return pl.pallas_call(
        _ffn_body,
        grid=(n // tm, ff // tf),
        in_specs=[pl.BlockSpec((tm, d), row), pl.BlockSpec((1, d), lambda i, j: (0, 0)),
                  pl.BlockSpec((d, tf), lambda i, j: (0, j)), pl.BlockSpec((d, tf), lambda i, j: (0, j)),
                  pl.BlockSpec((tf, d), lambda i, j: (j, 0))],
        out_specs=pl.BlockSpec((tm, d), row),
        out_shape=jax.ShapeDtypeStruct((n, d), f32),
        scratch_shapes=[pltpu.VMEM((tm, d), bf16), pltpu.VMEM((tm, d), f32)],
        compiler_params=_cparams("parallel", "arbitrary"),
        name="ffn",
    )(h, nw, wg, wu, wd)


MOE_TM = 1024
ROUTE_E, ROUTE_RANK, ROUTE_GATE, ROUTE_W = 0, 2, 4, 8


def _router_body(h_ref, nw_ref, r_ref, route_ref, cnt_ref, base_ref):
    @pl.when(pl.program_id(0) == 0)
    def _():
        base_ref[...] = jnp.zeros_like(base_ref)

    u = _rms(h_ref[...], nw_ref[...])
    u_hi, u_lo = _split(u)
    r_hi, r_lo = _split(r_ref[...])
    logits = _dot(u_hi, r_hi) + _dot(u_lo, r_hi) + _dot(u_hi, r_lo)
    tm = logits.shape[0]
    lane = lax.broadcasted_iota(jnp.int32, logits.shape, 1)
    logits = jnp.where(lane < N_EXPERTS, logits, NEG)
    m1 = jnp.max(logits, axis=-1, keepdims=True)
    i1 = jnp.min(jnp.where(logits == m1, lane, 128), axis=-1, keepdims=True)
    rest = jnp.where(lane == i1, NEG, logits)
    m2 = jnp.max(rest, axis=-1, keepdims=True)
    i2 = jnp.min(jnp.where(rest == m2, lane, 128), axis=-1, keepdims=True)
    e2 = jnp.exp(m2 - m1)
    g1 = 1.0 / (1.0 + e2)
    g2 = e2 * g1

    sel1 = lane == i1
    sel2 = lane == i2
    onehot = jnp.where(sel1, 1.0, jnp.where(sel2, 1.0, 0.0))
    strict = jnp.where(lax.broadcasted_iota(jnp.int32, (tm, tm), 0)
                       > lax.broadcasted_iota(jnp.int32, (tm, tm), 1), 1.0, 0.0).astype(bf16)
    before = _dot(strict, onehot.astype(bf16)) + base_ref[0:1, :]
    r1 = jnp.sum(jnp.where(sel1, before, 0.0), axis=-1, keepdims=True)
    r2 = jnp.sum(jnp.where(sel2, before, 0.0), axis=-1, keepdims=True)
    route = jnp.zeros(logits.shape, f32)
    for k, val in enumerate((i1.astype(f32), i2.astype(f32), r1, r2, g1, g2)):
        route = jnp.where(lane == k, val, route)
    route_ref[...] = route[:, :ROUTE_W]
    base_ref[0:1, :] = base_ref[0:1, :] + jnp.sum(onehot, axis=0, keepdims=True)
    cnt_ref[...] = base_ref[...]


def _router(h, nw, router):
    n, d = h.shape
    tm = TILE * _largest_divisor(n // TILE, 8)
    return pl.pallas_call(
        _router_body,
        grid=(n // tm,),
        in_specs=[pl.BlockSpec((tm, d), lambda i: (i, 0)), pl.BlockSpec((1, d), lambda i: (0, 0)),
                  pl.BlockSpec((d, 128), lambda i: (0, 0))],
        out_specs=[pl.BlockSpec((tm, ROUTE_W), lambda i: (i, 0)), pl.BlockSpec((8, 128), lambda i: (0, 0))],
        out_shape=[jax.ShapeDtypeStruct((n, ROUTE_W), f32), jax.ShapeDtypeStruct((8, 128), f32)],
        scratch_shapes=[pltpu.VMEM((8, 128), f32)],
        compiler_params=_cparams("arbitrary"),
        name="moe_router",
    )(h, nw, router)


LANES = 128
ROW_DMA_UNROLL = 8


def _row_copy(src_ref, src_row, dst_ref, dst_row, sem):
    return pltpu.make_async_copy(src_ref.at[src_row], dst_ref.at[dst_row], sem)


def _to_slabs(dst_ref, val):
    dst_ref[...] = val.reshape(dst_ref.shape)


def _from_slabs(src_ref):
    rows, chunks, lanes = src_ref.shape
    return src_ref[...].reshape(rows, chunks * lanes)


def _dispatch_body(fill_ref, h_ref, nw_ref, dest_ref, xs_ref, u_ref, idx_ref, zero_ref,
                   sem_idx, sem_row, sem_fill):
    i = pl.program_id(0)
    tt = u_ref.shape[0]

    @pl.when(i == 0)
    def _():
        zero_ref[...] = jnp.zeros_like(zero_ref)
        fills = [pltpu.make_async_copy(zero_ref, xs_ref.at[pl.ds(fill_ref[e], MOE_TM)], sem_fill)
                 for e in range(N_EXPERTS)]
        for cp in fills:
            cp.start()
        for cp in fills:
            cp.wait()
        last = xs_ref.shape[0] // MOE_TM - 1
        for j in range(last - N_EXPERTS, last + 1):
            @pl.when(j >= fill_ref[N_EXPERTS])
            def _():
                cp = pltpu.make_async_copy(zero_ref, xs_ref.at[pl.ds(j * MOE_TM, MOE_TM)], sem_fill)
                cp.start()
                cp.wait()

    idx_copy = pltpu.make_async_copy(dest_ref.at[pl.ds(i * 2 * tt, 2 * tt)], idx_ref, sem_idx)
    idx_copy.start()
    _to_slabs(u_ref, _rms(h_ref[...], nw_ref[...]))
    idx_copy.wait()

    def issue(r, carry):
        _row_copy(u_ref, r, xs_ref, idx_ref[r], sem_row).start()
        _row_copy(u_ref, r, xs_ref, idx_ref[tt + r], sem_row).start()
        return carry

    lax.fori_loop(0, tt, issue, 0, unroll=ROW_DMA_UNROLL)
    for _ in range(2):
        pltpu.make_async_copy(u_ref, xs_ref.at[pl.ds(0, tt)], sem_row).wait()


def _dispatch(h, nw, dest_flat, fill_rows, rows, tt):
    n, d = h.shape
    return pl.pallas_call(
        _dispatch_body,
        grid_spec=pltpu.PrefetchScalarGridSpec(
            num_scalar_prefetch=1,
            grid=(n // tt,),
            in_specs=[pl.BlockSpec((tt, d), lambda i, fr: (i, 0)), pl.BlockSpec((1, d), lambda i, fr: (0, 0)),
                      pl.BlockSpec(memory_space=pl.ANY)],
            out_specs=pl.BlockSpec(memory_space=pl.ANY),
            scratch_shapes=[pltpu.VMEM((tt, d // LANES, LANES), f32), pltpu.SMEM((2 * tt,), jnp.int32),
                            pltpu.VMEM((MOE_TM, d // LANES, LANES), f32), pltpu.SemaphoreType.DMA(()),
                            pltpu.SemaphoreType.DMA(()), pltpu.SemaphoreType.DMA(())]),
        out_shape=jax.ShapeDtypeStruct((rows, d // LANES, LANES), f32),
        compiler_params=_cparams("arbitrary"),
        name="moe_dispatch",
    )(fill_rows, h, nw, dest_flat)


def _experts_body(te_ref, nu_ref, x_ref, wg_ref, wu_ref, wd_ref, y_ref, xb_ref, acc_ref):
    j = pl.program_id(0)
    f = pl.program_id(1)

    @pl.when(j < nu_ref[0])
    def _():
        @pl.when(f == 0)
        def _():
            xb_ref[...] = _from_slabs(x_ref).astype(bf16)
            acc_ref[...] = jnp.zeros_like(acc_ref)

        x = xb_ref[...]
        t = _silu(_dot(x, wg_ref[...])) * _dot(x, wu_ref[...])
        acc_ref[...] += _dot(t.astype(bf16), wd_ref[...])

        @pl.when(f == pl.num_programs(1) - 1)
        def _():
            _to_slabs(y_ref, acc_ref[...])

    @pl.when(jnp.logical_and(j >= nu_ref[0], f == pl.num_programs(1) - 1))
    def _():
        y_ref[...] = jnp.zeros_like(y_ref)


def _experts(xs, tile_expert, n_used, wg, wu, wd, n_tiles):
    d = wg.shape[1]
    slab = (MOE_TM, d // LANES, LANES)
    ff = wg.shape[2]
    tf = 128 * _largest_divisor(ff // 128, 4)
    nf = ff // tf
    tile = lambda j, nu: jnp.minimum(j, nu[0] - 1)
    chunk = lambda j, f, nu: jnp.where(j < nu[0], f, nf - 1)
    return pl.pallas_call(
        _experts_body,
        grid_spec=pltpu.PrefetchScalarGridSpec(
            num_scalar_prefetch=2,
            grid=(n_tiles, nf),
            in_specs=[pl.BlockSpec(slab, lambda j, f, te, nu: (tile(j, nu), 0, 0)),
                      pl.BlockSpec((None, d, tf), lambda j, f, te, nu: (te[tile(j, nu)], 0, chunk(j, f, nu))),
                      pl.BlockSpec((None, d, tf), lambda j, f, te, nu: (te[tile(j, nu)], 0, chunk(j, f, nu))),
                      pl.BlockSpec((None, tf, d), lambda j, f, te, nu: (te[tile(j, nu)], chunk(j, f, nu), 0))],
            out_specs=pl.BlockSpec(slab, lambda j, f, te, nu: (j, 0, 0)),
            scratch_shapes=[pltpu.VMEM((MOE_TM, d), bf16), pltpu.VMEM((MOE_TM, d), f32)]),
        out_shape=jax.ShapeDtypeStruct((n_tiles * MOE_TM, d // LANES, LANES), f32),
        compiler_params=_cparams("arbitrary", "arbitrary"),
        name="moe_experts",
    )(tile_expert, n_used, xs, wg, wu, wd)


def _combine_body(final, h_ref, route_ref, fnw_ref, dest_ref, ys_ref, out_ref, buf_ref, idx_ref,
                  sem_idx, sem_row):
    i = pl.program_id(0)
    tt = h_ref.shape[0]
    idx_copy = pltpu.make_async_copy(dest_ref.at[pl.ds(i * 2 * tt, 2 * tt)], idx_ref, sem_idx)
    idx_copy.start()
    idx_copy.wait()

    def issue(r, carry):
        _row_copy(ys_ref, idx_ref[r], buf_ref.at[0], r, sem_row).start()
        _row_copy(ys_ref, idx_ref[tt + r], buf_ref.at[1], r, sem_row).start()
        return carry

    lax.fori_loop(0, tt, issue, 0, unroll=ROW_DMA_UNROLL)
    for slot in range(2):
        pltpu.make_async_copy(ys_ref.at[pl.ds(0, tt)], buf_ref.at[slot], sem_row).wait()
    route = route_ref[...]
    g1 = route[:, ROUTE_GATE:ROUTE_GATE + 1]
    g2 = route[:, ROUTE_GATE + 1:ROUTE_GATE + 2]
    y = h_ref[...] + g1 * _from_slabs(buf_ref.at[0]) + g2 * _from_slabs(buf_ref.at[1])
    out_ref[...] = _rms(y, fnw_ref[...]) if final else y


def _combine(h, route, fnw, dest_flat, ys, tt, final):
    n, d = h.shape
    return pl.pallas_call(
        functools.partial(_combine_body, final),
        grid=(n // tt,),
        in_specs=[pl.BlockSpec((tt, d), lambda i: (i, 0)), pl.BlockSpec((tt, ROUTE_W), lambda i: (i, 0)),
                  pl.BlockSpec((1, d), lambda i: (0, 0)),
                  pl.BlockSpec(memory_space=pl.ANY), pl.BlockSpec(memory_space=pl.ANY)],
        out_specs=pl.BlockSpec((tt, d), lambda i: (i, 0)),
        out_shape=jax.ShapeDtypeStruct((n, d), f32),
        scratch_shapes=[pltpu.VMEM((2, tt, d // LANES, LANES), f32), pltpu.SMEM((2 * tt,), jnp.int32),
                        pltpu.SemaphoreType.DMA(()), pltpu.SemaphoreType.DMA(())],
        compiler_params=_cparams("arbitrary"),
        name="moe_combine",
    )(h, route, fnw, dest_flat, ys)


def _moe(h, nw, router, wg, wu, wd, fnw, final):
    n, d = h.shape
    tt = TILE * _largest_divisor(n // TILE, 4)
    route, counts = _router(h, nw, router)

    cnt = counts[0, :N_EXPERTS].astype(jnp.int32)
    padded = (cnt + MOE_TM - 1) // MOE_TM * MOE_TM
    ends = jnp.cumsum(padded)
    off = ends - padded
    n_tiles = -(-2 * n // MOE_TM) + N_EXPERTS
    tile_expert = jnp.minimum(
        jnp.sum((jnp.arange(n_tiles)[:, None] * MOE_TM >= ends[None, :]).astype(jnp.int32), axis=1),
        N_EXPERTS - 1).astype(jnp.int32)
    n_used = (ends[-1:] // MOE_TM).astype(jnp.int32)
    sel = route[:, ROUTE_E:ROUTE_E + 2].astype(jnp.int32)
    rank = route[:, ROUTE_RANK:ROUTE_RANK + 2].astype(jnp.int32)
    dest = jnp.sum(jnp.where(sel[..., None] == jnp.arange(N_EXPERTS), off, 0), axis=-1) + rank
    dest_flat = dest.reshape(n // tt, tt, 2).transpose(0, 2, 1).reshape(-1)

    fill_rows = jnp.concatenate([off + cnt, n_used]).astype(jnp.int32)
    xs = _dispatch(h, nw, dest_flat, fill_rows, (n_tiles + 1) * MOE_TM, tt)
    ys = _experts(xs, tile_expert, n_used, wg, wu, wd, n_tiles)
    return _combine(h, route, fnw, dest_flat, ys, tt, final)


def _final_norm_body(h_ref, w_ref, out_ref):
    out_ref[...] = _rms(h_ref[...], w_ref[...])


def _final_norm(h, w):
    n, d = h.shape
    tm = TILE * _largest_divisor(n // TILE, 8)
    return pl.pallas_call(
        _final_norm_body,
        grid=(n // tm,),
        in_specs=[pl.BlockSpec((tm, d), lambda i: (i, 0)), pl.BlockSpec((1, d), lambda i: (0, 0))],
        out_specs=pl.BlockSpec((tm, d), lambda i: (i, 0)),
        out_shape=jax.ShapeDtypeStruct((n, d), f32),
        compiler_params=_cparams("parallel"),
        name="final_norm",
    )(h, w)


def _rope_tables(seq, pad):
    t = jnp.arange(seq)
    meta_pos = jnp.arange(N_META) - N_META
    row = jnp.concatenate([meta_pos, t // GRID_W]).astype(f32)
    col = jnp.concatenate([meta_pos, t % GRID_W]).astype(f32)
    half = HEAD_DIM // 2
    inv = ROPE_THETA ** (-jnp.arange(0, half, 2, dtype=f32) / half)
    ang = jnp.concatenate([row[:, None] * inv, col[:, None] * inv], axis=-1)
    ang = jnp.tile(jnp.repeat(ang, 2, axis=-1), (1, 4))
    ang = jnp.pad(ang, ((pad, 0), (0, 0)))
    return jnp.cos(ang), jnp.sin(ang)


def _pair_swap_matrix(width):
    i = jnp.arange(width)
    p = jnp.zeros((width, width), f32)
    p = p.at[i[1::2], i[0::2]].set(-1.0)
    p = p.at[i[0::2], i[1::2]].set(1.0)
    return p.astype(bf16)


def _group_mean_matrix(width, group):
    i = jnp.arange(width)
    return ((i[:, None] // group == i[None, :] // group).astype(f32) / group).astype(bf16)


def _t5_bucket(rel):
    nb = REL_BUCKETS // 2
    max_exact = nb // 2
    ret = (rel > 0).astype(jnp.int32) * nb
    n = jnp.abs(rel)
    nf = jnp.maximum(n, 1).astype(f32)
    large = max_exact + (jnp.log(nf / max_exact) / math.log(REL_MAX_DIST / max_exact)
                         * (nb - max_exact)).astype(jnp.int32)
    large = jnp.minimum(large, nb - 1)
    return ret + jnp.where(n < max_exact, n, large)


def _swa_bias_tables(rel_bias, lp):
    def lookup(bucket):
        out = jnp.zeros((rel_bias.shape[1],) + bucket.shape, f32)
        for b in range(REL_BUCKETS):
            out = jnp.where((bucket == b)[None], rel_bias[b].astype(f32)[:, None, None], out)
        return out

    qi = jnp.arange(TILE)
    ki = jnp.arange(3 * TILE)
    rel = ki[None, :] - TILE - qi[:, None]
    band = jnp.where((jnp.abs(rel) <= WINDOW)[None], lookup(_t5_bucket(rel)), NEG)
    pos = jnp.arange(lp) - (TILE - N_META)
    rel_m = jnp.arange(N_META)[None, :] - pos[:, None]
    meta = lookup(_t5_bucket(rel_m))
    nt = lp // TILE
    band_t = jnp.transpose(band, (2, 0, 1)).reshape(3 * TILE, 4 * TILE) * LOG2E
    meta_t = (meta.reshape(4, nt, TILE, N_META).transpose(1, 3, 0, 2).reshape(nt, N_META, 4 * TILE)
              * LOG2E)
    return band_t, meta_t


def _row(v, width=None):
    v = v.astype(f32).reshape(1, -1)
    if width is not None and v.shape[1] < width:
        v = jnp.pad(v, ((0, 0), (0, width - v.shape[1])))
    return v


def kernel(x, meta_tokens, rel_bias, norm_mix_w, norm_ffn_w, w_in, ssd_conv_w, ssd_conv_b, ssd_dt_bias, ssd_a_log, ssd_d, ssd_norm_w, gla_gate_w2, gla_gate_b, gla_norm_w, swa_sink, gqa_q_norm_w, gqa_k_norm_w, w_out, ffn_w_gate, ffn_w_up, ffn_w_down, moe_router, moe_w_gate, moe_w_up, moe_w_down, final_norm_w):
    bsz, seq, d = x.shape
    depth = w_in.shape[0]
    pad = (-(seq + N_META)) % TILE
    assert pad == TILE - N_META and seq % TILE == 0
    lp = pad + N_META + seq
    n = bsz * lp

    meta = jnp.broadcast_to(meta_tokens[None].astype(x.dtype), (bsz, N_META, d))
    h = jnp.concatenate([jnp.zeros((bsz, pad, d), x.dtype), meta, x], axis=1).reshape(n, d)

    cos, sin = _rope_tables(seq, pad)
    rot = _pair_swap_matrix(256)
    gavg = _group_mean_matrix(256, HEAD_DIM)
    bias_band, bias_meta = _swa_bias_tables(rel_bias, lp)
    offs = [0]
    for s in IN_SIZES:
        offs.append(offs[-1] + s)

    for i in range(depth):
        wi = w_in[i]
        cols = [wi[:, offs[j]:offs[j + 1]] for j in PACK_ORDER]
        w_pack = jnp.concatenate(cols + [jnp.zeros((d, PACK_WIDTH - offs[-1]), wi.dtype)], axis=1).astype(bf16)
        o_ssd, o_small, o_gla, sk, ak, aq, av, sq, sv = _inproj(
            h, _row(norm_mix_w[i]), w_pack, cos, sin, rot, gavg,
            _row(jnp.tile(gqa_q_norm_w[i], 4)), _row(jnp.tile(gqa_k_norm_w[i], 2)), bsz, lp)

        convw = jnp.pad(ssd_conv_w[i].astype(f32), ((0, 8 - SSD_CONV), (0, 0)))
        convb = _row(ssd_conv_b[i])
        dtb = _row(ssd_dt_bias[i].reshape(-1), 128)
        alog = _row(ssd_a_log[i].reshape(-1), 128)
        yf, xbc = _ssd(False, o_ssd, o_small, convw, convb, dtb, alog,
                       _row(jnp.repeat(ssd_d[i], SSD_HEAD_DIM)), bsz, lp)
        y_ssd = _ssd(True, o_ssd, o_small, convw, convb, dtb, alog, (yf, xbc, _row(ssd_norm_w[i])), bsz, lp)

        def gate_w(direction):
            lo = SMALL_GA + GLA_GATE_RANK * direction
            full = jnp.zeros((128, GLA_KEY), f32).at[lo:lo + GLA_GATE_RANK].set(gla_gate_w2[i, direction].astype(f32))
            return full.astype(bf16)

        of = _gla(False, o_gla, o_small, gate_w(0), _row(gla_gate_b[i, 0]), None, bsz, lp)
        y_gla = _gla(True, o_gla, o_small, gate_w(1), _row(gla_gate_b[i, 1]),
                     (of, _row(jnp.tile(gla_norm_w[i], GLA_HEADS)), gavg), bsz, lp)

        sink = _row(jnp.repeat(swa_sink[i].astype(f32), TILE)) * LOG2E
        y_swa = _swa(sq, sk, sv, bias_band, bias_meta, sink, bsz, lp)
        y_g2 = _flash(aq, ak, av, bsz, lp)

        h = _outproj((y_ssd, y_gla, y_swa, y_g2), h, w_out[i].astype(bf16))

        j = i // 2
        if i % 2 == 0:
            h = _ffn(h, _row(norm_ffn_w[i]), ffn_w_gate[j].astype(bf16), ffn_w_up[j].astype(bf16),
                     ffn_w_down[j].astype(bf16))
            if i == depth - 1:
                h = _final_norm(h, _row(final_norm_w))
        else:
            router = jnp.pad(moe_router[j].astype(f32), ((0, 0), (0, 128 - N_EXPERTS)))
            h = _moe(h, _row(norm_ffn_w[i]), router, moe_w_gate[j].astype(bf16), moe_w_up[j].astype(bf16),
                     moe_w_down[j].astype(bf16), _row(final_norm_w), i == depth - 1)
    return h.reshape(bsz, lp, d)[:, pad + N_META:]
```

```python
import functools
import math

import jax
import jax.numpy as jnp
from jax import lax
from jax.experimental import pallas as pl
from jax.experimental.pallas import tpu as pltpu

f32 = jnp.float32
bf16 = jnp.bfloat16

N_META = 16
HEAD_DIM = 64
GRID_W = 64
EPS = 1e-6
ROPE_THETA = 10000.0
TILE = 128
SSD_HEADS = 4
SSD_HEAD_DIM = 64
SSD_INNER = 256
SSD_STATE = 128
SSD_CONV = 5
SSD_CONV_DIM = 768
GLA_HEADS = 4
GLA_DK = 32
GLA_DV = 64
GLA_KEY = 128
GLA_VAL = 256
GLA_GATE_RANK = 16
GLA_GATE_NORM = 16.0
GLA_CHUNK = 64
WINDOW = 128
REL_BUCKETS = 32
REL_MAX_DIST = 128
N_EXPERTS = 8
NEG = -1e30
LOG2E = math.log2(math.e)
CONV_HALO = 8
VMEM_LIMIT = 56 * 1024 * 1024

IN_SIZES = (256, 768, 8, 128, 128, 256, 256, 32, 256, 128, 128, 256, 128, 128)
(_Z, _XBC, _DT, _GQ, _GK, _GV, _GR, _GA, _SQ, _SK, _SV, _AQ, _AK, _AV) = range(14)
PACK_ORDER = (_XBC, _Z, _GQ, _GK, _GV, _GR, _SQ, _SK, _SV, _AQ, _AK, _AV, _DT, _GA)
PACK_WIDTH = 2944
C_SSD, C_GLA, C_SWA, C_AQ, C_AK, C_AV, C_SMALL = 0, 1024, 1792, 2304, 2560, 2688, 2816
SMALL_DT, SMALL_GA = 0, 8


def _cparams(*sem):
    return pltpu.CompilerParams(dimension_semantics=sem, vmem_limit_bytes=VMEM_LIMIT)


def _dot(a, b):
    return jnp.dot(a, b, preferred_element_type=f32)


def _dot_nt(a, b):
    return lax.dot_general(a, b, (((1,), (1,)), ((), ())), preferred_element_type=f32)


def _dot_tn(a, b):
    return lax.dot_general(a, b, (((0,), (0,)), ((), ())), preferred_element_type=f32)


def _split(a):
    hi = a.astype(bf16)
    lo = (a - hi.astype(f32)).astype(bf16)
    return hi, lo


def _dot_split_lhs(a, b):
    hi, lo = _split(a)
    return _dot(hi, b) + _dot(lo, b)


def _dot_split_rhs(t, x):
    hi, lo = _split(x)
    return _dot(t, hi) + _dot(t, lo)


def _rms(x, w):
    return x * lax.rsqrt(jnp.mean(x * x, axis=-1, keepdims=True) + EPS) * w


def _silu(x):
    return x / (1.0 + jnp.exp(-x))


def _softplus(x):
    return jnp.maximum(x, 0.0) + jnp.log(1.0 + jnp.exp(-jnp.abs(x)))


def _log_sigmoid(x):
    return jnp.minimum(x, 0.0) - jnp.log(1.0 + jnp.exp(-jnp.abs(x)))


def _tri(n, rev):
    r = lax.broadcasted_iota(jnp.int32, (n, n), 0)
    c = lax.broadcasted_iota(jnp.int32, (n, n), 1)
    return (r <= c) if rev else (r >= c)


def _valid_rows(tile, n):
    rows = lax.broadcasted_iota(jnp.int32, (n, 1), 0)
    return jnp.logical_or(tile > 0, rows >= TILE - N_META)


def _inproj_body(h_ref, nw_ref, w_ref, cos_ref, sin_ref, rot_ref, gavg_ref, qnw_ref, knw_ref,
                 ssd_ref, small_ref, gla_ref, sk_ref, k_ref, q_ref, v_ref, sq_ref, sv_ref):
    u = _rms(h_ref[...], nw_ref[...]).astype(bf16)

    def mm(lo, hi):
        return _dot(u, w_ref[:, lo:hi])

    ssd_ref[...] = mm(C_SSD, C_GLA)
    gla_ref[...] = mm(C_GLA, C_SWA)
    small_ref[...] = mm(C_SMALL, PACK_WIDTH)
    akv = mm(C_AK, C_SMALL)
    v_ref[...] = akv[:, 128:].T.astype(bf16)

    skv = mm(C_SWA + 256, C_AQ)
    sk_ref[...] = skv[:, :128].astype(bf16)
    sv_ref[...] = skv[:, 128:].T.astype(bf16)
    sqt = (mm(C_SWA, C_SWA + 256) * (HEAD_DIM ** -0.5 * LOG2E)).T.astype(bf16)
    sq_ref[...] = jnp.zeros_like(sq_ref)
    for t in range(sq_ref.shape[0]):
        for hh in range(4):
            lo = HEAD_DIM * (hh // 2)
            sq_ref[t, lo:lo + HEAD_DIM, TILE * hh:TILE * (hh + 1)] = (
                sqt[HEAD_DIM * hh:HEAD_DIM * (hh + 1), TILE * t:TILE * (t + 1)])

    def norm_rope(t, w, width):
        ms = _dot_split_lhs(t * t, gavg_ref[:width, :width])
        tn = t * lax.rsqrt(ms + EPS) * w
        tr = _dot(tn.astype(bf16), rot_ref[:width, :width])
        return tn * cos_ref[:, :width] + tr * sin_ref[:, :width]

    q = norm_rope(mm(C_AQ, C_AK), qnw_ref[...], 256) * (HEAD_DIM ** -0.5 * LOG2E)
    q_ref[...] = q.T.astype(bf16)
    k_ref[...] = norm_rope(akv[:, :128], knw_ref[...], 128).astype(bf16)


def _inproj(h, nw, w, cos, sin, rot, gavg, qnw, knw, bsz, lp):
    n = h.shape[0]
    d = h.shape[1]
    tm = TILE * _largest_divisor(lp // TILE, 3)
    per = lp // tm
    row = lambda b, i: (b * per + i, 0)
    const = lambda b, i: (0, 0)
    tab = lambda b, i: (i, 0)
    outs = ((1024, f32), (128, f32), (768, f32), (128, bf16), (128, bf16))
    slab = lambda b, i: (b * per + i, 0, 0)
    sub = tm // TILE
    return pl.pallas_call(
        _inproj_body,
        grid=(bsz, per),
        in_specs=[pl.BlockSpec((tm, d), row), pl.BlockSpec((1, d), const),
                  pl.BlockSpec((d, PACK_WIDTH), const),
                  pl.BlockSpec((tm, 256), tab), pl.BlockSpec((tm, 256), tab),
                  pl.BlockSpec((256, 256), const), pl.BlockSpec((256, 256), const),
                  pl.BlockSpec((1, 256), const), pl.BlockSpec((1, 128), const)],
        out_specs=([pl.BlockSpec((tm, c), row) for c, _ in outs]
                   + [pl.BlockSpec((None, 256, tm), slab), pl.BlockSpec((None, 128, tm), slab),
                      pl.BlockSpec((sub, 128, 4 * TILE), slab), pl.BlockSpec((None, 128, tm), slab)]),
        out_shape=([jax.ShapeDtypeStruct((n, c), t) for c, t in outs]
                   + [jax.ShapeDtypeStruct((n // tm, 256, tm), bf16),
                      jax.ShapeDtypeStruct((n // tm, 128, tm), bf16),
                      jax.ShapeDtypeStruct((n // TILE, 128, 4 * TILE), bf16),
                      jax.ShapeDtypeStruct((n // tm, 128, tm), bf16)]),
        compiler_params=_cparams("parallel", "parallel"),
        name="inproj",
    )(h, nw, w, cos, sin, rot, gavg, qnw, knw)


def _largest_divisor(n, cap):
    return max(k for k in range(1, cap + 1) if n % k == 0)


def _for_each_batch(chain, batched, n_scratch):
    def body(*refs):
        nb = next(r.shape[0] for r, flag in zip(refs, batched) if flag)
        io, scratch = refs[:len(batched)], refs[len(batched):]
        assert len(scratch) == n_scratch * nb
        for b in range(nb):
            chain(*[r.at[b] if flag else r for r, flag in zip(io, batched)],
                  *[scratch[k * nb + b] for k in range(n_scratch)])
    return body


def _per_batch_scratch(nb, *shapes):
    return [pltpu.VMEM(shape, dtype) for shape, dtype in shapes for _ in range(nb)]


def _ssd_scan_tile(rev, xs, bmat, cmat, dt, cum, tot, st):
    col0 = SMALL_DT + (SSD_HEADS if rev else 0)
    causal = _tri(TILE, rev)
    cum_t = cum.T
    e_cum = jnp.exp(cum)
    e_tot = jnp.exp(tot)
    head_of = lax.broadcasted_iota(jnp.int32, (TILE, SSD_INNER), 1) // SSD_HEAD_DIM
    head_of_row = lax.broadcasted_iota(jnp.int32, (1, SSD_INNER), 1) // SSD_HEAD_DIM

    def widen(cols, like):
        out = jnp.zeros(like.shape, f32)
        for hh in range(SSD_HEADS):
            out = jnp.where(like == hh, cols[:, col0 + hh:col0 + hh + 1], out)
        return out

    def own_head(r):
        out = jnp.zeros((TILE, SSD_INNER), f32)
        for hh in range(SSD_HEADS):
            out = jnp.where(head_of == hh, r[TILE * hh:TILE * (hh + 1)], out)
        return out

    xd = (xs * widen(dt, head_of)).astype(bf16)
    scores, bws = [], []
    for g in range(2):
        bg = bmat[:, SSD_STATE * g:SSD_STATE * (g + 1)]
        cbg = _dot_nt(cmat[:, SSD_STATE * g:SSD_STATE * (g + 1)], bg)
        for hh in (2 * g, 2 * g + 1):
            col = col0 + hh
            a_col = cum[:, col:col + 1]
            decay = jnp.where(causal, jnp.exp(a_col - cum_t[col:col + 1, :]), 0.0)
            scores.append((cbg * decay).astype(bf16))
            bws.append((bg.astype(f32) * jnp.exp(tot[:, col:col + 1] - a_col)).astype(bf16))
    y = own_head(_dot(jnp.concatenate(scores, axis=0), xd))
    y = y + _dot(cmat, st.astype(bf16)) * widen(e_cum, head_of)
    upd = _dot_tn(jnp.concatenate(bws, axis=1), xd)
    rows = [jnp.where(head_of == 2 * g, upd[SSD_STATE * 2 * g:SSD_STATE * (2 * g + 1)],
                      jnp.where(head_of == 2 * g + 1,
                                upd[SSD_STATE * (2 * g + 1):SSD_STATE * (2 * g + 2)], 0.0))
            for g in range(2)]
    return y, st * widen(e_tot, head_of_row) + jnp.concatenate(rows, axis=0)


def _ssd_body(rev, nt, *refs):
    if rev:
        xbc_ref, small_ref, dtb_ref, alog_ref, yf_ref, z_ref, normw_ref, out_ref, state_ref = refs
    else:
        (cur_ref, prev_ref, next_ref, small_ref, convw_ref, convb_ref, dtb_ref, alog_ref, dskip_ref,
         out_ref, xbc_out_ref, state_ref) = refs[:12]
        ext_refs = refs[12:]
    c = pl.program_id(0)
    tile = (nt - 1 - c) if rev else c
    nb = small_ref.shape[0]
    valid = _valid_rows(tile, TILE)

    a = -jnp.exp(alog_ref[...])
    dts = [jnp.where(valid, _softplus(small_ref[b] + dtb_ref[...]), 0.0) for b in range(nb)]
    cum_all = _dot_split_rhs(_tri(TILE, rev).astype(bf16),
                             jnp.concatenate([dt * a for dt in dts], axis=1))
    for b in range(nb):
        dt = dts[b]
        if rev:
            xbc = xbc_ref[b]
            xs = xbc[:, :SSD_INNER].astype(f32)
            bc = xbc[:, SSD_INNER:]
        else:
            ext_ref = ext_refs[b]
            ext_ref[0:CONV_HALO, :] = jnp.where(tile > 0, prev_ref[b], 0.0)
            ext_ref[CONV_HALO:CONV_HALO + TILE, :] = jnp.where(valid, cur_ref[b], 0.0)
            ext_ref[CONV_HALO + TILE:, :] = jnp.where(tile < nt - 1, next_ref[b], 0.0)
            acc = jnp.zeros((TILE, SSD_CONV_DIM), f32) + convb_ref[...]
            first = CONV_HALO - (SSD_CONV - 1) // 2
            for k in range(SSD_CONV):
                acc = acc + convw_ref[k:k + 1, :] * ext_ref[first + k:first + k + TILE, :]
            xbc = jnp.where(valid, _silu(acc), 0.0)
            xbc_out_ref[b] = xbc.astype(bf16)
            xs = xbc[:, :SSD_INNER]
            bc = xbc[:, SSD_INNER:].astype(bf16)
        st = jnp.where(c == 0, 0.0, state_ref[b])
        y, st = _ssd_scan_tile(rev, xs, bc[:, :2 * SSD_STATE], bc[:, 2 * SSD_STATE:], dt,
                               cum_all[:, 128 * b:128 * (b + 1)],
                               jnp.sum(dt * a, axis=0, keepdims=True), st)
        state_ref[b] = st
        if rev:
            y = (yf_ref[b] + y) * _silu(z_ref[b])
            out_ref[b] = jnp.where(valid, _rms(y, normw_ref[...]), 0.0).astype(out_ref.dtype)
        else:
            out_ref[b] = y + dskip_ref[...] * xs


def _ssd(rev, o_ssd, o_small, convw, convb, dtb, alog, extra, bsz, lp):
    n = o_ssd.shape[0]
    nt = lp // TILE
    hb = TILE // CONV_HALO
    o_ssd = o_ssd.reshape(bsz, lp, -1)
    o_small = o_small.reshape(bsz, lp, -1)

    def tile_of(c):
        return (nt - 1 - c) if rev else c

    cur = lambda c: (0, tile_of(c), 0)
    prev = lambda c: (0, jnp.maximum(tile_of(c) * hb - 1, 0), 0)
    nxt = lambda c: (0, jnp.minimum((tile_of(c) + 1) * hb, lp // CONV_HALO - 1), 0)
    zcol = lambda c: (0, tile_of(c), SSD_CONV_DIM // SSD_INNER)
    const = lambda c: (0, 0)
    tile3 = lambda width: pl.BlockSpec((bsz, TILE, width), cur)
    state = pltpu.VMEM((bsz, 2 * SSD_STATE, SSD_INNER), f32)
    if rev:
        yf, xbc, normw = extra
        out = pl.pallas_call(
            functools.partial(_ssd_body, rev, nt),
            grid=(nt,),
            in_specs=[tile3(SSD_CONV_DIM), tile3(128), pl.BlockSpec((1, 128), const),
                      pl.BlockSpec((1, 128), const), tile3(SSD_INNER),
                      pl.BlockSpec((bsz, TILE, SSD_INNER), zcol), pl.BlockSpec((1, SSD_INNER), const)],
            out_specs=tile3(SSD_INNER),
            out_shape=jax.ShapeDtypeStruct((bsz, lp, SSD_INNER), bf16),
            scratch_shapes=[state],
            compiler_params=_cparams("arbitrary"),
            name="ssd_rev",
        )(xbc, o_small, dtb, alog, yf, o_ssd, normw)
        return out.reshape(n, SSD_INNER)
    halo = lambda im: pl.BlockSpec((bsz, CONV_HALO, SSD_CONV_DIM), im)
    return pl.pallas_call(
        functools.partial(_ssd_body, rev, nt),
        grid=(nt,),
        in_specs=[tile3(SSD_CONV_DIM), halo(prev), halo(nxt), tile3(128),
                  pl.BlockSpec((8, SSD_CONV_DIM), const), pl.BlockSpec((1, SSD_CONV_DIM), const),
                  pl.BlockSpec((1, 128), const), pl.BlockSpec((1, 128), const),
                  pl.BlockSpec((1, SSD_INNER), const)],
        out_specs=[tile3(SSD_INNER), tile3(SSD_CONV_DIM)],
        out_shape=[jax.ShapeDtypeStruct((bsz, lp, SSD_INNER), f32),
                   jax.ShapeDtypeStruct((bsz, lp, SSD_CONV_DIM), bf16)],
        scratch_shapes=[state] + _per_batch_scratch(bsz, ((TILE + 2 * CONV_HALO, SSD_CONV_DIM), f32)),
        compiler_params=_cparams("arbitrary"),
        name="ssd_fwd",
    )(o_ssd, o_ssd, o_ssd, o_small, convw, convb, dtb, alog, extra)


def _gla_body(rev, nt, *refs):
    if rev:
        x_ref, small_ref, wg_ref, gb_ref, of_ref, normw_ref, gavg_ref, out_ref, st_ref = refs
    else:
        x_ref, small_ref, wg_ref, gb_ref, out_ref, st_ref = refs
    c = pl.program_id(0)
    tile = (nt - 1 - c) if rev else c
    nb = x_ref.shape[0]
    n = GLA_CHUNK
    valid = _valid_rows(tile, TILE)

    pre = _dot(small_ref[...].reshape(nb * TILE, 128).astype(bf16), wg_ref[...]) + gb_ref[...]
    g_all = _log_sigmoid(pre) / GLA_GATE_NORM
    gs = [jnp.where(valid, g_all[TILE * b:TILE * (b + 1)], 0.0) for b in range(nb)]
    row = lax.broadcasted_iota(jnp.int32, (TILE, TILE), 0)
    col = lax.broadcasted_iota(jnp.int32, (TILE, TILE), 1)
    same_chunk = (row // n) == (col // n)
    cum_mat = jnp.where(jnp.logical_and(same_chunk, (row <= col) if rev else (row >= col)), 1.0, 0.0)
    bc_all = _dot_split_rhs(cum_mat.astype(bf16), jnp.concatenate(gs, axis=1))

    first = slice(n, 2 * n) if rev else slice(0, n)
    second = slice(0, n) if rev else slice(n, 2 * n)
    rows = lax.broadcasted_iota(jnp.int32, (TILE, 1), 0)
    in_first = (rows >= n) if rev else (rows < n)
    tri = _tri(n, rev)
    tri4 = jnp.concatenate([tri] * GLA_HEADS, axis=0)
    lane_head = lax.broadcasted_iota(jnp.int32, (n, GLA_KEY), 1) // GLA_DK
    out_head = lax.broadcasted_iota(jnp.int32, (n, GLA_VAL), 1) // GLA_DV
    blockdiag = (lax.broadcasted_iota(jnp.int32, (GLA_VAL, GLA_KEY), 0) // GLA_DV
                 == lax.broadcasted_iota(jnp.int32, (GLA_VAL, GLA_KEY), 1) // GLA_DK)

    def stack_heads(a):
        return jnp.concatenate([jnp.where(lane_head == hh, a, 0.0) for hh in range(GLA_HEADS)],
                               axis=0).astype(bf16)

    def own_head(r):
        out = jnp.zeros((n, GLA_VAL), f32)
        for hh in range(GLA_HEADS):
            out = jnp.where(out_head == hh, r[n * hh:n * (hh + 1)], out)
        return out

    both = []
    for b in range(nb):
        x = x_ref[b]
        g = gs[b]
        bc = bc_all[:, 128 * b:128 * (b + 1)]
        q = jnp.where(valid, x[:, :GLA_KEY], 0.0) * GLA_DK ** -0.5
        k = jnp.where(valid, x[:, GLA_KEY:2 * GLA_KEY], 0.0)
        v = jnp.where(valid, x[:, 2 * GLA_KEY:2 * GLA_KEY + GLA_VAL], 0.0).astype(bf16)
        bl_first = jnp.sum(g[first], axis=0, keepdims=True)
        bl_second = jnp.sum(g[second], axis=0, keepdims=True)
        qt = q * jnp.exp(bc)
        kt = k * jnp.exp(-bc)
        kw = k * jnp.exp(jnp.where(in_first, bl_first, bl_second) - bc)

        att_f = jnp.where(tri4, _dot_nt(stack_heads(qt[first]), kt[first].astype(bf16)), 0.0)
        o_f = own_head(_dot(att_f.astype(bf16), v[first]))
        keys = jnp.concatenate([kw[first], kt[second]], axis=0).astype(bf16)
        vals = jnp.concatenate([v[first], v[second]], axis=0)
        att_s = _dot_nt(stack_heads(qt[second]), keys)
        att_s = jnp.concatenate([att_s[:, :n], jnp.where(tri4, att_s[:, n:], 0.0)], axis=1)
        o_s = own_head(_dot(att_s.astype(bf16), vals))

        st = jnp.where(c == 0, 0.0, st_ref[b])
        q_in = qt * jnp.exp(jnp.where(in_first, 0.0, bl_first))
        o = _dot_nt(q_in.astype(bf16), st.astype(bf16)) + jnp.concatenate(
            [o_s, o_f] if rev else [o_f, o_s], axis=0)
        k_out = (kw * jnp.exp(jnp.where(in_first, bl_second, 0.0))).astype(bf16)
        st_ref[b] = st * jnp.exp(bl_first + bl_second) + jnp.where(blockdiag, _dot_tn(v, k_out), 0.0)

        if rev:
            both.append(of_ref[b] + o)
        else:
            out_ref[b] = o

    if rev:
        o = jnp.concatenate(both, axis=0)
        ms = _dot_split_lhs(o * o, gavg_ref[...])
        on = o * lax.rsqrt(ms + EPS) * normw_ref[...]
        for b in range(nb):
            r = x_ref[b, :, 2 * GLA_KEY + GLA_VAL:]
            out_ref[b] = jnp.where(valid, on[TILE * b:TILE * (b + 1)] * _silu(r), 0.0).astype(out_ref.dtype)


def _gla(rev, o_gla, o_small, wg, gb, extra, bsz, lp):
    n = o_gla.shape[0]
    nt = lp // TILE
    cur = lambda c: (0, (nt - 1 - c) if rev else c, 0)
    const = lambda c: (0, 0)
    in_specs = [pl.BlockSpec((bsz, TILE, 768), cur), pl.BlockSpec((bsz, TILE, 128), cur),
                pl.BlockSpec((128, GLA_KEY), const), pl.BlockSpec((1, GLA_KEY), const)]
    args = [o_gla.reshape(bsz, lp, -1), o_small.reshape(bsz, lp, -1), wg, gb]
    if rev:
        of, normw, gavg = extra
        in_specs += [pl.BlockSpec((bsz, TILE, GLA_VAL), cur), pl.BlockSpec((1, GLA_VAL), const),
                     pl.BlockSpec((GLA_VAL, GLA_VAL), const)]
        args += [of.reshape(bsz, lp, -1), normw, gavg]
    out = pl.pallas_call(
        functools.partial(_gla_body, rev, nt),
        grid=(nt,),
        in_specs=in_specs,
        out_specs=pl.BlockSpec((bsz, TILE, GLA_VAL), cur),
        out_shape=jax.ShapeDtypeStruct((bsz, lp, GLA_VAL), bf16 if rev else f32),
        scratch_shapes=[pltpu.VMEM((bsz, GLA_VAL, GLA_KEY), f32)],
        compiler_params=_cparams("arbitrary"),
        name="gla_rev" if rev else "gla_fwd",
    )(*args)
    return out.reshape(n, GLA_VAL)


def _swa_body(nt, qp_ref, kp_ref, kc_ref, kn_ref, km_ref, vp_ref, vc_ref, vn_ref, vm_ref,
              bias_ref, bmeta_ref, sink_ref, out_ref):
    c = pl.program_id(0)
    krow = lax.broadcasted_iota(jnp.int32, (3 * TILE, 1), 0)
    pen = jnp.where(krow < TILE, jnp.where(c >= 2, 0.0, NEG),
                    jnp.where(krow < 2 * TILE, jnp.where(c >= 1, 0.0, NEG),
                              jnp.where(c <= nt - 2, 0.0, NEG)))
    qp = qp_ref[...]
    kcat = jnp.concatenate([kp_ref[...], kc_ref[...], kn_ref[...]], axis=0)
    s = _dot(kcat, qp) + bias_ref[...] + pen
    sm = _dot(km_ref[...], qp) + bmeta_ref[...]
    sk = sink_ref[...]
    m = jnp.maximum(jnp.maximum(jnp.max(s, axis=0, keepdims=True),
                                jnp.max(sm, axis=0, keepdims=True)), sk)
    p = jnp.exp2(s - m)
    pm = jnp.exp2(sm - m)
    inv = 1.0 / (jnp.sum(p, axis=0, keepdims=True) + jnp.sum(pm, axis=0, keepdims=True)
                 + jnp.exp2(sk - m))
    pb = p.astype(bf16)
    pmb = jnp.concatenate([jnp.zeros((TILE - N_META, 4 * TILE), bf16), pm.astype(bf16)], axis=0)
    vcat = jnp.concatenate([vp_ref[...], vc_ref[...], vn_ref[...]], axis=1)
    heads = []
    for g in range(2):
        rows = slice(HEAD_DIM * g, HEAD_DIM * (g + 1))
        cols = slice(2 * TILE * g, 2 * TILE * (g + 1))
        pv = (_dot(vcat[rows, :], pb[:, cols]) + _dot(vm_ref[rows, :], pmb[:, cols])) * inv[:, cols]
        heads += [pv[:, :TILE], pv[:, TILE:]]
    o = jnp.concatenate(heads, axis=0).T
    out_ref[...] = jnp.where(_valid_rows(c, TILE), o, 0.0).astype(out_ref.dtype)


def _swa(qp, k, vt, bias_band, bias_meta, sink, bsz, lp):
    n = k.shape[0]
    nt = lp // TILE
    per, _, tm = vt.shape[0] // bsz, vt.shape[1], vt.shape[2]
    sub = tm // TILE
    qp = qp.reshape(bsz, nt, 128, 4 * TILE)
    k = k.reshape(bsz, lp, 128)
    vt = vt.reshape(bsz, per, 128, tm)
    prev = lambda c: jnp.maximum(c - 1, 0)
    nxt = lambda c: jnp.minimum(c + 1, nt - 1)
    kspec = lambda tile: pl.BlockSpec((bsz, TILE, 128), lambda c: (0, tile(c), 0))
    vspec = lambda tile: pl.BlockSpec((bsz, None, 128, TILE), lambda c: (0, tile(c) // sub, 0, tile(c) % sub))
    same = lambda c: c
    first = lambda c: 0
    out = pl.pallas_call(
        _for_each_batch(functools.partial(_swa_body, nt), [True] * 9 + [False] * 3 + [True], 0),
        grid=(nt,),
        in_specs=[pl.BlockSpec((bsz, None, 128, 4 * TILE), lambda c: (0, c, 0, 0)),
                  kspec(prev), kspec(same), kspec(nxt),
                  pl.BlockSpec((bsz, N_META, 128), lambda c: (0, TILE // N_META - 1, 0)),
                  vspec(prev), vspec(same), vspec(nxt), vspec(first),
                  pl.BlockSpec((3 * TILE, 4 * TILE), lambda c: (0, 0)),
                  pl.BlockSpec((None, N_META, 4 * TILE), lambda c: (c, 0, 0)),
                  pl.BlockSpec((1, 4 * TILE), lambda c: (0, 0))],
        out_specs=pl.BlockSpec((bsz, TILE, 256), lambda c: (0, c, 0)),
        out_shape=jax.ShapeDtypeStruct((bsz, lp, 256), bf16),
        compiler_params=_cparams("arbitrary"),
        name="swa",
    )(qp, k, k, k, k, vt, vt, vt, vt, bias_band, bias_meta, sink)
    return out.reshape(n, 256)


def _flash_body(nk, tq, tk, qt_ref, k_ref, vt_ref, out_ref, qpad_ref, m_ref, l_ref, acc_ref,
                sa_ref, sb_ref):
    i = pl.program_id(1)
    krow = lax.broadcasted_iota(jnp.int32, (tk, 1), 0)
    qpad_ref[...] = jnp.zeros_like(qpad_ref)
    for hh in range(4):
        lo = HEAD_DIM * (hh // 2)
        qpad_ref[lo:lo + HEAD_DIM, tq * hh:tq * (hh + 1)] = qt_ref[HEAD_DIM * hh:HEAD_DIM * (hh + 1), :]
    m_ref[...] = jnp.full_like(m_ref, NEG)
    l_ref[...] = jnp.zeros_like(l_ref)
    acc_ref[...] = jnp.zeros_like(acc_ref)

    def scores(j, s_ref, first=False):
        s = _dot(k_ref[pl.ds(pl.multiple_of(j * tk, tk), tk), :], qpad_ref[...])
        if first:
            s = jnp.where(krow >= TILE - N_META, s, NEG)
        s_ref[...] = s

    def absorb(j, s_ref):
        s = s_ref[...]
        m_old = m_ref[...]
        m_new = jnp.maximum(m_old, jnp.max(s, axis=0, keepdims=True))
        alpha = jnp.exp2(m_old - m_new)
        p = jnp.exp2(s - m_new)
        l_ref[...] = alpha * l_ref[...] + jnp.sum(p, axis=0, keepdims=True)
        m_ref[...] = m_new
        pb = p.astype(bf16)
        for g in range(2):
            pv = _dot(vt_ref[j, HEAD_DIM * g:HEAD_DIM * (g + 1), :], pb[:, 2 * g * tq:(2 * g + 2) * tq])
            for r in range(2):
                hh = 2 * g + r
                rows = slice(HEAD_DIM * hh, HEAD_DIM * (hh + 1))
                acc_ref[rows, :] = (alpha[:, tq * hh:tq * (hh + 1)] * acc_ref[rows, :]
                                    + pv[:, tq * r:tq * (r + 1)])

    scores(0, sa_ref, first=True)

    def body(jj, carry):
        j = 2 * jj
        scores(j + 1, sb_ref)
        absorb(j, sa_ref)
        scores(j + 2, sa_ref)
        absorb(j + 1, sb_ref)
        return carry

    lax.fori_loop(0, (nk - 1) // 2, body, 0)
    if nk % 2 == 1:
        absorb(nk - 1, sa_ref)
    else:
        scores(nk - 1, sb_ref)
        absorb(nk - 2, sa_ref)
        absorb(nk - 1, sb_ref)
    linv = 1.0 / l_ref[...]
    for hh in range(4):
        rows = slice(HEAD_DIM * hh, HEAD_DIM * (hh + 1))
        acc_ref[rows, :] = acc_ref[rows, :] * linv[:, tq * hh:tq * (hh + 1)]
    rows = lax.broadcasted_iota(jnp.int32, (tq, 1), 0)
    valid = jnp.logical_or(i > 0, rows >= TILE - N_META)
    out_ref[...] = jnp.where(valid, acc_ref[...].T, 0.0).astype(out_ref.dtype)


def _flash(qt, k, vt, bsz, lp):
    n = k.shape[0]
    tq = qt.shape[2]
    per = lp // tq
    return pl.pallas_call(
        functools.partial(_flash_body, per, tq, tq),
        grid=(bsz, per),
        in_specs=[pl.BlockSpec((None, 256, tq), lambda b, i: (b * per + i, 0, 0)),
                  pl.BlockSpec((lp, 128), lambda b, i: (b, 0)),
                  pl.BlockSpec((per, 128, tq), lambda b, i: (b, 0, 0))],
        out_specs=pl.BlockSpec((tq, 256), lambda b, i: (b * per + i, 0)),
        out_shape=jax.ShapeDtypeStruct((n, 256), bf16),
        scratch_shapes=[pltpu.VMEM((128, 4 * tq), bf16), pltpu.VMEM((1, 4 * tq), f32),
                        pltpu.VMEM((1, 4 * tq), f32), pltpu.VMEM((256, tq), f32),
                        pltpu.VMEM((tq, 4 * tq), f32), pltpu.VMEM((tq, 4 * tq), f32)],
        compiler_params=_cparams("parallel", "parallel"),
        name="gqa_full",
    )(qt, k, vt)


def _outproj_body(y0_ref, y1_ref, y2_ref, y3_ref, h_ref, w_ref, out_ref):
    mixed = jnp.concatenate([y0_ref[...], y1_ref[...], y2_ref[...], y3_ref[...]], axis=1)
    out_ref[...] = h_ref[...] + _dot(mixed, w_ref[...])


def _outproj(ys, h, w):
    n, d = h.shape
    tm = TILE * _largest_divisor(n // TILE, 4)
    row = lambda i: (i, 0)
    return pl.pallas_call(
        _outproj_body,
        grid=(n // tm,),
        in_specs=[pl.BlockSpec((tm, 256), row)] * 4 + [pl.BlockSpec((tm, d), row),
                                                      pl.BlockSpec((d, d), lambda i: (0, 0))],
        out_specs=pl.BlockSpec((tm, d), row),
        out_shape=jax.ShapeDtypeStruct((n, d), f32),
        compiler_params=_cparams("parallel"),
        name="outproj",
    )(*ys, h, w)


def _ffn_body(h_ref, nw_ref, wg_ref, wu_ref, wd_ref, out_ref, u_ref, acc_ref):
    j = pl.program_id(1)

    @pl.when(j == 0)
    def _():
        u_ref[...] = _rms(h_ref[...], nw_ref[...]).astype(bf16)
        acc_ref[...] = jnp.zeros_like(acc_ref)

    u = u_ref[...]
    t = _silu(_dot(u, wg_ref[...])) * _dot(u, wu_ref[...])
    acc_ref[...] += _dot(t.astype(bf16), wd_ref[...])

    @pl.when(j == pl.num_programs(1) - 1)
    def _():
        out_ref[...] = h_ref[...] + acc_ref[...]


def _ffn(h, nw, wg, wu, wd):
    n, d = h.shape
    ff = wg.shape[1]
    tm = TILE * _largest_divisor(n // TILE, 4)
    tf = 128 * _largest_divisor(ff // 128, 11)
    row = lambda i, j: (i, 0)
    return pl.pallas_call(
        _ffn_body,
        grid=(n // tm, ff // tf),
        in_specs=[pl.BlockSpec((tm, d), row), pl.BlockSpec((1, d), lambda i, j: (0, 0)),
                  pl.BlockSpec((d, tf), lambda i, j: (0, j)), pl.BlockSpec((d, tf), lambda i, j: (0, j)),
                  pl.BlockSpec((tf, d), lambda i, j: (j, 0))],
        out_specs=pl.BlockSpec((tm, d), row),
        out_shape=jax.ShapeDtypeStruct((n, d), f32),
        scratch_shapes=[pltpu.VMEM((tm, d), bf16), pltpu.VMEM((tm, d), f32)],
        compiler_params=_cparams("parallel", "arbitrary"),
        name="ffn",
    )(h, nw, wg, wu, wd)


MOE_TM = 1024
ROUTE_E, ROUTE_RANK, ROUTE_GATE, ROUTE_W = 0, 2, 4, 8


def _router_body(h_ref, nw_ref, r_ref, route_ref, cnt_ref, base_ref):
    @pl.when(pl.program_id(0) == 0)
    def _():
        base_ref[...] = jnp.zeros_like(base_ref)

    u = _rms(h_ref[...], nw_ref[...])
    u_hi, u_lo = _split(u)
    r_hi, r_lo = _split(r_ref[...])
    logits = _dot(u_hi, r_hi) + _dot(u_lo, r_hi) + _dot(u_hi, r_lo)
    tm = logits.shape[0]
    lane = lax.broadcasted_iota(jnp.int32, logits.shape, 1)
    logits = jnp.where(lane < N_EXPERTS, logits, NEG)
    m1 = jnp.max(logits, axis=-1, keepdims=True)
    i1 = jnp.min(jnp.where(logits == m1, lane, 128), axis=-1, keepdims=True)
    rest = jnp.where(lane == i1, NEG, logits)
    m2 = jnp.max(rest, axis=-1, keepdims=True)
    i2 = jnp.min(jnp.where(rest == m2, lane, 128), axis=-1, keepdims=True)
    e2 = jnp.exp(m2 - m1)
    g1 = 1.0 / (1.0 + e2)
    g2 = e2 * g1

    sel1 = lane == i1
    sel2 = lane == i2
    onehot = jnp.where(sel1, 1.0, jnp.where(sel2, 1.0, 0.0))
    strict = jnp.where(lax.broadcasted_iota(jnp.int32, (tm, tm), 0)
                       > lax.broadcasted_iota(jnp.int32, (tm, tm), 1), 1.0, 0.0).astype(bf16)
    before = _dot(strict, onehot.astype(bf16)) + base_ref[0:1, :]
    r1 = jnp.sum(jnp.where(sel1, before, 0.0), axis=-1, keepdims=True)
    r2 = jnp.sum(jnp.where(sel2, before, 0.0), axis=-1, keepdims=True)
    route = jnp.zeros(logits.shape, f32)
    for k, val in enumerate((i1.astype(f32), i2.astype(f32), r1, r2, g1, g2)):
        route = jnp.where(lane == k, val, route)
    route_ref[...] = route[:, :ROUTE_W]
    base_ref[0:1, :] = base_ref[0:1, :] + jnp.sum(onehot, axis=0, keepdims=True)
    cnt_ref[...] = base_ref[...]


def _router(h, nw, router):
    n, d = h.shape
    tm = TILE * _largest_divisor(n // TILE, 8)
    return pl.pallas_call(
        _router_body,
        grid=(n // tm,),
        in_specs=[pl.BlockSpec((tm, d), lambda i: (i, 0)), pl.BlockSpec((1, d), lambda i: (0, 0)),
                  pl.BlockSpec((d, 128), lambda i: (0, 0))],
        out_specs=[pl.BlockSpec((tm, ROUTE_W), lambda i: (i, 0)), pl.BlockSpec((8, 128), lambda i: (0, 0))],
        out_shape=[jax.ShapeDtypeStruct((n, ROUTE_W), f32), jax.ShapeDtypeStruct((8, 128), f32)],
        scratch_shapes=[pltpu.VMEM((8, 128), f32)],
        compiler_params=_cparams("arbitrary"),
        name="moe_router",
    )(h, nw, router)


LANES = 128
ROW_DMA_UNROLL = 8


def _row_copy(src_ref, src_row, dst_ref, dst_row, sem):
    return pltpu.make_async_copy(src_ref.at[src_row], dst_ref.at[dst_row], sem)


def _to_slabs(dst_ref, val):
    dst_ref[...] = val.reshape(dst_ref.shape)


def _from_slabs(src_ref):
    rows, chunks, lanes = src_ref.shape
    return src_ref[...].reshape(rows, chunks * lanes)


def _dispatch_body(fill_ref, h_ref, nw_ref, dest_ref, xs_ref, u_ref, idx_ref, zero_ref,
                   sem_idx, sem_row, sem_fill):
    i = pl.program_id(0)
    tt = u_ref.shape[1]

    @pl.when(i == 0)
    def _():
        zero_ref[...] = jnp.zeros_like(zero_ref)
        fills = [pltpu.make_async_copy(zero_ref, xs_ref.at[pl.ds(fill_ref[e], MOE_TM)], sem_fill)
                 for e in range(N_EXPERTS)]
        for cp in fills:
            cp.start()
        for cp in fills:
            cp.wait()
        last = xs_ref.shape[0] // MOE_TM - 1
        for j in range(last - N_EXPERTS, last + 1):
            @pl.when(j >= fill_ref[N_EXPERTS])
            def _():
                cp = pltpu.make_async_copy(zero_ref, xs_ref.at[pl.ds(j * MOE_TM, MOE_TM)], sem_fill)
                cp.start()
                cp.wait()

    slot = i % 2
    rows_ref = u_ref.at[slot]
    sem = sem_row.at[slot]
    idx_copy = pltpu.make_async_copy(dest_ref.at[pl.ds(i * 2 * tt, 2 * tt)], idx_ref, sem_idx)
    idx_copy.start()
    _to_slabs(rows_ref, _rms(h_ref[...], nw_ref[...]))
    idx_copy.wait()

    def issue(r, carry):
        _row_copy(rows_ref, r, xs_ref, idx_ref[r], sem).start()
        _row_copy(rows_ref, r, xs_ref, idx_ref[tt + r], sem).start()
        return carry

    lax.fori_loop(0, tt, issue, 0, unroll=ROW_DMA_UNROLL)

    def drain(s):
        for _ in range(2):
            pltpu.make_async_copy(u_ref.at[s], xs_ref.at[pl.ds(0, tt)], sem_row.at[s]).wait()

    @pl.when(i > 0)
    def _():
        drain(1 - slot)

    @pl.when(i == pl.num_programs(0) - 1)
    def _():
        drain(slot)


def _dispatch(h, nw, dest_flat, fill_rows, rows, tt):
    n, d = h.shape
    return pl.pallas_call(
        _dispatch_body,
        grid_spec=pltpu.PrefetchScalarGridSpec(
            num_scalar_prefetch=1,
            grid=(n // tt,),
            in_specs=[pl.BlockSpec((tt, d), lambda i, fr: (i, 0)), pl.BlockSpec((1, d), lambda i, fr: (0, 0)),
                      pl.BlockSpec(memory_space=pl.ANY)],
            out_specs=pl.BlockSpec(memory_space=pl.ANY),
            scratch_shapes=[pltpu.VMEM((2, tt, d // LANES, LANES), f32), pltpu.SMEM((2 * tt,), jnp.int32),
                            pltpu.VMEM((MOE_TM, d // LANES, LANES), f32), pltpu.SemaphoreType.DMA(()),
                            pltpu.SemaphoreType.DMA((2,)), pltpu.SemaphoreType.DMA(())]),
        out_shape=jax.ShapeDtypeStruct((rows, d // LANES, LANES), f32),
        compiler_params=_cparams("arbitrary"),
        name="moe_dispatch",
    )(fill_rows, h, nw, dest_flat)


def _experts_body(te_ref, nu_ref, x_ref, wg_ref, wu_ref, wd_ref, y_ref, xb_ref, acc_ref):
    j = pl.program_id(0)
    f = pl.program_id(1)

    @pl.when(j < nu_ref[0])
    def _():
        @pl.when(f == 0)
        def _():
            xb_ref[...] = _from_slabs(x_ref).astype(bf16)
            acc_ref[...] = jnp.zeros_like(acc_ref)

        x = xb_ref[...]
        t = _silu(_dot(x, wg_ref[...])) * _dot(x, wu_ref[...])
        acc_ref[...] += _dot(t.astype(bf16), wd_ref[...])

        @pl.when(f == pl.num_programs(1) - 1)
        def _():
            _to_slabs(y_ref, acc_ref[...])

    @pl.when(jnp.logical_and(j >= nu_ref[0], f == pl.num_programs(1) - 1))
    def _():
        y_ref[...] = jnp.zeros_like(y_ref)


def _experts(xs, tile_expert, n_used, wg, wu, wd, n_tiles):
    d = wg.shape[1]
    slab = (MOE_TM, d // LANES, LANES)
    ff = wg.shape[2]
    tf = 128 * _largest_divisor(ff // 128, 4)
    nf = ff // tf
    tile = lambda j, nu: jnp.minimum(j, nu[0] - 1)
    chunk = lambda j, f, nu: jnp.where(j < nu[0], f, nf - 1)
    return pl.pallas_call(
        _experts_body,
        grid_spec=pltpu.PrefetchScalarGridSpec(
            num_scalar_prefetch=2,
            grid=(n_tiles, nf),
            in_specs=[pl.BlockSpec(slab, lambda j, f, te, nu: (tile(j, nu), 0, 0)),
                      pl.BlockSpec((None, d, tf), lambda j, f, te, nu: (te[tile(j, nu)], 0, chunk(j, f, nu))),
                      pl.BlockSpec((None, d, tf), lambda j, f, te, nu: (te[tile(j, nu)], 0, chunk(j, f, nu))),
                      pl.BlockSpec((None, tf, d), lambda j, f, te, nu: (te[tile(j, nu)], chunk(j, f, nu), 0))],
            out_specs=pl.BlockSpec(slab, lambda j, f, te, nu: (j, 0, 0)),
            scratch_shapes=[pltpu.VMEM((MOE_TM, d), bf16), pltpu.VMEM((MOE_TM, d), f32)]),
        out_shape=jax.ShapeDtypeStruct((n_tiles * MOE_TM, d // LANES, LANES), f32),
        compiler_params=_cparams("arbitrary", "arbitrary"),
        name="moe_experts",
    )(tile_expert, n_used, xs, wg, wu, wd)


def _combine_body(final, h_ref, route_ref, fnw_ref, dest_ref, ys_ref, out_ref, buf_ref, idx_ref,
                  sem_idx, sem_row):
    i = pl.program_id(0)
    tt = h_ref.shape[0]
    slot = i % 2

    def gather(step, s):
        idx_copy = pltpu.make_async_copy(dest_ref.at[pl.ds(step * 2 * tt, 2 * tt)], idx_ref, sem_idx)
        idx_copy.start()
        idx_copy.wait()

        def issue(r, carry):
            _row_copy(ys_ref, idx_ref[r], buf_ref.at[s, 0], r, sem_row.at[s]).start()
            _row_copy(ys_ref, idx_ref[tt + r], buf_ref.at[s, 1], r, sem_row.at[s]).start()
            return carry

        lax.fori_loop(0, tt, issue, 0, unroll=ROW_DMA_UNROLL)

    @pl.when(i == 0)
    def _():
        gather(0, 0)

    @pl.when(i + 1 < pl.num_programs(0))
    def _():
        gather(i + 1, 1 - slot)

    for k in range(2):
        pltpu.make_async_copy(ys_ref.at[pl.ds(0, tt)], buf_ref.at[slot, k], sem_row.at[slot]).wait()
    route = route_ref[...]
    g1 = route[:, ROUTE_GATE:ROUTE_GATE + 1]
    g2 = route[:, ROUTE_GATE + 1:ROUTE_GATE + 2]
    y = h_ref[...] + g1 * _from_slabs(buf_ref.at[slot, 0]) + g2 * _from_slabs(buf_ref.at[slot, 1])
    out_ref[...] = _rms(y, fnw_ref[...]) if final else y


def _combine(h, route, fnw, dest_flat, ys, tt, final):
    n, d = h.shape
    return pl.pallas_call(
        functools.partial(_combine_body, final),
        grid=(n // tt,),
        in_specs=[pl.BlockSpec((tt, d), lambda i: (i, 0)), pl.BlockSpec((tt, ROUTE_W), lambda i: (i, 0)),
                  pl.BlockSpec((1, d), lambda i: (0, 0)),
                  pl.BlockSpec(memory_space=pl.ANY), pl.BlockSpec(memory_space=pl.ANY)],
        out_specs=pl.BlockSpec((tt, d), lambda i: (i, 0)),
        out_shape=jax.ShapeDtypeStruct((n, d), f32),
        scratch_shapes=[pltpu.VMEM((2, 2, tt, d // LANES, LANES), f32), pltpu.SMEM((2 * tt,), jnp.int32),
                        pltpu.SemaphoreType.DMA(()), pltpu.SemaphoreType.DMA((2,))],
        compiler_params=_cparams("arbitrary"),
        name="moe_combine",
    )(h, route, fnw, dest_flat, ys)


def _moe(h, nw, router, wg, wu, wd, fnw, final):
    n, d = h.shape
    tt = TILE * _largest_divisor(n // TILE, 4)
    route, counts = _router(h, nw, router)

    cnt = counts[0, :N_EXPERTS].astype(jnp.int32)
    padded = (cnt + MOE_TM - 1) // MOE_TM * MOE_TM
    ends = jnp.cumsum(padded)
    off = ends - padded
    n_tiles = -(-2 * n // MOE_TM) + N_EXPERTS
    tile_expert = jnp.minimum(
        jnp.sum((jnp.arange(n_tiles)[:, None] * MOE_TM >= ends[None, :]).astype(jnp.int32), axis=1),
        N_EXPERTS - 1).astype(jnp.int32)
    n_used = (ends[-1:] // MOE_TM).astype(jnp.int32)
    sel = route[:, ROUTE_E:ROUTE_E + 2].astype(jnp.int32)
    rank = route[:, ROUTE_RANK:ROUTE_RANK + 2].astype(jnp.int32)
    dest = jnp.sum(jnp.where(sel[..., None] == jnp.arange(N_EXPERTS), off, 0), axis=-1) + rank
    dest_flat = dest.reshape(n // tt, tt, 2).transpose(0, 2, 1).reshape(-1)

    fill_rows = jnp.concatenate([off + cnt, n_used]).astype(jnp.int32)
    xs = _dispatch(h, nw, dest_flat, fill_rows, (n_tiles + 1) * MOE_TM, tt)
    ys = _experts(xs, tile_expert, n_used, wg, wu, wd, n_tiles)
    return _combine(h, route, fnw, dest_flat, ys, tt, final)


def _final_norm_body(h_ref, w_ref, out_ref):
    out_ref[...] = _rms(h_ref[...], w_ref[...])


def _final_norm(h, w):
    n, d = h.shape
    tm = TILE * _largest_divisor(n // TILE, 8)
    return pl.pallas_call(
        _final_norm_body,
        grid=(n // tm,),
        in_specs=[pl.BlockSpec((tm, d), lambda i: (i, 0)), pl.BlockSpec((1, d), lambda i: (0, 0))],
        out_specs=pl.BlockSpec((tm, d), lambda i: (i, 0)),
        out_shape=jax.ShapeDtypeStruct((n, d), f32),
        compiler_params=_cparams("parallel"),
        name="final_norm",
    )(h, w)


def _rope_tables(seq, pad):
    t = jnp.arange(seq)
    meta_pos = jnp.arange(N_META) - N_META
    row = jnp.concatenate([meta_pos, t // GRID_W]).astype(f32)
    col = jnp.concatenate([meta_pos, t % GRID_W]).astype(f32)
    half = HEAD_DIM // 2
    inv = ROPE_THETA ** (-jnp.arange(0, half, 2, dtype=f32) / half)
    ang = jnp.concatenate([row[:, None] * inv, col[:, None] * inv], axis=-1)
    ang = jnp.tile(jnp.repeat(ang, 2, axis=-1), (1, 4))
    ang = jnp.pad(ang, ((pad, 0), (0, 0)))
    return jnp.cos(ang), jnp.sin(ang)


def _pair_swap_matrix(width):
    i = jnp.arange(width)
    p = jnp.zeros((width, width), f32)
    p = p.at[i[1::2], i[0::2]].set(-1.0)
    p = p.at[i[0::2], i[1::2]].set(1.0)
    return p.astype(bf16)


def _group_mean_matrix(width, group):
    i = jnp.arange(width)
    return ((i[:, None] // group == i[None, :] // group).astype(f32) / group).astype(bf16)


def _t5_bucket(rel):
    nb = REL_BUCKETS // 2
    max_exact = nb // 2
    ret = (rel > 0).astype(jnp.int32) * nb
    n = jnp.abs(rel)
    nf = jnp.maximum(n, 1).astype(f32)
    large = max_exact + (jnp.log(nf / max_exact) / math.log(REL_MAX_DIST / max_exact)
                         * (nb - max_exact)).astype(jnp.int32)
    large = jnp.minimum(large, nb - 1)
    return ret + jnp.where(n < max_exact, n, large)


def _swa_bias_tables(rel_bias, lp):
    def lookup(bucket):
        out = jnp.zeros((rel_bias.shape[1],) + bucket.shape, f32)
        for b in range(REL_BUCKETS):
            out = jnp.where((bucket == b)[None], rel_bias[b].astype(f32)[:, None, None], out)
        return out

    qi = jnp.arange(TILE)
    ki = jnp.arange(3 * TILE)
    rel = ki[None, :] - TILE - qi[:, None]
    band = jnp.where((jnp.abs(rel) <= WINDOW)[None], lookup(_t5_bucket(rel)), NEG)
    pos = jnp.arange(lp) - (TILE - N_META)
    rel_m = jnp.arange(N_META)[None, :] - pos[:, None]
    meta = lookup(_t5_bucket(rel_m))
    nt = lp // TILE
    band_t = jnp.transpose(band, (2, 0, 1)).reshape(3 * TILE, 4 * TILE) * LOG2E
    meta_t = (meta.reshape(4, nt, TILE, N_META).transpose(1, 3, 0, 2).reshape(nt, N_META, 4 * TILE)
              * LOG2E)
    return band_t, meta_t


def _row(v, width=None):
    v = v.astype(f32).reshape(1, -1)
    if width is not None and v.shape[1] < width:
        v = jnp.pad(v, ((0, 0), (0, width - v.shape[1])))
    return v


def kernel(x, meta_tokens, rel_bias, norm_mix_w, norm_ffn_w, w_in, ssd_conv_w, ssd_conv_b, ssd_dt_bias, ssd_a_log, ssd_d, ssd_norm_w, gla_gate_w2, gla_gate_b, gla_norm_w, swa_sink, gqa_q_norm_w, gqa_k_norm_w, w_out, ffn_w_gate, ffn_w_up, ffn_w_down, moe_router, moe_w_gate, moe_w_up, moe_w_down, final_norm_w):
    bsz, seq, d = x.shape
    depth = w_in.shape[0]
    pad = (-(seq + N_META)) % TILE
    assert pad == TILE - N_META and seq % TILE == 0
    lp = pad + N_META + seq
    n = bsz * lp

    meta = jnp.broadcast_to(meta_tokens[None].astype(x.dtype), (bsz, N_META, d))
    h = jnp.concatenate([jnp.zeros((bsz, pad, d), x.dtype), meta, x], axis=1).reshape(n, d)

    cos, sin = _rope_tables(seq, pad)
    rot = _pair_swap_matrix(256)
    gavg = _group_mean_matrix(256, HEAD_DIM)
    bias_band, bias_meta = _swa_bias_tables(rel_bias, lp)
    offs = [0]
    for s in IN_SIZES:
        offs.append(offs[-1] + s)

    for i in range(depth):
        wi = w_in[i]
        cols = [wi[:, offs[j]:offs[j + 1]] for j in PACK_ORDER]
        w_pack = jnp.concatenate(cols + [jnp.zeros((d, PACK_WIDTH - offs[-1]), wi.dtype)], axis=1).astype(bf16)
        o_ssd, o_small, o_gla, sk, ak, aq, av, sq, sv = _inproj(
            h, _row(norm_mix_w[i]), w_pack, cos, sin, rot, gavg,
            _row(jnp.tile(gqa_q_norm_w[i], 4)), _row(jnp.tile(gqa_k_norm_w[i], 2)), bsz, lp)

        convw = jnp.pad(ssd_conv_w[i].astype(f32), ((0, 8 - SSD_CONV), (0, 0)))
        convb = _row(ssd_conv_b[i])
        dtb = _row(ssd_dt_bias[i].reshape(-1), 128)
        alog = _row(ssd_a_log[i].reshape(-1), 128)
        yf, xbc = _ssd(False, o_ssd, o_small, convw, convb, dtb, alog,
                       _row(jnp.repeat(ssd_d[i], SSD_HEAD_DIM)), bsz, lp)
        y_ssd = _ssd(True, o_ssd, o_small, convw, convb, dtb, alog, (yf, xbc, _row(ssd_norm_w[i])), bsz, lp)

        def gate_w(direction):
            lo = SMALL_GA + GLA_GATE_RANK * direction
            full = jnp.zeros((128, GLA_KEY), f32).at[lo:lo + GLA_GATE_RANK].set(gla_gate_w2[i, direction].astype(f32))
            return full.astype(bf16)

        of = _gla(False, o_gla, o_small, gate_w(0), _row(gla_gate_b[i, 0]), None, bsz, lp)
        y_gla = _gla(True, o_gla, o_small, gate_w(1), _row(gla_gate_b[i, 1]),
                     (of, _row(jnp.tile(gla_norm_w[i], GLA_HEADS)), gavg), bsz, lp)

        sink = _row(jnp.repeat(swa_sink[i].astype(f32), TILE)) * LOG2E
        y_swa = _swa(sq, sk, sv, bias_band, bias_meta, sink, bsz, lp)
        y_g2 = _flash(aq, ak, av, bsz, lp)

        h = _outproj((y_ssd, y_gla, y_swa, y_g2), h, w_out[i].astype(bf16))

        j = i // 2
        if i % 2 == 0:
            h = _ffn(h, _row(norm_ffn_w[i]), ffn_w_gate[j].astype(bf16), ffn_w_up[j].astype(bf16),
                     ffn_w_down[j].astype(bf16))
            if i == depth - 1:
                h = _final_norm(h, _row(final_norm_w))
        else:
            router = jnp.pad(moe_router[j].astype(f32), ((0, 0), (0, 128 - N_EXPERTS)))
            h = _moe(h, _row(norm_ffn_w[i]), router, moe_w_gate[j].astype(bf16), moe_w_up[j].astype(bf16),
                     moe_w_down[j].astype(bf16), _row(final_norm_w), i == depth - 1)
    return h.reshape(bsz, lp, d)[:, pad + N_META:]
```

```python
import functools
import math

import jax
import jax.numpy as jnp
from jax import lax
from jax.experimental import pallas as pl
from jax.experimental.pallas import tpu as pltpu

f32 = jnp.float32
bf16 = jnp.bfloat16

N_META = 16
HEAD_DIM = 64
GRID_W = 64
EPS = 1e-6
ROPE_THETA = 10000.0
TILE = 128
SSD_HEADS = 4
SSD_HEAD_DIM = 64
SSD_INNER = 256
SSD_STATE = 128
SSD_CONV = 5
SSD_CONV_DIM = 768
GLA_HEADS = 4
GLA_DK = 32
GLA_DV = 64
GLA_KEY = 128
GLA_VAL = 256
GLA_GATE_RANK = 16
GLA_GATE_NORM = 16.0
GLA_CHUNK = 64
WINDOW = 128
REL_BUCKETS = 32
REL_MAX_DIST = 128
N_EXPERTS = 8
NEG = -1e30
LOG2E = math.log2(math.e)
QK_WIDTH = 384
CONV_HALO = 8
VMEM_LIMIT = 56 * 1024 * 1024

IN_SIZES = (256, 768, 8, 128, 128, 256, 256, 32, 256, 128, 128, 256, 128, 128)
(_Z, _XBC, _DT, _GQ, _GK, _GV, _GR, _GA, _SQ, _SK, _SV, _AQ, _AK, _AV) = range(14)
PACK_ORDER = (_XBC, _Z, _GQ, _GK, _GV, _GR, _SQ, _SK, _SV, _AQ, _AK, _AV, _DT, _GA)
PACK_WIDTH = 2944
C_SSD, C_GLA, C_SWA, C_AQ, C_AK, C_AV, C_SMALL = 0, 1024, 1792, 2304, 2560, 2688, 2816
SMALL_DT, SMALL_GA = 0, 8


def _cparams(*sem):
    return pltpu.CompilerParams(dimension_semantics=sem, vmem_limit_bytes=VMEM_LIMIT)


def _dot(a, b):
    return jnp.dot(a, b, preferred_element_type=f32)


def _dot_nt(a, b):
    return lax.dot_general(a, b, (((1,), (1,)), ((), ())), preferred_element_type=f32)


def _dot_tn(a, b):
    return lax.dot_general(a, b, (((0,), (0,)), ((), ())), preferred_element_type=f32)


def _split(a):
    hi = a.astype(bf16)
    lo = (a - hi.astype(f32)).astype(bf16)
    return hi, lo


def _dot_split_lhs(a, b):
    hi, lo = _split(a)
    return _dot(hi, b) + _dot(lo, b)


def _dot_split_rhs(t, x):
    hi, lo = _split(x)
    return _dot(t, hi) + _dot(t, lo)


def _rms(x, w):
    return x * lax.rsqrt(jnp.mean(x * x, axis=-1, keepdims=True) + EPS) * w


def _silu(x):
    return x / (1.0 + jnp.exp(-x))


def _softplus(x):
    return jnp.maximum(x, 0.0) + jnp.log(1.0 + jnp.exp(-jnp.abs(x)))


def _log_sigmoid(x):
    return jnp.minimum(x, 0.0) - jnp.log(1.0 + jnp.exp(-jnp.abs(x)))


def _tri(n, rev):
    r = lax.broadcasted_iota(jnp.int32, (n, n), 0)
    c = lax.broadcasted_iota(jnp.int32, (n, n), 1)
    return (r <= c) if rev else (r >= c)


def _valid_rows(tile, n):
    rows = lax.broadcasted_iota(jnp.int32, (n, 1), 0)
    return jnp.logical_or(tile > 0, rows >= TILE - N_META)


def _inproj_body(h_ref, nw_ref, w_ref, cos_ref, sin_ref, rot_ref, gavg_ref, qkw_ref,
                 ssd_ref, small_ref, gla_ref, sk_ref, k_ref, q_ref, v_ref, sq_ref, sv_ref):
    u = _rms(h_ref[...], nw_ref[...]).astype(bf16)

    def mm(lo, hi):
        return _dot(u, w_ref[:, lo:hi])

    ssd_ref[...] = mm(C_SSD, C_GLA)
    gla_ref[...] = mm(C_GLA, C_SWA)
    small_ref[...] = mm(C_SMALL, PACK_WIDTH)
    akv = mm(C_AK, C_SMALL)
    v_ref[...] = akv[:, 128:].T.astype(bf16)

    skv = mm(C_SWA + 256, C_AQ)
    sk_ref[...] = skv[:, :128].astype(bf16)
    sv_ref[...] = skv[:, 128:].T.astype(bf16)
    sqt = (mm(C_SWA, C_SWA + 256) * (HEAD_DIM ** -0.5 * LOG2E)).T.astype(bf16)
    sq_ref[...] = jnp.zeros_like(sq_ref)
    for t in range(sq_ref.shape[0]):
        for hh in range(4):
            lo = HEAD_DIM * (hh // 2)
            sq_ref[t, lo:lo + HEAD_DIM, TILE * hh:TILE * (hh + 1)] = (
                sqt[HEAD_DIM * hh:HEAD_DIM * (hh + 1), TILE * t:TILE * (t + 1)])

    t = jnp.concatenate([mm(C_AQ, C_AK), akv[:, :128]], axis=1)
    ms = _dot_split_lhs(t * t, gavg_ref[...])
    tn = t * lax.rsqrt(ms + EPS) * qkw_ref[...]
    tr = _dot(tn.astype(bf16), rot_ref[...])
    qk = tn * cos_ref[...] + tr * sin_ref[...]
    q_ref[...] = (qk[:, :256] * (HEAD_DIM ** -0.5 * LOG2E)).T.astype(bf16)
    k_ref[...] = qk[:, 256:].astype(bf16)


def _inproj(h, nw, w, cos, sin, rot, gavg, qkw, bsz, lp):
    n = h.shape[0]
    d = h.shape[1]
    tm = TILE * _largest_divisor(lp // TILE, 3)
    per = lp // tm
    row = lambda b, i: (b * per + i, 0)
    const = lambda b, i: (0, 0)
    tab = lambda b, i: (i, 0)
    outs = ((1024, f32), (128, f32), (768, f32), (128, bf16), (128, bf16))
    slab = lambda b, i: (b * per + i, 0, 0)
    sub = tm // TILE
    return pl.pallas_call(
        _inproj_body,
        grid=(bsz, per),
        in_specs=[pl.BlockSpec((tm, d), row), pl.BlockSpec((1, d), const),
                  pl.BlockSpec((d, PACK_WIDTH), const),
                  pl.BlockSpec((tm, QK_WIDTH), tab), pl.BlockSpec((tm, QK_WIDTH), tab),
                  pl.BlockSpec((QK_WIDTH, QK_WIDTH), const), pl.BlockSpec((QK_WIDTH, QK_WIDTH), const),
                  pl.BlockSpec((1, QK_WIDTH), const)],
        out_specs=([pl.BlockSpec((tm, c), row) for c, _ in outs]
                   + [pl.BlockSpec((None, 256, tm), slab), pl.BlockSpec((None, 128, tm), slab),
                      pl.BlockSpec((sub, 128, 4 * TILE), slab), pl.BlockSpec((None, 128, tm), slab)]),
        out_shape=([jax.ShapeDtypeStruct((n, c), t) for c, t in outs]
                   + [jax.ShapeDtypeStruct((n // tm, 256, tm), bf16),
                      jax.ShapeDtypeStruct((n // tm, 128, tm), bf16),
                      jax.ShapeDtypeStruct((n // TILE, 128, 4 * TILE), bf16),
                      jax.ShapeDtypeStruct((n // tm, 128, tm), bf16)]),
        compiler_params=_cparams("parallel", "parallel"),
        name="inproj",
    )(h, nw, w, cos, sin, rot, gavg, qkw)


def _largest_divisor(n, cap):
    return max(k for k in range(1, cap + 1) if n % k == 0)


def _for_each_batch(chain, batched, n_scratch):
    def body(*refs):
        nb = next(r.shape[0] for r, flag in zip(refs, batched) if flag)
        io, scratch = refs[:len(batched)], refs[len(batched):]
        assert len(scratch) == n_scratch * nb
        for b in range(nb):
            chain(*[r.at[b] if flag else r for r, flag in zip(io, batched)],
                  *[scratch[k * nb + b] for k in range(n_scratch)])
    return body


def _per_batch_scratch(nb, *shapes):
    return [pltpu.VMEM(shape, dtype) for shape, dtype in shapes for _ in range(nb)]


def _ssd_scan_tile(rev, xs, bmat, cmat, dt, cum, tot, st):
    col0 = SMALL_DT + (SSD_HEADS if rev else 0)
    causal = _tri(TILE, rev)
    cum_t = cum.T
    e_cum = jnp.exp(cum)
    e_tot = jnp.exp(tot)
    head_of = lax.broadcasted_iota(jnp.int32, (TILE, SSD_INNER), 1) // SSD_HEAD_DIM
    head_of_row = lax.broadcasted_iota(jnp.int32, (1, SSD_INNER), 1) // SSD_HEAD_DIM

    def widen(cols, like):
        out = jnp.zeros(like.shape, f32)
        for hh in range(SSD_HEADS):
            out = jnp.where(like == hh, cols[:, col0 + hh:col0 + hh + 1], out)
        return out

    def own_head(r):
        out = jnp.zeros((TILE, SSD_INNER), f32)
        for hh in range(SSD_HEADS):
            out = jnp.where(head_of == hh, r[TILE * hh:TILE * (hh + 1)], out)
        return out

    xd = (xs * widen(dt, head_of)).astype(bf16)
    scores, bws = [], []
    for g in range(2):
        bg = bmat[:, SSD_STATE * g:SSD_STATE * (g + 1)]
        cbg = _dot_nt(cmat[:, SSD_STATE * g:SSD_STATE * (g + 1)], bg)
        for hh in (2 * g, 2 * g + 1):
            col = col0 + hh
            a_col = cum[:, col:col + 1]
            decay = jnp.where(causal, jnp.exp(a_col - cum_t[col:col + 1, :]), 0.0)
            scores.append((cbg * decay).astype(bf16))
            bws.append((bg.astype(f32) * jnp.exp(tot[:, col:col + 1] - a_col)).astype(bf16))
    y = own_head(_dot(jnp.concatenate(scores, axis=0), xd))
    y = y + _dot(cmat, st.astype(bf16)) * widen(e_cum, head_of)
    upd = _dot_tn(jnp.concatenate(bws, axis=1), xd)
    rows = [jnp.where(head_of == 2 * g, upd[SSD_STATE * 2 * g:SSD_STATE * (2 * g + 1)],
                      jnp.where(head_of == 2 * g + 1,
                                upd[SSD_STATE * (2 * g + 1):SSD_STATE * (2 * g + 2)], 0.0))
            for g in range(2)]
    return y, st * widen(e_tot, head_of_row) + jnp.concatenate(rows, axis=0)


def _ssd_body(rev, nt, *refs):
    if rev:
        xbc_ref, small_ref, dtb_ref, alog_ref, yf_ref, z_ref, normw_ref, out_ref, state_ref = refs
    else:
        (cur_ref, prev_ref, next_ref, small_ref, convw_ref, convb_ref, dtb_ref, alog_ref, dskip_ref,
         out_ref, xbc_out_ref, state_ref) = refs[:12]
        ext_refs = refs[12:]
    c = pl.program_id(0)
    tile = (nt - 1 - c) if rev else c
    nb = small_ref.shape[0]
    valid = _valid_rows(tile, TILE)

    a = -jnp.exp(alog_ref[...])
    dts = [jnp.where(valid, _softplus(small_ref[b] + dtb_ref[...]), 0.0) for b in range(nb)]
    cum_all = _dot_split_rhs(_tri(TILE, rev).astype(bf16),
                             jnp.concatenate([dt * a for dt in dts], axis=1))
    for b in range(nb):
        dt = dts[b]
        if rev:
            xbc = xbc_ref[b]
            xs = xbc[:, :SSD_INNER].astype(f32)
            bc = xbc[:, SSD_INNER:]
        else:
            ext_ref = ext_refs[b]
            ext_ref[0:CONV_HALO, :] = jnp.where(tile > 0, prev_ref[b], 0.0)
            ext_ref[CONV_HALO:CONV_HALO + TILE, :] = jnp.where(valid, cur_ref[b], 0.0)
            ext_ref[CONV_HALO + TILE:, :] = jnp.where(tile < nt - 1, next_ref[b], 0.0)
            acc = jnp.zeros((TILE, SSD_CONV_DIM), f32) + convb_ref[...]
            first = CONV_HALO - (SSD_CONV - 1) // 2
            for k in range(SSD_CONV):
                acc = acc + convw_ref[k:k + 1, :] * ext_ref[first + k:first + k + TILE, :]
            xbc = jnp.where(valid, _silu(acc), 0.0)
            xbc_out_ref[b] = xbc.astype(bf16)
            xs = xbc[:, :SSD_INNER]
            bc = xbc[:, SSD_INNER:].astype(bf16)
        st = jnp.where(c == 0, 0.0, state_ref[b])
        y, st = _ssd_scan_tile(rev, xs, bc[:, :2 * SSD_STATE], bc[:, 2 * SSD_STATE:], dt,
                               cum_all[:, 128 * b:128 * (b + 1)],
                               jnp.sum(dt * a, axis=0, keepdims=True), st)
        state_ref[b] = st
        if rev:
            y = (yf_ref[b] + y) * _silu(z_ref[b])
            out_ref[b] = jnp.where(valid, _rms(y, normw_ref[...]), 0.0).astype(out_ref.dtype)
        else:
            out_ref[b] = y + dskip_ref[...] * xs


def _ssd(rev, o_ssd, o_small, convw, convb, dtb, alog, extra, bsz, lp):
    n = o_ssd.shape[0]
    nt = lp // TILE
    hb = TILE // CONV_HALO
    o_ssd = o_ssd.reshape(bsz, lp, -1)
    o_small = o_small.reshape(bsz, lp, -1)

    def tile_of(c):
        return (nt - 1 - c) if rev else c

    cur = lambda c: (0, tile_of(c), 0)
    prev = lambda c: (0, jnp.maximum(tile_of(c) * hb - 1, 0), 0)
    nxt = lambda c: (0, jnp.minimum((tile_of(c) + 1) * hb, lp // CONV_HALO - 1), 0)
    zcol = lambda c: (0, tile_of(c), SSD_CONV_DIM // SSD_INNER)
    const = lambda c: (0, 0)
    tile3 = lambda width: pl.BlockSpec((bsz, TILE, width), cur)
    state = pltpu.VMEM((bsz, 2 * SSD_STATE, SSD_INNER), f32)
    if rev:
        yf, xbc, normw = extra
        out = pl.pallas_call(
            functools.partial(_ssd_body, rev, nt),
            grid=(nt,),
            in_specs=[tile3(SSD_CONV_DIM), tile3(128), pl.BlockSpec((1, 128), const),
                      pl.BlockSpec((1, 128), const), tile3(SSD_INNER),
                      pl.BlockSpec((bsz, TILE, SSD_INNER), zcol), pl.BlockSpec((1, SSD_INNER), const)],
            out_specs=tile3(SSD_INNER),
            out_shape=jax.ShapeDtypeStruct((bsz, lp, SSD_INNER), bf16),
            scratch_shapes=[state],
            compiler_params=_cparams("arbitrary"),
            name="ssd_rev",
        )(xbc, o_small, dtb, alog, yf, o_ssd, normw)
        return out.reshape(n, SSD_INNER)
    halo = lambda im: pl.BlockSpec((bsz, CONV_HALO, SSD_CONV_DIM), im)
    return pl.pallas_call(
        functools.partial(_ssd_body, rev, nt),
        grid=(nt,),
        in_specs=[tile3(SSD_CONV_DIM), halo(prev), halo(nxt), tile3(128),
                  pl.BlockSpec((8, SSD_CONV_DIM), const), pl.BlockSpec((1, SSD_CONV_DIM), const),
                  pl.BlockSpec((1, 128), const), pl.BlockSpec((1, 128), const),
                  pl.BlockSpec((1, SSD_INNER), const)],
        out_specs=[tile3(SSD_INNER), tile3(SSD_CONV_DIM)],
        out_shape=[jax.ShapeDtypeStruct((bsz, lp, SSD_INNER), f32),
                   jax.ShapeDtypeStruct((bsz, lp, SSD_CONV_DIM), bf16)],
        scratch_shapes=[state] + _per_batch_scratch(bsz, ((TILE + 2 * CONV_HALO, SSD_CONV_DIM), f32)),
        compiler_params=_cparams("arbitrary"),
        name="ssd_fwd",
    )(o_ssd, o_ssd, o_ssd, o_small, convw, convb, dtb, alog, extra)


def _gla_body(rev, nt, *refs):
    if rev:
        x_ref, small_ref, wg_ref, gb_ref, of_ref, normw_ref, gavg_ref, out_ref, st_ref = refs
    else:
        x_ref, small_ref, wg_ref, gb_ref, out_ref, st_ref = refs
    c = pl.program_id(0)
    tile = (nt - 1 - c) if rev else c
    nb = x_ref.shape[0]
    n = GLA_CHUNK
    valid = _valid_rows(tile, TILE)

    pre = _dot(small_ref[...].reshape(nb * TILE, 128).astype(bf16), wg_ref[...]) + gb_ref[...]
    g_all = _log_sigmoid(pre) / GLA_GATE_NORM
    gs = [jnp.where(valid, g_all[TILE * b:TILE * (b + 1)], 0.0) for b in range(nb)]
    row = lax.broadcasted_iota(jnp.int32, (TILE, TILE), 0)
    col = lax.broadcasted_iota(jnp.int32, (TILE, TILE), 1)
    same_chunk = (row // n) == (col // n)
    cum_mat = jnp.where(jnp.logical_and(same_chunk, (row <= col) if rev else (row >= col)), 1.0, 0.0)
    bc_all = _dot_split_rhs(cum_mat.astype(bf16), jnp.concatenate(gs, axis=1))

    first = slice(n, 2 * n) if rev else slice(0, n)
    second = slice(0, n) if rev else slice(n, 2 * n)
    rows = lax.broadcasted_iota(jnp.int32, (TILE, 1), 0)
    in_first = (rows >= n) if rev else (rows < n)
    tri = _tri(n, rev)
    tri4 = jnp.concatenate([tri] * GLA_HEADS, axis=0)
    lane_head = lax.broadcasted_iota(jnp.int32, (n, GLA_KEY), 1) // GLA_DK
    out_head = lax.broadcasted_iota(jnp.int32, (n, GLA_VAL), 1) // GLA_DV
    blockdiag = (lax.broadcasted_iota(jnp.int32, (GLA_VAL, GLA_KEY), 0) // GLA_DV
                 == lax.broadcasted_iota(jnp.int32, (GLA_VAL, GLA_KEY), 1) // GLA_DK)

    def stack_heads(a):
        return jnp.concatenate([jnp.where(lane_head == hh, a, 0.0) for hh in range(GLA_HEADS)],
                               axis=0).astype(bf16)

    def own_head(r):
        out = jnp.zeros((n, GLA_VAL), f32)
        for hh in range(GLA_HEADS):
            out = jnp.where(out_head == hh, r[n * hh:n * (hh + 1)], out)
        return out

    both = []
    for b in range(nb):
        x = x_ref[b]
        g = gs[b]
        bc = bc_all[:, 128 * b:128 * (b + 1)]
        q = jnp.where(valid, x[:, :GLA_KEY], 0.0) * GLA_DK ** -0.5
        k = jnp.where(valid, x[:, GLA_KEY:2 * GLA_KEY], 0.0)
        v = jnp.where(valid, x[:, 2 * GLA_KEY:2 * GLA_KEY + GLA_VAL], 0.0).astype(bf16)
        bl_first = jnp.sum(g[first], axis=0, keepdims=True)
        bl_second = jnp.sum(g[second], axis=0, keepdims=True)
        qt = q * jnp.exp(bc)
        kt = k * jnp.exp(-bc)
        kw = k * jnp.exp(jnp.where(in_first, bl_first, bl_second) - bc)

        att_f = jnp.where(tri4, _dot_nt(stack_heads(qt[first]), kt[first].astype(bf16)), 0.0)
        o_f = own_head(_dot(att_f.astype(bf16), v[first]))
        keys = jnp.concatenate([kw[first], kt[second]], axis=0).astype(bf16)
        vals = jnp.concatenate([v[first], v[second]], axis=0)
        att_s = _dot_nt(stack_heads(qt[second]), keys)
        att_s = jnp.concatenate([att_s[:, :n], jnp.where(tri4, att_s[:, n:], 0.0)], axis=1)
        o_s = own_head(_dot(att_s.astype(bf16), vals))

        st = jnp.where(c == 0, 0.0, st_ref[b])
        q_in = qt * jnp.exp(jnp.where(in_first, 0.0, bl_first))
        o = _dot_nt(q_in.astype(bf16), st.astype(bf16)) + jnp.concatenate(
            [o_s, o_f] if rev else [o_f, o_s], axis=0)
        k_out = (kw * jnp.exp(jnp.where(in_first, bl_second, 0.0))).astype(bf16)
        st_ref[b] = st * jnp.exp(bl_first + bl_second) + jnp.where(blockdiag, _dot_tn(v, k_out), 0.0)

        if rev:
            both.append(of_ref[b] + o)
        else:
            out_ref[b] = o

    if rev:
        o = jnp.concatenate(both, axis=0)
        ms = _dot_split_lhs(o * o, gavg_ref[...])
        on = o * lax.rsqrt(ms + EPS) * normw_ref[...]
        for b in range(nb):
            r = x_ref[b, :, 2 * GLA_KEY + GLA_VAL:]
            out_ref[b] = jnp.where(valid, on[TILE * b:TILE * (b + 1)] * _silu(r), 0.0).astype(out_ref.dtype)


def _gla(rev, o_gla, o_small, wg, gb, extra, bsz, lp):
    n = o_gla.shape[0]
    nt = lp // TILE
    cur = lambda c: (0, (nt - 1 - c) if rev else c, 0)
    const = lambda c: (0, 0)
    in_specs = [pl.BlockSpec((bsz, TILE, 768), cur), pl.BlockSpec((bsz, TILE, 128), cur),
                pl.BlockSpec((128, GLA_KEY), const), pl.BlockSpec((1, GLA_KEY), const)]
    args = [o_gla.reshape(bsz, lp, -1), o_small.reshape(bsz, lp, -1), wg, gb]
    if rev:
        of, normw, gavg = extra
        in_specs += [pl.BlockSpec((bsz, TILE, GLA_VAL), cur), pl.BlockSpec((1, GLA_VAL), const),
                     pl.BlockSpec((GLA_VAL, GLA_VAL), const)]
        args += [of.reshape(bsz, lp, -1), normw, gavg]
    out = pl.pallas_call(
        functools.partial(_gla_body, rev, nt),
        grid=(nt,),
        in_specs=in_specs,
        out_specs=pl.BlockSpec((bsz, TILE, GLA_VAL), cur),
        out_shape=jax.ShapeDtypeStruct((bsz, lp, GLA_VAL), bf16 if rev else f32),
        scratch_shapes=[pltpu.VMEM((bsz, GLA_VAL, GLA_KEY), f32)],
        compiler_params=_cparams("arbitrary"),
        name="gla_rev" if rev else "gla_fwd",
    )(*args)
    return out.reshape(n, GLA_VAL)


def _swa_body(nt, qp_ref, kp_ref, kc_ref, kn_ref, km_ref, vp_ref, vc_ref, vn_ref, vm_ref,
              bias_ref, bmeta_ref, sink_ref, out_ref):
    c = pl.program_id(0)
    krow = lax.broadcasted_iota(jnp.int32, (3 * TILE, 1), 0)
    pen = jnp.where(krow < TILE, jnp.where(c >= 2, 0.0, NEG),
                    jnp.where(krow < 2 * TILE, jnp.where(c >= 1, 0.0, NEG),
                              jnp.where(c <= nt - 2, 0.0, NEG)))
    qp = qp_ref[...]
    kcat = jnp.concatenate([kp_ref[...], kc_ref[...], kn_ref[...]], axis=0)
    s = _dot(kcat, qp) + bias_ref[...] + pen
    sm = _dot(km_ref[...], qp) + bmeta_ref[...]
    sk = sink_ref[...]
    m = jnp.maximum(jnp.maximum(jnp.max(s, axis=0, keepdims=True),
                                jnp.max(sm, axis=0, keepdims=True)), sk)
    p = jnp.exp2(s - m)
    pm = jnp.exp2(sm - m)
    inv = 1.0 / (jnp.sum(p, axis=0, keepdims=True) + jnp.sum(pm, axis=0, keepdims=True)
                 + jnp.exp2(sk - m))
    pb = p.astype(bf16)
    pmb = jnp.concatenate([jnp.zeros((TILE - N_META, 4 * TILE), bf16), pm.astype(bf16)], axis=0)
    vcat = jnp.concatenate([vp_ref[...], vc_ref[...], vn_ref[...]], axis=1)
    heads = []
    for g in range(2):
        rows = slice(HEAD_DIM * g, HEAD_DIM * (g + 1))
        cols = slice(2 * TILE * g, 2 * TILE * (g + 1))
        pv = (_dot(vcat[rows, :], pb[:, cols]) + _dot(vm_ref[rows, :], pmb[:, cols])) * inv[:, cols]
        heads += [pv[:, :TILE], pv[:, TILE:]]
    o = jnp.concatenate(heads, axis=0).T
    out_ref[...] = jnp.where(_valid_rows(c, TILE), o, 0.0).astype(out_ref.dtype)


def _swa(qp, k, vt, bias_band, bias_meta, sink, bsz, lp):
    n = k.shape[0]
    nt = lp // TILE
    per, _, tm = vt.shape[0] // bsz, vt.shape[1], vt.shape[2]
    sub = tm // TILE
    qp = qp.reshape(bsz, nt, 128, 4 * TILE)
    k = k.reshape(bsz, lp, 128)
    vt = vt.reshape(bsz, per, 128, tm)
    prev = lambda c: jnp.maximum(c - 1, 0)
    nxt = lambda c: jnp.minimum(c + 1, nt - 1)
    kspec = lambda tile: pl.BlockSpec((bsz, TILE, 128), lambda c: (0, tile(c), 0))
    vspec = lambda tile: pl.BlockSpec((bsz, None, 128, TILE), lambda c: (0, tile(c) // sub, 0, tile(c) % sub))
    same = lambda c: c
    first = lambda c: 0
    out = pl.pallas_call(
        _for_each_batch(functools.partial(_swa_body, nt), [True] * 9 + [False] * 3 + [True], 0),
        grid=(nt,),
        in_specs=[pl.BlockSpec((bsz, None, 128, 4 * TILE), lambda c: (0, c, 0, 0)),
                  kspec(prev), kspec(same), kspec(nxt),
                  pl.BlockSpec((bsz, N_META, 128), lambda c: (0, TILE // N_META - 1, 0)),
                  vspec(prev), vspec(same), vspec(nxt), vspec(first),
                  pl.BlockSpec((3 * TILE, 4 * TILE), lambda c: (0, 0)),
                  pl.BlockSpec((None, N_META, 4 * TILE), lambda c: (c, 0, 0)),
                  pl.BlockSpec((1, 4 * TILE), lambda c: (0, 0))],
        out_specs=pl.BlockSpec((bsz, TILE, 256), lambda c: (0, c, 0)),
        out_shape=jax.ShapeDtypeStruct((bsz, lp, 256), bf16),
        compiler_params=_cparams("arbitrary"),
        name="swa",
    )(qp, k, k, k, k, vt, vt, vt, vt, bias_band, bias_meta, sink)
    return out.reshape(n, 256)


def _flash_body(nk, tq, tk, qt_ref, k_ref, vt_ref, out_ref, qpad_ref, m_ref, l_ref, acc_ref,
                sa_ref, sb_ref, ca_ref, cb_ref):
    i = pl.program_id(1)
    buf_a, buf_b = (sa_ref, ca_ref), (sb_ref, cb_ref)
    krow = lax.broadcasted_iota(jnp.int32, (tk, 1), 0)
    qpad_ref[...] = jnp.zeros_like(qpad_ref)
    for hh in range(4):
        lo = HEAD_DIM * (hh // 2)
        qpad_ref[lo:lo + HEAD_DIM, tq * hh:tq * (hh + 1)] = qt_ref[HEAD_DIM * hh:HEAD_DIM * (hh + 1), :]
    m_ref[...] = jnp.full_like(m_ref, NEG)
    l_ref[...] = jnp.zeros_like(l_ref)
    acc_ref[...] = jnp.zeros_like(acc_ref)

    def scores(j, buf, first=False):
        s_ref, cmax_ref = buf
        s = _dot(k_ref[pl.ds(pl.multiple_of(j * tk, tk), tk), :], qpad_ref[...])
        if first:
            s = jnp.where(krow >= TILE - N_META, s, NEG)
        s_ref[...] = s
        cmax_ref[...] = jnp.max(s, axis=0, keepdims=True)

    def absorb(j, buf):
        s_ref, cmax_ref = buf
        s = s_ref[...]
        m_old = m_ref[...]
        m_new = jnp.maximum(m_old, cmax_ref[...])
        alpha = jnp.exp2(m_old - m_new)
        p = jnp.exp2(s - m_new)
        l_ref[...] = alpha * l_ref[...] + jnp.sum(p, axis=0, keepdims=True)
        m_ref[...] = m_new
        pb = p.astype(bf16)
        for g in range(2):
            pv = _dot(vt_ref[j, HEAD_DIM * g:HEAD_DIM * (g + 1), :], pb[:, 2 * g * tq:(2 * g + 2) * tq])
            for r in range(2):
                hh = 2 * g + r
                rows = slice(HEAD_DIM * hh, HEAD_DIM * (hh + 1))
                acc_ref[rows, :] = (alpha[:, tq * hh:tq * (hh + 1)] * acc_ref[rows, :]
                                    + pv[:, tq * r:tq * (r + 1)])

    scores(0, buf_a, first=True)

    def body(jj, carry):
        j = 2 * jj
        scores(j + 1, buf_b)
        absorb(j, buf_a)
        scores(j + 2, buf_a)
        absorb(j + 1, buf_b)
        return carry

    lax.fori_loop(0, (nk - 1) // 2, body, 0)
    if nk % 2 == 1:
        absorb(nk - 1, buf_a)
    else:
        scores(nk - 1, buf_b)
        absorb(nk - 2, buf_a)
        absorb(nk - 1, buf_b)
    linv = 1.0 / l_ref[...]
    for hh in range(4):
        rows = slice(HEAD_DIM * hh, HEAD_DIM * (hh + 1))
        acc_ref[rows, :] = acc_ref[rows, :] * linv[:, tq * hh:tq * (hh + 1)]
    rows = lax.broadcasted_iota(jnp.int32, (tq, 1), 0)
    valid = jnp.logical_or(i > 0, rows >= TILE - N_META)
    out_ref[...] = jnp.where(valid, acc_ref[...].T, 0.0).astype(out_ref.dtype)


def _flash(qt, k, vt, bsz, lp):
    n = k.shape[0]
    tq = qt.shape[2]
    per = lp // tq
    return pl.pallas_call(
        functools.partial(_flash_body, per, tq, tq),
        grid=(bsz, per),
        in_specs=[pl.BlockSpec((None, 256, tq), lambda b, i: (b * per + i, 0, 0)),
                  pl.BlockSpec((lp, 128), lambda b, i: (b, 0)),
                  pl.BlockSpec((per, 128, tq), lambda b, i: (b, 0, 0))],
        out_specs=pl.BlockSpec((tq, 256), lambda b, i: (b * per + i, 0)),
        out_shape=jax.ShapeDtypeStruct((n, 256), bf16),
        scratch_shapes=[pltpu.VMEM((128, 4 * tq), bf16), pltpu.VMEM((1, 4 * tq), f32),
                        pltpu.VMEM((1, 4 * tq), f32), pltpu.VMEM((256, tq), f32),
                        pltpu.VMEM((tq, 4 * tq), f32), pltpu.VMEM((tq, 4 * tq), f32),
                        pltpu.VMEM((1, 4 * tq), f32), pltpu.VMEM((1, 4 * tq), f32)],
        compiler_params=_cparams("parallel", "parallel"),
        name="gqa_full",
    )(qt, k, vt)


def _mix_residual(y_refs, h_ref, wo_ref):
    mixed = jnp.concatenate([y_ref[...] for y_ref in y_refs], axis=1)
    return h_ref[...] + _dot(mixed, wo_ref[...])


def _ffn_body(y0_ref, y1_ref, y2_ref, y3_ref, h_ref, wo_ref, nw_ref, wg_ref, wu_ref, wd_ref,
              out_ref, hn_ref, u_ref, acc_ref):
    j = pl.program_id(1)

    @pl.when(j == 0)
    def _():
        hn = _mix_residual((y0_ref, y1_ref, y2_ref, y3_ref), h_ref, wo_ref)
        hn_ref[...] = hn
        u_ref[...] = _rms(hn, nw_ref[...]).astype(bf16)
        acc_ref[...] = jnp.zeros_like(acc_ref)

    u = u_ref[...]
    t = _silu(_dot(u, wg_ref[...])) * _dot(u, wu_ref[...])
    acc_ref[...] += _dot(t.astype(bf16), wd_ref[...])

    @pl.when(j == pl.num_programs(1) - 1)
    def _():
        out_ref[...] = hn_ref[...] + acc_ref[...]


def _ffn(ys, h, wo, nw, wg, wu, wd):
    n, d = h.shape
    ff = wg.shape[1]
    tm = TILE * _largest_divisor(n // TILE, 4)
    tf = 128 * _largest_divisor(ff // 128, 11)
    row = lambda i, j: (i, 0)
    const = lambda i, j: (0, 0)
    return pl.pallas_call(
        _ffn_body,
        grid=(n // tm, ff // tf),
        in_specs=[pl.BlockSpec((tm, 256), row)] * 4
        + [pl.BlockSpec((tm, d), row), pl.BlockSpec((d, d), const), pl.BlockSpec((1, d), const),
           pl.BlockSpec((d, tf), lambda i, j: (0, j)), pl.BlockSpec((d, tf), lambda i, j: (0, j)),
           pl.BlockSpec((tf, d), lambda i, j: (j, 0))],
        out_specs=pl.BlockSpec((tm, d), row),
        out_shape=jax.ShapeDtypeStruct((n, d), f32),
        scratch_shapes=[pltpu.VMEM((tm, d), f32), pltpu.VMEM((tm, d), bf16), pltpu.VMEM((tm, d), f32)],
        compiler_params=_cparams("parallel", "arbitrary"),
        name="ffn",
    )(*ys, h, wo, nw, wg, wu, wd)


MOE_TM = 1024
ROUTE_E, ROUTE_RANK, ROUTE_GATE, ROUTE_W = 0, 2, 4, 8


def _router_body(y0_ref, y1_ref, y2_ref, y3_ref, h_ref, wo_ref, nw_ref, r_ref,
                 hn_ref, route_ref, cnt_ref, base_ref):
    @pl.when(pl.program_id(0) == 0)
    def _():
        base_ref[...] = jnp.zeros_like(base_ref)

    hn = _mix_residual((y0_ref, y1_ref, y2_ref, y3_ref), h_ref, wo_ref)
    hn_ref[...] = hn
    u = _rms(hn, nw_ref[...])
    u_hi, u_lo = _split(u)
    r_hi, r_lo = _split(r_ref[...])
    logits = _dot(u_hi, r_hi) + _dot(u_lo, r_hi) + _dot(u_hi, r_lo)
    tm = logits.shape[0]
    lane = lax.broadcasted_iota(jnp.int32, logits.shape, 1)
    logits = jnp.where(lane < N_EXPERTS, logits, NEG)
    m1 = jnp.max(logits, axis=-1, keepdims=True)
    i1 = jnp.min(jnp.where(logits == m1, lane, 128), axis=-1, keepdims=True)
    rest = jnp.where(lane == i1, NEG, logits)
    m2 = jnp.max(rest, axis=-1, keepdims=True)
    i2 = jnp.min(jnp.where(rest == m2, lane, 128), axis=-1, keepdims=True)
    e2 = jnp.exp(m2 - m1)
    g1 = 1.0 / (1.0 + e2)
    g2 = e2 * g1

    sel1 = lane == i1
    sel2 = lane == i2
    onehot = jnp.where(sel1, 1.0, jnp.where(sel2, 1.0, 0.0))
    strict = jnp.where(lax.broadcasted_iota(jnp.int32, (tm, tm), 0)
                       > lax.broadcasted_iota(jnp.int32, (tm, tm), 1), 1.0, 0.0).astype(bf16)
    before = _dot(strict, onehot.astype(bf16)) + base_ref[0:1, :]
    r1 = jnp.sum(jnp.where(sel1, before, 0.0), axis=-1, keepdims=True)
    r2 = jnp.sum(jnp.where(sel2, before, 0.0), axis=-1, keepdims=True)
    route = jnp.zeros(logits.shape, f32)
    for k, val in enumerate((i1.astype(f32), i2.astype(f32), r1, r2, g1, g2)):
        route = jnp.where(lane == k, val, route)
    route_ref[...] = route[:, :ROUTE_W]
    base_ref[0:1, :] = base_ref[0:1, :] + jnp.sum(onehot, axis=0, keepdims=True)
    cnt_ref[...] = base_ref[...]


def _router(ys, h, wo, nw, router):
    n, d = h.shape
    tm = TILE * _largest_divisor(n // TILE, 8)
    row = lambda i: (i, 0)
    const = lambda i: (0, 0)
    return pl.pallas_call(
        _router_body,
        grid=(n // tm,),
        in_specs=[pl.BlockSpec((tm, 256), row)] * 4
        + [pl.BlockSpec((tm, d), row), pl.BlockSpec((d, d), const), pl.BlockSpec((1, d), const),
           pl.BlockSpec((d, 128), const)],
        out_specs=[pl.BlockSpec((tm, d), row), pl.BlockSpec((tm, ROUTE_W), row), pl.BlockSpec((8, 128), const)],
        out_shape=[jax.ShapeDtypeStruct((n, d), f32), jax.ShapeDtypeStruct((n, ROUTE_W), f32),
                   jax.ShapeDtypeStruct((8, 128), f32)],
        scratch_shapes=[pltpu.VMEM((8, 128), f32)],
        compiler_params=_cparams("arbitrary"),
        name="moe_router",
    )(*ys, h, wo, nw, router)


LANES = 128
ROW_DMA_UNROLL = 8


def _row_copy(src_ref, src_row, dst_ref, dst_row, sem):
    return pltpu.make_async_copy(src_ref.at[src_row], dst_ref.at[dst_row], sem)


def _to_slabs(dst_ref, val):
    dst_ref[...] = val.reshape(dst_ref.shape)


def _from_slabs(src_ref):
    rows, chunks, lanes = src_ref.shape
    return src_ref[...].reshape(rows, chunks * lanes)


def _dispatch_body(fill_ref, h_ref, nw_ref, dest_ref, xs_ref, u_ref, idx_ref, zero_ref,
                   sem_idx, sem_row, sem_fill):
    i = pl.program_id(0)
    tt = u_ref.shape[1]

    @pl.when(i == 0)
    def _():
        zero_ref[...] = jnp.zeros_like(zero_ref)
        fills = [pltpu.make_async_copy(zero_ref, xs_ref.at[pl.ds(fill_ref[e], MOE_TM)], sem_fill)
                 for e in range(N_EXPERTS)]
        for cp in fills:
            cp.start()
        for cp in fills:
            cp.wait()
        last = xs_ref.shape[0] // MOE_TM - 1
        for j in range(last - N_EXPERTS, last + 1):
            @pl.when(j >= fill_ref[N_EXPERTS])
            def _():
                cp = pltpu.make_async_copy(zero_ref, xs_ref.at[pl.ds(j * MOE_TM, MOE_TM)], sem_fill)
                cp.start()
                cp.wait()

    slot = i % 2
    rows_ref = u_ref.at[slot]
    sem = sem_row.at[slot]
    idx_copy = pltpu.make_async_copy(dest_ref.at[pl.ds(i * 2 * tt, 2 * tt)], idx_ref, sem_idx)
    idx_copy.start()
    _to_slabs(rows_ref, _rms(h_ref[...], nw_ref[...]))
    idx_copy.wait()

    def issue(r, carry):
        _row_copy(rows_ref, r, xs_ref, idx_ref[r], sem).start()
        _row_copy(rows_ref, r, xs_ref, idx_ref[tt + r], sem).start()
        return carry

    lax.fori_loop(0, tt, issue, 0, unroll=ROW_DMA_UNROLL)

    def drain(s):
        for _ in range(2):
            pltpu.make_async_copy(u_ref.at[s], xs_ref.at[pl.ds(0, tt)], sem_row.at[s]).wait()

    @pl.when(i > 0)
    def _():
        drain(1 - slot)

    @pl.when(i == pl.num_programs(0) - 1)
    def _():
        drain(slot)


def _dispatch(h, nw, dest_flat, fill_rows, rows, tt):
    n, d = h.shape
    return pl.pallas_call(
        _dispatch_body,
        grid_spec=pltpu.PrefetchScalarGridSpec(
            num_scalar_prefetch=1,
            grid=(n // tt,),
            in_specs=[pl.BlockSpec((tt, d), lambda i, fr: (i, 0)), pl.BlockSpec((1, d), lambda i, fr: (0, 0)),
                      pl.BlockSpec(memory_space=pl.ANY)],
            out_specs=pl.BlockSpec(memory_space=pl.ANY),
            scratch_shapes=[pltpu.VMEM((2, tt, d // LANES, LANES), f32), pltpu.SMEM((2 * tt,), jnp.int32),
                            pltpu.VMEM((MOE_TM, d // LANES, LANES), f32), pltpu.SemaphoreType.DMA(()),
                            pltpu.SemaphoreType.DMA((2,)), pltpu.SemaphoreType.DMA(())]),
        out_shape=jax.ShapeDtypeStruct((rows, d // LANES, LANES), f32),
        compiler_params=_cparams("arbitrary"),
        name="moe_dispatch",
    )(fill_rows, h, nw, dest_flat)


def _experts_body(te_ref, nu_ref, x_ref, wg_ref, wu_ref, wd_ref, y_ref, xb_ref, acc_ref):
    j = pl.program_id(0)
    f = pl.program_id(1)

    @pl.when(j < nu_ref[0])
    def _():
        @pl.when(f == 0)
        def _():
            xb_ref[...] = _from_slabs(x_ref).astype(bf16)
            acc_ref[...] = jnp.zeros_like(acc_ref)

        x = xb_ref[...]
        t = _silu(_dot(x, wg_ref[...])) * _dot(x, wu_ref[...])
        acc_ref[...] += _dot(t.astype(bf16), wd_ref[...])

        @pl.when(f == pl.num_programs(1) - 1)
        def _():
            _to_slabs(y_ref, acc_ref[...])

    @pl.when(jnp.logical_and(j >= nu_ref[0], f == pl.num_programs(1) - 1))
    def _():
        y_ref[...] = jnp.zeros_like(y_ref)


def _experts(xs, tile_expert, n_used, wg, wu, wd, n_tiles):
    d = wg.shape[1]
    slab = (MOE_TM, d // LANES, LANES)
    ff = wg.shape[2]
    tf = 256 * _largest_divisor(ff // 256, 2)
    nf = ff // tf
    tile = lambda j, nu: jnp.minimum(j, nu[0] - 1)
    chunk = lambda j, f, nu: jnp.where(j < nu[0], f, nf - 1)
    return pl.pallas_call(
        _experts_body,
        grid_spec=pltpu.PrefetchScalarGridSpec(
            num_scalar_prefetch=2,
            grid=(n_tiles, nf),
            in_specs=[pl.BlockSpec(slab, lambda j, f, te, nu: (tile(j, nu), 0, 0)),
                      pl.BlockSpec((None, d, tf), lambda j, f, te, nu: (te[tile(j, nu)], 0, chunk(j, f, nu))),
                      pl.BlockSpec((None, d, tf), lambda j, f, te, nu: (te[tile(j, nu)], 0, chunk(j, f, nu))),
                      pl.BlockSpec((None, tf, d), lambda j, f, te, nu: (te[tile(j, nu)], chunk(j, f, nu), 0))],
            out_specs=pl.BlockSpec(slab, lambda j, f, te, nu: (j, 0, 0)),
            scratch_shapes=[pltpu.VMEM((MOE_TM, d), bf16), pltpu.VMEM((MOE_TM, d), f32)]),
        out_shape=jax.ShapeDtypeStruct((n_tiles * MOE_TM, d // LANES, LANES), f32),
        compiler_params=_cparams("arbitrary", "arbitrary"),
        name="moe_experts",
    )(tile_expert, n_used, xs, wg, wu, wd)


def _combine_body(final, h_ref, route_ref, fnw_ref, dest_ref, ys_ref, out_ref, buf_ref, idx_ref,
                  sem_idx, sem_row):
    i = pl.program_id(0)
    tt = h_ref.shape[0]
    slot = i % 2

    def gather(step, s):
        idx_copy = pltpu.make_async_copy(dest_ref.at[pl.ds(step * 2 * tt, 2 * tt)], idx_ref, sem_idx)
        idx_copy.start()
        idx_copy.wait()

        def issue(r, carry):
            _row_copy(ys_ref, idx_ref[r], buf_ref.at[s, 0], r, sem_row.at[s]).start()
            _row_copy(ys_ref, idx_ref[tt + r], buf_ref.at[s, 1], r, sem_row.at[s]).start()
            return carry

        lax.fori_loop(0, tt, issue, 0, unroll=ROW_DMA_UNROLL)

    @pl.when(i == 0)
    def _():
        gather(0, 0)

    @pl.when(i + 1 < pl.num_programs(0))
    def _():
        gather(i + 1, 1 - slot)

    for k in range(2):
        pltpu.make_async_copy(ys_ref.at[pl.ds(0, tt)], buf_ref.at[slot, k], sem_row.at[slot]).wait()
    route = route_ref[...]
    g1 = route[:, ROUTE_GATE:ROUTE_GATE + 1]
    g2 = route[:, ROUTE_GATE + 1:ROUTE_GATE + 2]
    y = h_ref[...] + g1 * _from_slabs(buf_ref.at[slot, 0]) + g2 * _from_slabs(buf_ref.at[slot, 1])
    out_ref[...] = _rms(y, fnw_ref[...]) if final else y


def _combine(h, route, fnw, dest_flat, ys, tt, final):
    n, d = h.shape
    return pl.pallas_call(
        functools.partial(_combine_body, final),
        grid=(n // tt,),
        in_specs=[pl.BlockSpec((tt, d), lambda i: (i, 0)), pl.BlockSpec((tt, ROUTE_W), lambda i: (i, 0)),
                  pl.BlockSpec((1, d), lambda i: (0, 0)),
                  pl.BlockSpec(memory_space=pl.ANY), pl.BlockSpec(memory_space=pl.ANY)],
        out_specs=pl.BlockSpec((tt, d), lambda i: (i, 0)),
        out_shape=jax.ShapeDtypeStruct((n, d), f32),
        scratch_shapes=[pltpu.VMEM((2, 2, tt, d // LANES, LANES), f32), pltpu.SMEM((2 * tt,), jnp.int32),
                        pltpu.SemaphoreType.DMA(()), pltpu.SemaphoreType.DMA((2,))],
        compiler_params=_cparams("arbitrary"),
        name="moe_combine",
    )(h, route, fnw, dest_flat, ys)


def _moe(mix, h, wo, nw, router, wg, wu, wd, fnw, final):
    n, d = h.shape
    tt = TILE * _largest_divisor(n // TILE, 4)
    h, route, counts = _router(mix, h, wo, nw, router)

    cnt = counts[0, :N_EXPERTS].astype(jnp.int32)
    padded = (cnt + MOE_TM - 1) // MOE_TM * MOE_TM
    ends = jnp.cumsum(padded)
    off = ends - padded
    n_tiles = -(-2 * n // MOE_TM) + N_EXPERTS
    tile_expert = jnp.minimum(
        jnp.sum((jnp.arange(n_tiles)[:, None] * MOE_TM >= ends[None, :]).astype(jnp.int32), axis=1),
        N_EXPERTS - 1).astype(jnp.int32)
    n_used = (ends[-1:] // MOE_TM).astype(jnp.int32)
    sel = route[:, ROUTE_E:ROUTE_E + 2].astype(jnp.int32)
    rank = route[:, ROUTE_RANK:ROUTE_RANK + 2].astype(jnp.int32)
    dest = jnp.sum(jnp.where(sel[..., None] == jnp.arange(N_EXPERTS), off, 0), axis=-1) + rank
    dest_flat = dest.reshape(n // tt, tt, 2).transpose(0, 2, 1).reshape(-1)

    fill_rows = jnp.concatenate([off + cnt, n_used]).astype(jnp.int32)
    xs = _dispatch(h, nw, dest_flat, fill_rows, (n_tiles + 1) * MOE_TM, tt)
    ys = _experts(xs, tile_expert, n_used, wg, wu, wd, n_tiles)
    return _combine(h, route, fnw, dest_flat, ys, tt, final)


def _final_norm_body(h_ref, w_ref, out_ref):
    out_ref[...] = _rms(h_ref[...], w_ref[...])


def _final_norm(h, w):
    n, d = h.shape
    tm = TILE * _largest_divisor(n // TILE, 8)
    return pl.pallas_call(
        _final_norm_body,
        grid=(n // tm,),
        in_specs=[pl.BlockSpec((tm, d), lambda i: (i, 0)), pl.BlockSpec((1, d), lambda i: (0, 0))],
        out_specs=pl.BlockSpec((tm, d), lambda i: (i, 0)),
        out_shape=jax.ShapeDtypeStruct((n, d), f32),
        compiler_params=_cparams("parallel"),
        name="final_norm",
    )(h, w)


def _rope_tables(seq, pad):
    t = jnp.arange(seq)
    meta_pos = jnp.arange(N_META) - N_META
    row = jnp.concatenate([meta_pos, t // GRID_W]).astype(f32)
    col = jnp.concatenate([meta_pos, t % GRID_W]).astype(f32)
    half = HEAD_DIM // 2
    inv = ROPE_THETA ** (-jnp.arange(0, half, 2, dtype=f32) / half)
    ang = jnp.concatenate([row[:, None] * inv, col[:, None] * inv], axis=-1)
    ang = jnp.tile(jnp.repeat(ang, 2, axis=-1), (1, QK_WIDTH // HEAD_DIM))
    ang = jnp.pad(ang, ((pad, 0), (0, 0)))
    return jnp.cos(ang), jnp.sin(ang)


def _pair_swap_matrix(width):
    i = jnp.arange(width)
    p = jnp.zeros((width, width), f32)
    p = p.at[i[1::2], i[0::2]].set(-1.0)
    p = p.at[i[0::2], i[1::2]].set(1.0)
    return p.astype(bf16)


def _group_mean_matrix(width, group):
    i = jnp.arange(width)
    return ((i[:, None] // group == i[None, :] // group).astype(f32) / group).astype(bf16)


def _t5_bucket(rel):
    nb = REL_BUCKETS // 2
    max_exact = nb // 2
    ret = (rel > 0).astype(jnp.int32) * nb
    n = jnp.abs(rel)
    nf = jnp.maximum(n, 1).astype(f32)
    large = max_exact + (jnp.log(nf / max_exact) / math.log(REL_MAX_DIST / max_exact)
                         * (nb - max_exact)).astype(jnp.int32)
    large = jnp.minimum(large, nb - 1)
    return ret + jnp.where(n < max_exact, n, large)


def _swa_bias_tables(rel_bias, lp):
    def lookup(bucket):
        out = jnp.zeros((rel_bias.shape[1],) + bucket.shape, f32)
        for b in range(REL_BUCKETS):
            out = jnp.where((bucket == b)[None], rel_bias[b].astype(f32)[:, None, None], out)
        return out

    qi = jnp.arange(TILE)
    ki = jnp.arange(3 * TILE)
    rel = ki[None, :] - TILE - qi[:, None]
    band = jnp.where((jnp.abs(rel) <= WINDOW)[None], lookup(_t5_bucket(rel)), NEG)
    pos = jnp.arange(lp) - (TILE - N_META)
    rel_m = jnp.arange(N_META)[None, :] - pos[:, None]
    meta = lookup(_t5_bucket(rel_m))
    nt = lp // TILE
    band_t = jnp.transpose(band, (2, 0, 1)).reshape(3 * TILE, 4 * TILE) * LOG2E
    meta_t = (meta.reshape(4, nt, TILE, N_META).transpose(1, 3, 0, 2).reshape(nt, N_META, 4 * TILE)
              * LOG2E)
    return band_t, meta_t


def _row(v, width=None):
    v = v.astype(f32).reshape(1, -1)
    if width is not None and v.shape[1] < width:
        v = jnp.pad(v, ((0, 0), (0, width - v.shape[1])))
    return v


def kernel(x, meta_tokens, rel_bias, norm_mix_w, norm_ffn_w, w_in, ssd_conv_w, ssd_conv_b, ssd_dt_bias, ssd_a_log, ssd_d, ssd_norm_w, gla_gate_w2, gla_gate_b, gla_norm_w, swa_sink, gqa_q_norm_w, gqa_k_norm_w, w_out, ffn_w_gate, ffn_w_up, ffn_w_down, moe_router, moe_w_gate, moe_w_up, moe_w_down, final_norm_w):
    bsz, seq, d = x.shape
    depth = w_in.shape[0]
    pad = (-(seq + N_META)) % TILE
    assert pad == TILE - N_META and seq % TILE == 0
    lp = pad + N_META + seq
    n = bsz * lp

    meta = jnp.broadcast_to(meta_tokens[None].astype(x.dtype), (bsz, N_META, d))
    h = jnp.concatenate([jnp.zeros((bsz, pad, d), x.dtype), meta, x], axis=1).reshape(n, d)

    cos, sin = _rope_tables(seq, pad)
    rot = _pair_swap_matrix(QK_WIDTH)
    gavg = _group_mean_matrix(QK_WIDTH, HEAD_DIM)
    bias_band, bias_meta = _swa_bias_tables(rel_bias, lp)
    offs = [0]
    for s in IN_SIZES:
        offs.append(offs[-1] + s)

    for i in range(depth):
        wi = w_in[i]
        cols = [wi[:, offs[j]:offs[j + 1]] for j in PACK_ORDER]
        w_pack = jnp.concatenate(cols + [jnp.zeros((d, PACK_WIDTH - offs[-1]), wi.dtype)], axis=1).astype(bf16)
        o_ssd, o_small, o_gla, sk, ak, aq, av, sq, sv = _inproj(
            h, _row(norm_mix_w[i]), w_pack, cos, sin, rot, gavg,
            _row(jnp.concatenate([jnp.tile(gqa_q_norm_w[i], 4), jnp.tile(gqa_k_norm_w[i], 2)])), bsz, lp)

        convw = jnp.pad(ssd_conv_w[i].astype(f32), ((0, 8 - SSD_CONV), (0, 0)))
        convb = _row(ssd_conv_b[i])
        dtb = _row(ssd_dt_bias[i].reshape(-1), 128)
        alog = _row(ssd_a_log[i].reshape(-1), 128)
        yf, xbc = _ssd(False, o_ssd, o_small, convw, convb, dtb, alog,
                       _row(jnp.repeat(ssd_d[i], SSD_HEAD_DIM)), bsz, lp)
        y_ssd = _ssd(True, o_ssd, o_small, convw, convb, dtb, alog, (yf, xbc, _row(ssd_norm_w[i])), bsz, lp)

        def gate_w(direction):
            lo = SMALL_GA + GLA_GATE_RANK * direction
            full = jnp.zeros((128, GLA_KEY), f32).at[lo:lo + GLA_GATE_RANK].set(gla_gate_w2[i, direction].astype(f32))
            return full.astype(bf16)

        of = _gla(False, o_gla, o_small, gate_w(0), _row(gla_gate_b[i, 0]), None, bsz, lp)
        y_gla = _gla(True, o_gla, o_small, gate_w(1), _row(gla_gate_b[i, 1]),
                     (of, _row(jnp.tile(gla_norm_w[i], GLA_HEADS)), gavg), bsz, lp)

        sink = _row(jnp.repeat(swa_sink[i].astype(f32), TILE)) * LOG2E
        y_swa = _swa(sq, sk, sv, bias_band, bias_meta, sink, bsz, lp)
        y_g2 = _flash(aq, ak, av, bsz, lp)

        mix = (y_ssd, y_gla, y_swa, y_g2)
        wo = w_out[i].astype(bf16)
        j = i // 2
        if i % 2 == 0:
            h = _ffn(mix, h, wo, _row(norm_ffn_w[i]), ffn_w_gate[j].astype(bf16), ffn_w_up[j].astype(bf16),
                     ffn_w_down[j].astype(bf16))
            if i == depth - 1:
                h = _final_norm(h, _row(final_norm_w))
        else:
            router = jnp.pad(moe_router[j].astype(f32), ((0, 0), (0, 128 - N_EXPERTS)))
            h = _moe(mix, h, wo, _row(norm_ffn_w[i]), router, moe_w_gate[j].astype(bf16),
                     moe_w_up[j].astype(bf16), moe_w_down[j].astype(bf16), _row(final_norm_w), i == depth - 1)
    return h.reshape(bsz, lp, d)[:, pad + N_META:]
```

```python
import functools
import math

import jax
import jax.numpy as jnp
from jax import lax
from jax.experimental import pallas as pl
from jax.experimental.pallas import tpu as pltpu

f32 = jnp.float32
bf16 = jnp.bfloat16

N_META = 16
HEAD_DIM = 64
GRID_W = 64
EPS = 1e-6
ROPE_THETA = 10000.0
TILE = 128
SSD_HEADS = 4
SSD_HEAD_DIM = 64
SSD_INNER = 256
SSD_STATE = 128
SSD_CONV = 5
SSD_CONV_DIM = 768
GLA_HEADS = 4
GLA_DK = 32
GLA_DV = 64
GLA_KEY = 128
GLA_VAL = 256
GLA_GATE_RANK = 16
GLA_GATE_NORM = 16.0
GLA_CHUNK = 64
WINDOW = 128
REL_BUCKETS = 32
REL_MAX_DIST = 128
N_EXPERTS = 8
NEG = -1e30
LOG2E = math.log2(math.e)
QK_WIDTH = 384
CONV_HALO = 16
VMEM_LIMIT = 56 * 1024 * 1024

IN_SIZES = (256, 768, 8, 128, 128, 256, 256, 32, 256, 128, 128, 256, 128, 128)
(_Z, _XBC, _DT, _GQ, _GK, _GV, _GR, _GA, _SQ, _SK, _SV, _AQ, _AK, _AV) = range(14)
PACK_ORDER = (_XBC, _Z, _GQ, _GK, _GV, _GR, _SQ, _SK, _SV, _AQ, _AK, _AV, _DT, _GA)
PACK_WIDTH = 2944
C_SSD, C_GLA, C_SWA, C_AQ, C_AK, C_AV, C_SMALL = 0, 1024, 1792, 2304, 2560, 2688, 2816
SMALL_DT, SMALL_GA = 0, 8


def _cparams(*sem):
    return pltpu.CompilerParams(dimension_semantics=sem, vmem_limit_bytes=VMEM_LIMIT)


def _dot(a, b):
    return jnp.dot(a, b, preferred_element_type=f32)


def _dot_nt(a, b):
    return lax.dot_general(a, b, (((1,), (1,)), ((), ())), preferred_element_type=f32)


def _dot_tn(a, b):
    return lax.dot_general(a, b, (((0,), (0,)), ((), ())), preferred_element_type=f32)


def _split(a):
    hi = a.astype(bf16)
    lo = (a - hi.astype(f32)).astype(bf16)
    return hi, lo


def _dot_split_lhs(a, b):
    hi, lo = _split(a)
    return _dot(hi, b) + _dot(lo, b)


def _dot_split_rhs(t, x):
    hi, lo = _split(x)
    return _dot(t, hi) + _dot(t, lo)


def _rms(x, w):
    return x * lax.rsqrt(jnp.mean(x * x, axis=-1, keepdims=True) + EPS) * w


def _silu(x):
    return x * (0.5 + 0.5 * jnp.tanh(0.5 * x))


def _softplus(x):
    return jnp.maximum(x, 0.0) + jnp.log(1.0 + jnp.exp(-jnp.abs(x)))


def _log_sigmoid(x):
    return jnp.minimum(x, 0.0) - jnp.log(1.0 + jnp.exp(-jnp.abs(x)))


def _tri(n, rev):
    r = lax.broadcasted_iota(jnp.int32, (n, n), 0)
    c = lax.broadcasted_iota(jnp.int32, (n, n), 1)
    return (r <= c) if rev else (r >= c)


def _valid_rows(tile, n):
    rows = lax.broadcasted_iota(jnp.int32, (n, 1), 0)
    return jnp.logical_or(tile > 0, rows >= TILE - N_META)


def _inproj_body(h_ref, nw_ref, w_ref, cos_ref, sin_ref, rot_ref, gavg_ref, qkw_ref,
                 ssd_ref, small_ref, gla_ref, sk_ref, k_ref, q_ref, v_ref, sq_ref, sv_ref):
    u = _rms(h_ref[...], nw_ref[...]).astype(bf16)

    def mm(lo, hi):
        return _dot(u, w_ref[:, lo:hi])

    ssd_ref[...] = mm(C_SSD, C_GLA).astype(bf16)
    gla_ref[...] = mm(C_GLA, C_SWA).astype(bf16)
    small_ref[...] = mm(C_SMALL, PACK_WIDTH)
    akv = mm(C_AK, C_SMALL)
    v_ref[...] = akv[:, 128:].T.astype(bf16)

    skv = mm(C_SWA + 256, C_AQ)
    sk_ref[...] = skv[:, :128].astype(bf16)
    sv_ref[...] = skv[:, 128:].T.astype(bf16)
    sqt = (mm(C_SWA, C_SWA + 256) * (HEAD_DIM ** -0.5 * LOG2E)).T.astype(bf16)
    sq_ref[...] = jnp.zeros_like(sq_ref)
    for t in range(sq_ref.shape[0]):
        for hh in range(4):
            lo = HEAD_DIM * (hh // 2)
            sq_ref[t, lo:lo + HEAD_DIM, TILE * hh:TILE * (hh + 1)] = (
                sqt[HEAD_DIM * hh:HEAD_DIM * (hh + 1), TILE * t:TILE * (t + 1)])

    t = jnp.concatenate([mm(C_AQ, C_AK), akv[:, :128]], axis=1)
    ms = _dot_split_lhs(t * t, gavg_ref[...])
    tn = t * lax.rsqrt(ms + EPS) * qkw_ref[...]
    tr = _dot(tn.astype(bf16), rot_ref[...])
    qk = tn * cos_ref[...] + tr * sin_ref[...]
    q_ref[...] = (qk[:, :256] * (HEAD_DIM ** -0.5 * LOG2E)).T.astype(bf16)
    k_ref[...] = qk[:, 256:].astype(bf16)


def _inproj(h, nw, w, cos, sin, rot, gavg, qkw, bsz, lp):
    n = h.shape[0]
    d = h.shape[1]
    tm = TILE * _largest_divisor(lp // TILE, 3)
    per = lp // tm
    row = lambda b, i: (b * per + i, 0)
    const = lambda b, i: (0, 0)
    tab = lambda b, i: (i, 0)
    outs = ((1024, bf16), (128, f32), (768, bf16), (128, bf16), (128, bf16))
    slab = lambda b, i: (b * per + i, 0, 0)
    sub = tm // TILE
    return pl.pallas_call(
        _inproj_body,
        grid=(bsz, per),
        in_specs=[pl.BlockSpec((tm, d), row), pl.BlockSpec((1, d), const),
                  pl.BlockSpec((d, PACK_WIDTH), const),
                  pl.BlockSpec((tm, QK_WIDTH), tab), pl.BlockSpec((tm, QK_WIDTH), tab),
                  pl.BlockSpec((QK_WIDTH, QK_WIDTH), const), pl.BlockSpec((QK_WIDTH, QK_WIDTH), const),
                  pl.BlockSpec((1, QK_WIDTH), const)],
        out_specs=([pl.BlockSpec((tm, c), row) for c, _ in outs]
                   + [pl.BlockSpec((None, 256, tm), slab), pl.BlockSpec((None, 128, tm), slab),
                      pl.BlockSpec((sub, 128, 4 * TILE), slab), pl.BlockSpec((None, 128, tm), slab)]),
        out_shape=([jax.ShapeDtypeStruct((n, c), t) for c, t in outs]
                   + [jax.ShapeDtypeStruct((n // tm, 256, tm), bf16),
                      jax.ShapeDtypeStruct((n // tm, 128, tm), bf16),
                      jax.ShapeDtypeStruct((n // TILE, 128, 4 * TILE), bf16),
                      jax.ShapeDtypeStruct((n // tm, 128, tm), bf16)]),
        compiler_params=_cparams("parallel", "parallel"),
        name="inproj",
    )(h, nw, w, cos, sin, rot, gavg, qkw)


def _largest_divisor(n, cap):
    return max(k for k in range(1, cap + 1) if n % k == 0)


def _for_each_batch(chain, batched, n_scratch):
    def body(*refs):
        nb = next(r.shape[0] for r, flag in zip(refs, batched) if flag)
        io, scratch = refs[:len(batched)], refs[len(batched):]
        assert len(scratch) == n_scratch * nb
        for b in range(nb):
            chain(*[r.at[b] if flag else r for r, flag in zip(io, batched)],
                  *[scratch[k * nb + b] for k in range(n_scratch)])
    return body


def _per_batch_scratch(nb, *shapes):
    return [pltpu.VMEM(shape, dtype) for shape, dtype in shapes for _ in range(nb)]


def _ssd_scan_tile(rev, xs, bmat, cmat, dt, cum, tot, st):
    col0 = SMALL_DT + (SSD_HEADS if rev else 0)
    causal = _tri(TILE, rev)
    cum_t = cum.T
    e_cum = jnp.exp(cum)
    e_tot = jnp.exp(tot)
    head_of = lax.broadcasted_iota(jnp.int32, (TILE, SSD_INNER), 1) // SSD_HEAD_DIM
    head_of_row = lax.broadcasted_iota(jnp.int32, (1, SSD_INNER), 1) // SSD_HEAD_DIM

    def widen(cols, like):
        out = jnp.zeros(like.shape, f32)
        for hh in range(SSD_HEADS):
            out = jnp.where(like == hh, cols[:, col0 + hh:col0 + hh + 1], out)
        return out

    def own_head(r):
        out = jnp.zeros((TILE, SSD_INNER), f32)
        for hh in range(SSD_HEADS):
            out = jnp.where(head_of == hh, r[TILE * hh:TILE * (hh + 1)], out)
        return out

    xd = (xs * widen(dt, head_of)).astype(bf16)
    scores, bws = [], []
    for g in range(2):
        bg = bmat[:, SSD_STATE * g:SSD_STATE * (g + 1)]
        cbg = _dot_nt(cmat[:, SSD_STATE * g:SSD_STATE * (g + 1)], bg)
        for hh in (2 * g, 2 * g + 1):
            col = col0 + hh
            a_col = cum[:, col:col + 1]
            decay = jnp.where(causal, jnp.exp(a_col - cum_t[col:col + 1, :]), 0.0)
            scores.append((cbg * decay).astype(bf16))
            bws.append((bg.astype(f32) * jnp.exp(tot[:, col:col + 1] - a_col)).astype(bf16))
    y = own_head(_dot(jnp.concatenate(scores, axis=0), xd))
    y = y + _dot(cmat, st.astype(bf16)) * widen(e_cum, head_of)
    upd = _dot_tn(jnp.concatenate(bws, axis=1), xd)
    rows = [jnp.where(head_of == 2 * g, upd[SSD_STATE * 2 * g:SSD_STATE * (2 * g + 1)],
                      jnp.where(head_of == 2 * g + 1,
                                upd[SSD_STATE * (2 * g + 1):SSD_STATE * (2 * g + 2)], 0.0))
            for g in range(2)]
    return y, st * widen(e_tot, head_of_row) + jnp.concatenate(rows, axis=0)


def _ssd_body(rev, nt, *refs):
    if rev:
        xbc_ref, small_ref, dtb_ref, alog_ref, yf_ref, z_ref, normw_ref, out_ref, state_ref = refs
    else:
        (cur_ref, prev_ref, next_ref, small_ref, convw_ref, convb_ref, dtb_ref, alog_ref, dskip_ref,
         out_ref, xbc_out_ref, state_ref) = refs[:12]
        ext_refs = refs[12:]
    c = pl.program_id(0)
    tile = (nt - 1 - c) if rev else c
    nb = small_ref.shape[0]
    valid = _valid_rows(tile, TILE)

    a = -jnp.exp(alog_ref[...])
    dts = [jnp.where(valid, _softplus(small_ref[b] + dtb_ref[...]), 0.0) for b in range(nb)]
    cum_all = _dot_split_rhs(_tri(TILE, rev).astype(bf16),
                             jnp.concatenate([dt * a for dt in dts], axis=1))
    for b in range(nb):
        dt = dts[b]
        if rev:
            xbc = xbc_ref[b]
            xs = xbc[:, :SSD_INNER].astype(f32)
            bc = xbc[:, SSD_INNER:]
        else:
            ext_ref = ext_refs[b]
            ext_ref[0:CONV_HALO, :] = jnp.where(tile > 0, prev_ref[b].astype(f32), 0.0)
            ext_ref[CONV_HALO:CONV_HALO + TILE, :] = jnp.where(valid, cur_ref[b].astype(f32), 0.0)
            ext_ref[CONV_HALO + TILE:, :] = jnp.where(tile < nt - 1, next_ref[b].astype(f32), 0.0)
            first = CONV_HALO - (SSD_CONV - 1) // 2
            strips = []
            for lo in range(0, SSD_CONV_DIM, 128):
                cols = slice(lo, lo + 128)
                acc = jnp.zeros((TILE, 128), f32) + convb_ref[:, cols]
                for k in range(SSD_CONV):
                    acc = acc + convw_ref[k:k + 1, cols] * ext_ref[first + k:first + k + TILE, cols]
                strips.append(jnp.where(valid, _silu(acc), 0.0))
            xbc = jnp.concatenate(strips, axis=1)
            xbc_out_ref[b] = xbc.astype(bf16)
            xs = xbc[:, :SSD_INNER]
            bc = xbc[:, SSD_INNER:].astype(bf16)
        st = jnp.where(c == 0, 0.0, state_ref[b])
        y, st = _ssd_scan_tile(rev, xs, bc[:, :2 * SSD_STATE], bc[:, 2 * SSD_STATE:], dt,
                               cum_all[:, 128 * b:128 * (b + 1)],
                               jnp.sum(dt * a, axis=0, keepdims=True), st)
        state_ref[b] = st
        if rev:
            y = (yf_ref[b] + y) * _silu(z_ref[b].astype(f32))
            out_ref[b] = jnp.where(valid, _rms(y, normw_ref[...]), 0.0).astype(out_ref.dtype)
        else:
            out_ref[b] = y + dskip_ref[...] * xs


def _ssd(rev, o_ssd, o_small, convw, convb, dtb, alog, extra, bsz, lp):
    n = o_ssd.shape[0]
    nt = lp // TILE
    hb = TILE // CONV_HALO
    o_ssd = o_ssd.reshape(bsz, lp, -1)
    o_small = o_small.reshape(bsz, lp, -1)

    def tile_of(c):
        return (nt - 1 - c) if rev else c

    cur = lambda c: (0, tile_of(c), 0)
    prev = lambda c: (0, jnp.maximum(tile_of(c) * hb - 1, 0), 0)
    nxt = lambda c: (0, jnp.minimum((tile_of(c) + 1) * hb, lp // CONV_HALO - 1), 0)
    zcol = lambda c: (0, tile_of(c), SSD_CONV_DIM // SSD_INNER)
    const = lambda c: (0, 0)
    tile3 = lambda width: pl.BlockSpec((bsz, TILE, width), cur)
    state = pltpu.VMEM((bsz, 2 * SSD_STATE, SSD_INNER), f32)
    if rev:
        yf, xbc, normw = extra
        out = pl.pallas_call(
            functools.partial(_ssd_body, rev, nt),
            grid=(nt,),
            in_specs=[tile3(SSD_CONV_DIM), tile3(128), pl.BlockSpec((1, 128), const),
                      pl.BlockSpec((1, 128), const), tile3(SSD_INNER),
                      pl.BlockSpec((bsz, TILE, SSD_INNER), zcol), pl.BlockSpec((1, SSD_INNER), const)],
            out_specs=tile3(SSD_INNER),
            out_shape=jax.ShapeDtypeStruct((bsz, lp, SSD_INNER), bf16),
            scratch_shapes=[state],
            compiler_params=_cparams("arbitrary"),
            name="ssd_rev",
        )(xbc, o_small, dtb, alog, yf, o_ssd, normw)
        return out.reshape(n, SSD_INNER)
    halo = lambda im: pl.BlockSpec((bsz, CONV_HALO, SSD_CONV_DIM), im)
    return pl.pallas_call(
        functools.partial(_ssd_body, rev, nt),
        grid=(nt,),
        in_specs=[tile3(SSD_CONV_DIM), halo(prev), halo(nxt), tile3(128),
                  pl.BlockSpec((8, SSD_CONV_DIM), const), pl.BlockSpec((1, SSD_CONV_DIM), const),
                  pl.BlockSpec((1, 128), const), pl.BlockSpec((1, 128), const),
                  pl.BlockSpec((1, SSD_INNER), const)],
        out_specs=[tile3(SSD_INNER), tile3(SSD_CONV_DIM)],
        out_shape=[jax.ShapeDtypeStruct((bsz, lp, SSD_INNER), f32),
                   jax.ShapeDtypeStruct((bsz, lp, SSD_CONV_DIM), bf16)],
        scratch_shapes=[state] + _per_batch_scratch(bsz, ((TILE + 2 * CONV_HALO, SSD_CONV_DIM), f32)),
        compiler_params=_cparams("arbitrary"),
        name="ssd_fwd",
    )(o_ssd, o_ssd, o_ssd, o_small, convw, convb, dtb, alog, extra)


def _gla_body(rev, nt, *refs):
    if rev:
        x_ref, small_ref, wg_ref, gb_ref, of_ref, normw_ref, gavg_ref, out_ref, st_ref = refs
    else:
        x_ref, small_ref, wg_ref, gb_ref, out_ref, st_ref = refs
    c = pl.program_id(0)
    tile = (nt - 1 - c) if rev else c
    nb = x_ref.shape[0]
    n = GLA_CHUNK
    valid = _valid_rows(tile, TILE)

    pre = _dot(small_ref[...].reshape(nb * TILE, 128).astype(bf16), wg_ref[...]) + gb_ref[...]
    g_all = _log_sigmoid(pre) / GLA_GATE_NORM
    gs = [jnp.where(valid, g_all[TILE * b:TILE * (b + 1)], 0.0) for b in range(nb)]
    row = lax.broadcasted_iota(jnp.int32, (TILE, TILE), 0)
    col = lax.broadcasted_iota(jnp.int32, (TILE, TILE), 1)
    same_chunk = (row // n) == (col // n)
    cum_mat = jnp.where(jnp.logical_and(same_chunk, (row <= col) if rev else (row >= col)), 1.0, 0.0)
    bc_all = _dot_split_rhs(cum_mat.astype(bf16), jnp.concatenate(gs, axis=1))

    first = slice(n, 2 * n) if rev else slice(0, n)
    second = slice(0, n) if rev else slice(n, 2 * n)
    rows = lax.broadcasted_iota(jnp.int32, (TILE, 1), 0)
    in_first = (rows >= n) if rev else (rows < n)
    tri = _tri(n, rev)
    tri4 = jnp.concatenate([tri] * GLA_HEADS, axis=0)
    lane_head = lax.broadcasted_iota(jnp.int32, (n, GLA_KEY), 1) // GLA_DK
    out_head = lax.broadcasted_iota(jnp.int32, (n, GLA_VAL), 1) // GLA_DV
    blockdiag = (lax.broadcasted_iota(jnp.int32, (GLA_VAL, GLA_KEY), 0) // GLA_DV
                 == lax.broadcasted_iota(jnp.int32, (GLA_VAL, GLA_KEY), 1) // GLA_DK)

    def stack_heads(a):
        return jnp.concatenate([jnp.where(lane_head == hh, a, 0.0) for hh in range(GLA_HEADS)],
                               axis=0).astype(bf16)

    def own_head(r):
        out = jnp.zeros((n, GLA_VAL), f32)
        for hh in range(GLA_HEADS):
            out = jnp.where(out_head == hh, r[n * hh:n * (hh + 1)], out)
        return out

    both = []
    for b in range(nb):
        x = x_ref[b].astype(f32)
        g = gs[b]
        bc = bc_all[:, 128 * b:128 * (b + 1)]
        q = jnp.where(valid, x[:, :GLA_KEY], 0.0) * GLA_DK ** -0.5
        k = jnp.where(valid, x[:, GLA_KEY:2 * GLA_KEY], 0.0)
        v = jnp.where(valid, x[:, 2 * GLA_KEY:2 * GLA_KEY + GLA_VAL], 0.0).astype(bf16)
        bl_first = jnp.sum(g[first], axis=0, keepdims=True)
        bl_second = jnp.sum(g[second], axis=0, keepdims=True)
        qt = q * jnp.exp(bc)
        kt = k * jnp.exp(-bc)
        kw = k * jnp.exp(jnp.where(in_first, bl_first, bl_second) - bc)

        att_f = jnp.where(tri4, _dot_nt(stack_heads(qt[first]), kt[first].astype(bf16)), 0.0)
        o_f = own_head(_dot(att_f.astype(bf16), v[first]))
        keys = jnp.concatenate([kw[first], kt[second]], axis=0).astype(bf16)
        vals = jnp.concatenate([v[first], v[second]], axis=0)
        att_s = _dot_nt(stack_heads(qt[second]), keys)
        att_s = jnp.concatenate([att_s[:, :n], jnp.where(tri4, att_s[:, n:], 0.0)], axis=1)
        o_s = own_head(_dot(att_s.astype(bf16), vals))

        st = jnp.where(c == 0, 0.0, st_ref[b])
        q_in = qt * jnp.exp(jnp.where(in_first, 0.0, bl_first))
        o = _dot_nt(q_in.astype(bf16), st.astype(bf16)) + jnp.concatenate(
            [o_s, o_f] if rev else [o_f, o_s], axis=0)
        k_out = (kw * jnp.exp(jnp.where(in_first, bl_second, 0.0))).astype(bf16)
        st_ref[b] = st * jnp.exp(bl_first + bl_second) + jnp.where(blockdiag, _dot_tn(v, k_out), 0.0)

        if rev:
            both.append(of_ref[b] + o)
        else:
            out_ref[b] = o

    if rev:
        o = jnp.concatenate(both, axis=0)
        ms = _dot_split_lhs(o * o, gavg_ref[...])
        on = o * lax.rsqrt(ms + EPS) * normw_ref[...]
        for b in range(nb):
            r = x_ref[b, :, 2 * GLA_KEY + GLA_VAL:].astype(f32)
            out_ref[b] = jnp.where(valid, on[TILE * b:TILE * (b + 1)] * _silu(r), 0.0).astype(out_ref.dtype)


def _gla(rev, o_gla, o_small, wg, gb, extra, bsz, lp):
    n = o_gla.shape[0]
    nt = lp // TILE
    cur = lambda c: (0, (nt - 1 - c) if rev else c, 0)
    const = lambda c: (0, 0)
    in_specs = [pl.BlockSpec((bsz, TILE, 768), cur), pl.BlockSpec((bsz, TILE, 128), cur),
                pl.BlockSpec((128, GLA_KEY), const), pl.BlockSpec((1, GLA_KEY), const)]
    args = [o_gla.reshape(bsz, lp, -1), o_small.reshape(bsz, lp, -1), wg, gb]
    if rev:
        of, normw, gavg = extra
        in_specs += [pl.BlockSpec((bsz, TILE, GLA_VAL), cur), pl.BlockSpec((1, GLA_VAL), const),
                     pl.BlockSpec((GLA_VAL, GLA_VAL), const)]
        args += [of.reshape(bsz, lp, -1), normw, gavg]
    out = pl.pallas_call(
        functools.partial(_gla_body, rev, nt),
        grid=(nt,),
        in_specs=in_specs,
        out_specs=pl.BlockSpec((bsz, TILE, GLA_VAL), cur),
        out_shape=jax.ShapeDtypeStruct((bsz, lp, GLA_VAL), bf16 if rev else f32),
        scratch_shapes=[pltpu.VMEM((bsz, GLA_VAL, GLA_KEY), f32)],
        compiler_params=_cparams("arbitrary"),
        name="gla_rev" if rev else "gla_fwd",
    )(*args)
    return out.reshape(n, GLA_VAL)


def _swa_body(nt, qp_ref, kp_ref, kc_ref, kn_ref, km_ref, vp_ref, vc_ref, vn_ref, vm_ref,
              bias_ref, bmeta_ref, sink_ref, out_ref):
    c = pl.program_id(0)
    krow = lax.broadcasted_iota(jnp.int32, (3 * TILE, 1), 0)
    pen = jnp.where(krow < TILE, jnp.where(c >= 2, 0.0, NEG),
                    jnp.where(krow < 2 * TILE, jnp.where(c >= 1, 0.0, NEG),
                              jnp.where(c <= nt - 2, 0.0, NEG)))
    qp = qp_ref[...]
    kcat = jnp.concatenate([kp_ref[...], kc_ref[...], kn_ref[...]], axis=0)
    s = _dot(kcat, qp) + bias_ref[...] + pen
    sm = _dot(km_ref[...], qp) + bmeta_ref[...]
    sk = sink_ref[...]
    m = jnp.maximum(jnp.maximum(jnp.max(s, axis=0, keepdims=True),
                                jnp.max(sm, axis=0, keepdims=True)), sk)
    p = jnp.exp2(s - m)
    pm = jnp.exp2(sm - m)
    inv = 1.0 / (jnp.sum(p, axis=0, keepdims=True) + jnp.sum(pm, axis=0, keepdims=True)
                 + jnp.exp2(sk - m))
    pb = p.astype(bf16)
    pmb = jnp.concatenate([jnp.zeros((TILE - N_META, 4 * TILE), bf16), pm.astype(bf16)], axis=0)
    vcat = jnp.concatenate([vp_ref[...], vc_ref[...], vn_ref[...]], axis=1)
    heads = []
    for g in range(2):
        rows = slice(HEAD_DIM * g, HEAD_DIM * (g + 1))
        cols = slice(2 * TILE * g, 2 * TILE * (g + 1))
        pv = (_dot(vcat[rows, :], pb[:, cols]) + _dot(vm_ref[rows, :], pmb[:, cols])) * inv[:, cols]
        heads += [pv[:, :TILE], pv[:, TILE:]]
    o = jnp.concatenate(heads, axis=0).T
    out_ref[...] = jnp.where(_valid_rows(c, TILE), o, 0.0).astype(out_ref.dtype)


def _swa(qp, k, vt, bias_band, bias_meta, sink, bsz, lp):
    n = k.shape[0]
    nt = lp // TILE
    per, _, tm = vt.shape[0] // bsz, vt.shape[1], vt.shape[2]
    sub = tm // TILE
    qp = qp.reshape(bsz, nt, 128, 4 * TILE)
    k = k.reshape(bsz, lp, 128)
    vt = vt.reshape(bsz, per, 128, tm)
    prev = lambda c: jnp.maximum(c - 1, 0)
    nxt = lambda c: jnp.minimum(c + 1, nt - 1)
    kspec = lambda tile: pl.BlockSpec((bsz, TILE, 128), lambda c: (0, tile(c), 0))
    vspec = lambda tile: pl.BlockSpec((bsz, None, 128, TILE), lambda c: (0, tile(c) // sub, 0, tile(c) % sub))
    same = lambda c: c
    first = lambda c: 0
    out = pl.pallas_call(
        _for_each_batch(functools.partial(_swa_body, nt), [True] * 9 + [False] * 3 + [True], 0),
        grid=(nt,),
        in_specs=[pl.BlockSpec((bsz, None, 128, 4 * TILE), lambda c: (0, c, 0, 0)),
                  kspec(prev), kspec(same), kspec(nxt),
                  pl.BlockSpec((bsz, N_META, 128), lambda c: (0, TILE // N_META - 1, 0)),
                  vspec(prev), vspec(same), vspec(nxt), vspec(first),
                  pl.BlockSpec((3 * TILE, 4 * TILE), lambda c: (0, 0)),
                  pl.BlockSpec((None, N_META, 4 * TILE), lambda c: (c, 0, 0)),
                  pl.BlockSpec((1, 4 * TILE), lambda c: (0, 0))],
        out_specs=pl.BlockSpec((bsz, TILE, 256), lambda c: (0, c, 0)),
        out_shape=jax.ShapeDtypeStruct((bsz, lp, 256), bf16),
        compiler_params=_cparams("arbitrary"),
        name="swa",
    )(qp, k, k, k, k, vt, vt, vt, vt, bias_band, bias_meta, sink)
    return out.reshape(n, 256)


def _flash_body(nk, tq, tk, qt_ref, k_ref, vt_ref, out_ref, qpad_ref, m_ref, l_ref, acc_ref,
                sa_ref, sb_ref, ca_ref, cb_ref):
    i = pl.program_id(1)
    buf_a, buf_b = (sa_ref, ca_ref), (sb_ref, cb_ref)
    krow = lax.broadcasted_iota(jnp.int32, (tk, 1), 0)
    qpad_ref[...] = jnp.zeros_like(qpad_ref)
    for hh in range(4):
        lo = HEAD_DIM * (hh // 2)
        qpad_ref[lo:lo + HEAD_DIM, tq * hh:tq * (hh + 1)] = qt_ref[HEAD_DIM * hh:HEAD_DIM * (hh + 1), :]
    m_ref[...] = jnp.full_like(m_ref, NEG)
    l_ref[...] = jnp.zeros_like(l_ref)
    acc_ref[...] = jnp.zeros_like(acc_ref)

    def scores(j, buf, first=False):
        s_ref, cmax_ref = buf
        s = _dot(k_ref[pl.ds(pl.multiple_of(j * tk, tk), tk), :], qpad_ref[...])
        if first:
            s = jnp.where(krow >= TILE - N_META, s, NEG)
        s_ref[...] = s
        cmax_ref[...] = jnp.max(s, axis=0, keepdims=True)

    def absorb(j, buf):
        s_ref, cmax_ref = buf
        s = s_ref[...]
        m_old = m_ref[...]
        m_new = jnp.maximum(m_old, cmax_ref[...])
        alpha = jnp.exp2(m_old - m_new)
        p = jnp.exp2(s - m_new)
        l_ref[...] = alpha * l_ref[...] + jnp.sum(p, axis=0, keepdims=True)
        m_ref[...] = m_new
        pb = p.astype(bf16)
        for g in range(2):
            pv = _dot(vt_ref[j, HEAD_DIM * g:HEAD_DIM * (g + 1), :], pb[:, 2 * g * tq:(2 * g + 2) * tq])
            for r in range(2):
                hh = 2 * g + r
                rows = slice(HEAD_DIM * hh, HEAD_DIM * (hh + 1))
                acc_ref[rows, :] = (alpha[:, tq * hh:tq * (hh + 1)] * acc_ref[rows, :]
                                    + pv[:, tq * r:tq * (r + 1)])

    scores(0, buf_a, first=True)

    def body(jj, carry):
        j = 2 * jj
        scores(j + 1, buf_b)
        absorb(j, buf_a)
        scores(j + 2, buf_a)
        absorb(j + 1, buf_b)
        return carry

    lax.fori_loop(0, (nk - 1) // 2, body, 0)
    if nk % 2 == 1:
        absorb(nk - 1, buf_a)
    else:
        scores(nk - 1, buf_b)
        absorb(nk - 2, buf_a)
        absorb(nk - 1, buf_b)
    linv = 1.0 / l_ref[...]
    for hh in range(4):
        rows = slice(HEAD_DIM * hh, HEAD_DIM * (hh + 1))
        acc_ref[rows, :] = acc_ref[rows, :] * linv[:, tq * hh:tq * (hh + 1)]
    rows = lax.broadcasted_iota(jnp.int32, (tq, 1), 0)
    valid = jnp.logical_or(i > 0, rows >= TILE - N_META)
    out_ref[...] = jnp.where(valid, acc_ref[...].T, 0.0).astype(out_ref.dtype)


def _flash(qt, k, vt, bsz, lp):
    n = k.shape[0]
    tq = qt.shape[2]
    per = lp // tq
    return pl.pallas_call(
        functools.partial(_flash_body, per, tq, tq),
        grid=(bsz, per),
        in_specs=[pl.BlockSpec((None, 256, tq), lambda b, i: (b * per + i, 0, 0)),
                  pl.BlockSpec((lp, 128), lambda b, i: (b, 0)),
                  pl.BlockSpec((per, 128, tq), lambda b, i: (b, 0, 0))],
        out_specs=pl.BlockSpec((tq, 256), lambda b, i: (b * per + i, 0)),
        out_shape=jax.ShapeDtypeStruct((n, 256), bf16),
        scratch_shapes=[pltpu.VMEM((128, 4 * tq), bf16), pltpu.VMEM((1, 4 * tq), f32),
                        pltpu.VMEM((1, 4 * tq), f32), pltpu.VMEM((256, tq), f32),
                        pltpu.VMEM((tq, 4 * tq), f32), pltpu.VMEM((tq, 4 * tq), f32),
                        pltpu.VMEM((1, 4 * tq), f32), pltpu.VMEM((1, 4 * tq), f32)],
        compiler_params=_cparams("parallel", "parallel"),
        name="gqa_full",
    )(qt, k, vt)


def _mix_residual(y_refs, h_ref, wo_ref):
    mixed = jnp.concatenate([y_ref[...] for y_ref in y_refs], axis=1)
    return h_ref[...] + _dot(mixed, wo_ref[...])


def _ffn_body(y0_ref, y1_ref, y2_ref, y3_ref, h_ref, wo_ref, nw_ref, wg_ref, wu_ref, wd_ref,
              out_ref, hn_ref, u_ref, acc_ref):
    j = pl.program_id(1)

    @pl.when(j == 0)
    def _():
        hn = _mix_residual((y0_ref, y1_ref, y2_ref, y3_ref), h_ref, wo_ref)
        hn_ref[...] = hn
        u_ref[...] = _rms(hn, nw_ref[...]).astype(bf16)
        acc_ref[...] = jnp.zeros_like(acc_ref)

    u = u_ref[...]
    t = _silu(_dot(u, wg_ref[...])) * _dot(u, wu_ref[...])
    acc_ref[...] += _dot(t.astype(bf16), wd_ref[...])

    @pl.when(j == pl.num_programs(1) - 1)
    def _():
        out_ref[...] = hn_ref[...] + acc_ref[...]


def _ffn(ys, h, wo, nw, wg, wu, wd):
    n, d = h.shape
    ff = wg.shape[1]
    tm = TILE * _largest_divisor(n // TILE, 4)
    tf = 128 * _largest_divisor(ff // 128, 11)
    row = lambda i, j: (i, 0)
    const = lambda i, j: (0, 0)
    return pl.pallas_call(
        _ffn_body,
        grid=(n // tm, ff // tf),
        in_specs=[pl.BlockSpec((tm, 256), row)] * 4
        + [pl.BlockSpec((tm, d), row), pl.BlockSpec((d, d), const), pl.BlockSpec((1, d), const),
           pl.BlockSpec((d, tf), lambda i, j: (0, j)), pl.BlockSpec((d, tf), lambda i, j: (0, j)),
           pl.BlockSpec((tf, d), lambda i, j: (j, 0))],
        out_specs=pl.BlockSpec((tm, d), row),
        out_shape=jax.ShapeDtypeStruct((n, d), f32),
        scratch_shapes=[pltpu.VMEM((tm, d), f32), pltpu.VMEM((tm, d), bf16), pltpu.VMEM((tm, d), f32)],
        compiler_params=_cparams("parallel", "arbitrary"),
        name="ffn",
    )(*ys, h, wo, nw, wg, wu, wd)


MOE_TM = 1024
ROUTE_E, ROUTE_RANK, ROUTE_GATE, ROUTE_W = 0, 2, 4, 8


def _router_body(y0_ref, y1_ref, y2_ref, y3_ref, h_ref, wo_ref, nw_ref, r_ref,
                 hn_ref, route_ref, cnt_ref, base_ref):
    @pl.when(pl.program_id(0) == 0)
    def _():
        base_ref[...] = jnp.zeros_like(base_ref)

    hn = _mix_residual((y0_ref, y1_ref, y2_ref, y3_ref), h_ref, wo_ref)
    hn_ref[...] = hn
    u = _rms(hn, nw_ref[...])
    u_hi, u_lo = _split(u)
    r_hi, r_lo = _split(r_ref[...])
    logits = _dot(u_hi, r_hi) + _dot(u_lo, r_hi) + _dot(u_hi, r_lo)
    tm = logits.shape[0]
    lane = lax.broadcasted_iota(jnp.int32, logits.shape, 1)
    logits = jnp.where(lane < N_EXPERTS, logits, NEG)
    m1 = jnp.max(logits, axis=-1, keepdims=True)
    i1 = jnp.min(jnp.where(logits == m1, lane, 128), axis=-1, keepdims=True)
    rest = jnp.where(lane == i1, NEG, logits)
    m2 = jnp.max(rest, axis=-1, keepdims=True)
    i2 = jnp.min(jnp.where(rest == m2, lane, 128), axis=-1, keepdims=True)
    e2 = jnp.exp(m2 - m1)
    g1 = 1.0 / (1.0 + e2)
    g2 = e2 * g1

    sel1 = lane == i1
    sel2 = lane == i2
    onehot = jnp.where(sel1, 1.0, jnp.where(sel2, 1.0, 0.0))
    strict = jnp.where(lax.broadcasted_iota(jnp.int32, (tm, tm), 0)
                       > lax.broadcasted_iota(jnp.int32, (tm, tm), 1), 1.0, 0.0).astype(bf16)
    before = _dot(strict, onehot.astype(bf16)) + base_ref[0:1, :]
    r1 = jnp.sum(jnp.where(sel1, before, 0.0), axis=-1, keepdims=True)
    r2 = jnp.sum(jnp.where(sel2, before, 0.0), axis=-1, keepdims=True)
    route = jnp.zeros(logits.shape, f32)
    for k, val in enumerate((i1.astype(f32), i2.astype(f32), r1, r2, g1, g2)):
        route = jnp.where(lane == k, val, route)
    route_ref[...] = route[:, :ROUTE_W]
    base_ref[0:1, :] = base_ref[0:1, :] + jnp.sum(onehot, axis=0, keepdims=True)
    cnt_ref[...] = base_ref[...]


def _router(ys, h, wo, nw, router):
    n, d = h.shape
    tm = TILE * _largest_divisor(n // TILE, 8)
    row = lambda i: (i, 0)
    const = lambda i: (0, 0)
    return pl.pallas_call(
        _router_body,
        grid=(n // tm,),
        in_specs=[pl.BlockSpec((tm, 256), row)] * 4
        + [pl.BlockSpec((tm, d), row), pl.BlockSpec((d, d), const), pl.BlockSpec((1, d), const),
           pl.BlockSpec((d, 128), const)],
        out_specs=[pl.BlockSpec((tm, d), row), pl.BlockSpec((tm, ROUTE_W), row), pl.BlockSpec((8, 128), const)],
        out_shape=[jax.ShapeDtypeStruct((n, d), f32), jax.ShapeDtypeStruct((n, ROUTE_W), f32),
                   jax.ShapeDtypeStruct((8, 128), f32)],
        scratch_shapes=[pltpu.VMEM((8, 128), f32)],
        compiler_params=_cparams("arbitrary"),
        name="moe_router",
    )(*ys, h, wo, nw, router)


LANES = 128
ROW_DMA_UNROLL = 8


def _row_copy(src_ref, src_row, dst_ref, dst_row, sem):
    return pltpu.make_async_copy(src_ref.at[src_row], dst_ref.at[dst_row], sem)


def _to_slabs(dst_ref, val):
    dst_ref[...] = val.reshape(dst_ref.shape)


def _from_slabs(src_ref):
    rows, chunks, lanes = src_ref.shape
    return src_ref[...].reshape(rows, chunks * lanes)


def _dispatch_body(fill_ref, h_ref, nw_ref, dest_ref, xs_ref, u_ref, idx_ref, zero_ref,
                   sem_idx, sem_row, sem_fill):
    i = pl.program_id(0)
    tt = u_ref.shape[1]

    @pl.when(i == 0)
    def _():
        zero_ref[...] = jnp.zeros_like(zero_ref)
        fills = [pltpu.make_async_copy(zero_ref, xs_ref.at[pl.ds(fill_ref[e], MOE_TM)], sem_fill)
                 for e in range(N_EXPERTS)]
        for cp in fills:
            cp.start()
        for cp in fills:
            cp.wait()
        last = xs_ref.shape[0] // MOE_TM - 1
        for j in range(last - N_EXPERTS, last + 1):
            @pl.when(j >= fill_ref[N_EXPERTS])
            def _():
                cp = pltpu.make_async_copy(zero_ref, xs_ref.at[pl.ds(j * MOE_TM, MOE_TM)], sem_fill)
                cp.start()
                cp.wait()

    slot = i % 2
    rows_ref = u_ref.at[slot]
    sem = sem_row.at[slot]
    idx_copy = pltpu.make_async_copy(dest_ref.at[pl.ds(i * 2 * tt, 2 * tt)], idx_ref, sem_idx)
    idx_copy.start()
    _to_slabs(rows_ref, _rms(h_ref[...], nw_ref[...]))
    idx_copy.wait()

    def issue(r, carry):
        _row_copy(rows_ref, r, xs_ref, idx_ref[r], sem).start()
        _row_copy(rows_ref, r, xs_ref, idx_ref[tt + r], sem).start()
        return carry

    lax.fori_loop(0, tt, issue, 0, unroll=ROW_DMA_UNROLL)

    def drain(s):
        for _ in range(2):
            pltpu.make_async_copy(u_ref.at[s], xs_ref.at[pl.ds(0, tt)], sem_row.at[s]).wait()

    @pl.when(i > 0)
    def _():
        drain(1 - slot)

    @pl.when(i == pl.num_programs(0) - 1)
    def _():
        drain(slot)


def _dispatch(h, nw, dest_flat, fill_rows, rows, tt):
    n, d = h.shape
    return pl.pallas_call(
        _dispatch_body,
        grid_spec=pltpu.PrefetchScalarGridSpec(
            num_scalar_prefetch=1,
            grid=(n // tt,),
            in_specs=[pl.BlockSpec((tt, d), lambda i, fr: (i, 0)), pl.BlockSpec((1, d), lambda i, fr: (0, 0)),
                      pl.BlockSpec(memory_space=pl.ANY)],
            out_specs=pl.BlockSpec(memory_space=pl.ANY),
            scratch_shapes=[pltpu.VMEM((2, tt, d // LANES, LANES), f32), pltpu.SMEM((2 * tt,), jnp.int32),
                            pltpu.VMEM((MOE_TM, d // LANES, LANES), f32), pltpu.SemaphoreType.DMA(()),
                            pltpu.SemaphoreType.DMA((2,)), pltpu.SemaphoreType.DMA(())]),
        out_shape=jax.ShapeDtypeStruct((rows, d // LANES, LANES), f32),
        compiler_params=_cparams("arbitrary"),
        name="moe_dispatch",
    )(fill_rows, h, nw, dest_flat)


def _experts_body(te_ref, nu_ref, x_ref, wg_ref, wu_ref, wd_ref, y_ref, xb_ref, acc_ref):
    j = pl.program_id(0)
    f = pl.program_id(1)

    @pl.when(j < nu_ref[0])
    def _():
        @pl.when(f == 0)
        def _():
            xb_ref[...] = _from_slabs(x_ref).astype(bf16)
            acc_ref[...] = jnp.zeros_like(acc_ref)

        x = xb_ref[...]
        t = _silu(_dot(x, wg_ref[...])) * _dot(x, wu_ref[...])
        acc_ref[...] += _dot(t.astype(bf16), wd_ref[...])

        @pl.when(f == pl.num_programs(1) - 1)
        def _():
            _to_slabs(y_ref, acc_ref[...])

    @pl.when(jnp.logical_and(j >= nu_ref[0], f == pl.num_programs(1) - 1))
    def _():
        y_ref[...] = jnp.zeros_like(y_ref)


def _experts(xs, tile_expert, n_used, wg, wu, wd, n_tiles):
    d = wg.shape[1]
    slab = (MOE_TM, d // LANES, LANES)
    ff = wg.shape[2]
    tf = 256 * _largest_divisor(ff // 256, 2)
    nf = ff // tf
    tile = lambda j, nu: jnp.minimum(j, nu[0] - 1)
    chunk = lambda j, f, nu: jnp.where(j < nu[0], f, nf - 1)
    return pl.pallas_call(
        _experts_body,
        grid_spec=pltpu.PrefetchScalarGridSpec(
            num_scalar_prefetch=2,
            grid=(n_tiles, nf),
            in_specs=[pl.BlockSpec(slab, lambda j, f, te, nu: (tile(j, nu), 0, 0)),
                      pl.BlockSpec((None, d, tf), lambda j, f, te, nu: (te[tile(j, nu)], 0, chunk(j, f, nu))),
                      pl.BlockSpec((None, d, tf), lambda j, f, te, nu: (te[tile(j, nu)], 0, chunk(j, f, nu))),
                      pl.BlockSpec((None, tf, d), lambda j, f, te, nu: (te[tile(j, nu)], chunk(j, f, nu), 0))],
            out_specs=pl.BlockSpec(slab, lambda j, f, te, nu: (j, 0, 0)),
            scratch_shapes=[pltpu.VMEM((MOE_TM, d), bf16), pltpu.VMEM((MOE_TM, d), f32)]),
        out_shape=jax.ShapeDtypeStruct((n_tiles * MOE_TM, d // LANES, LANES), f32),
        compiler_params=_cparams("arbitrary", "arbitrary"),
        name="moe_experts",
    )(tile_expert, n_used, xs, wg, wu, wd)


def _combine_body(final, h_ref, route_ref, fnw_ref, dest_ref, ys_ref, out_ref, buf_ref, idx_ref,
                  sem_idx, sem_row):
    i = pl.program_id(0)
    tt = h_ref.shape[0]
    slot = i % 2

    def gather(step, s):
        idx_copy = pltpu.make_async_copy(dest_ref.at[pl.ds(step * 2 * tt, 2 * tt)], idx_ref, sem_idx)
        idx_copy.start()
        idx_copy.wait()

        def issue(r, carry):
            _row_copy(ys_ref, idx_ref[r], buf_ref.at[s, 0], r, sem_row.at[s]).start()
            _row_copy(ys_ref, idx_ref[tt + r], buf_ref.at[s, 1], r, sem_row.at[s]).start()
            return carry

        lax.fori_loop(0, tt, issue, 0, unroll=ROW_DMA_UNROLL)

    @pl.when(i == 0)
    def _():
        gather(0, 0)

    @pl.when(i + 1 < pl.num_programs(0))
    def _():
        gather(i + 1, 1 - slot)

    for k in range(2):
        pltpu.make_async_copy(ys_ref.at[pl.ds(0, tt)], buf_ref.at[slot, k], sem_row.at[slot]).wait()
    route = route_ref[...]
    g1 = route[:, ROUTE_GATE:ROUTE_GATE + 1]
    g2 = route[:, ROUTE_GATE + 1:ROUTE_GATE + 2]
    y = h_ref[...] + g1 * _from_slabs(buf_ref.at[slot, 0]) + g2 * _from_slabs(buf_ref.at[slot, 1])
    out_ref[...] = _rms(y, fnw_ref[...]) if final else y


def _combine(h, route, fnw, dest_flat, ys, tt, final):
    n, d = h.shape
    return pl.pallas_call(
        functools.partial(_combine_body, final),
        grid=(n // tt,),
        in_specs=[pl.BlockSpec((tt, d), lambda i: (i, 0)), pl.BlockSpec((tt, ROUTE_W), lambda i: (i, 0)),
                  pl.BlockSpec((1, d), lambda i: (0, 0)),
                  pl.BlockSpec(memory_space=pl.ANY), pl.BlockSpec(memory_space=pl.ANY)],
        out_specs=pl.BlockSpec((tt, d), lambda i: (i, 0)),
        out_shape=jax.ShapeDtypeStruct((n, d), f32),
        scratch_shapes=[pltpu.VMEM((2, 2, tt, d // LANES, LANES), f32), pltpu.SMEM((2 * tt,), jnp.int32),
                        pltpu.SemaphoreType.DMA(()), pltpu.SemaphoreType.DMA((2,))],
        compiler_params=_cparams("arbitrary"),
        name="moe_combine",
    )(h, route, fnw, dest_flat, ys)


def _moe(mix, h, wo, nw, router, wg, wu, wd, fnw, final):
    n, d = h.shape
    tt = TILE * _largest_divisor(n // TILE, 4)
    h, route, counts = _router(mix, h, wo, nw, router)

    cnt = counts[0, :N_EXPERTS].astype(jnp.int32)
    padded = (cnt + MOE_TM - 1) // MOE_TM * MOE_TM
    ends = jnp.cumsum(padded)
    off = ends - padded
    n_tiles = -(-2 * n // MOE_TM) + N_EXPERTS
    tile_expert = jnp.minimum(
        jnp.sum((jnp.arange(n_tiles)[:, None] * MOE_TM >= ends[None, :]).astype(jnp.int32), axis=1),
        N_EXPERTS - 1).astype(jnp.int32)
    n_used = (ends[-1:] // MOE_TM).astype(jnp.int32)
    sel = route[:, ROUTE_E:ROUTE_E + 2].astype(jnp.int32)
    rank = route[:, ROUTE_RANK:ROUTE_RANK + 2].astype(jnp.int32)
    dest = jnp.sum(jnp.where(sel[..., None] == jnp.arange(N_EXPERTS), off, 0), axis=-1) + rank
    dest_flat = dest.reshape(n // tt, tt, 2).transpose(0, 2, 1).reshape(-1)

    fill_rows = jnp.concatenate([off + cnt, n_used]).astype(jnp.int32)
    xs = _dispatch(h, nw, dest_flat, fill_rows, (n_tiles + 1) * MOE_TM, tt)
    ys = _experts(xs, tile_expert, n_used, wg, wu, wd, n_tiles)
    return _combine(h, route, fnw, dest_flat, ys, tt, final)


def _final_norm_body(h_ref, w_ref, out_ref):
    out_ref[...] = _rms(h_ref[...], w_ref[...])


def _final_norm(h, w):
    n, d = h.shape
    tm = TILE * _largest_divisor(n // TILE, 8)
    return pl.pallas_call(
        _final_norm_body,
        grid=(n // tm,),
        in_specs=[pl.BlockSpec((tm, d), lambda i: (i, 0)), pl.BlockSpec((1, d), lambda i: (0, 0))],
        out_specs=pl.BlockSpec((tm, d), lambda i: (i, 0)),
        out_shape=jax.ShapeDtypeStruct((n, d), f32),
        compiler_params=_cparams("parallel"),
        name="final_norm",
    )(h, w)


def _rope_tables(seq, pad):
    t = jnp.arange(seq)
    meta_pos = jnp.arange(N_META) - N_META
    row = jnp.concatenate([meta_pos, t // GRID_W]).astype(f32)
    col = jnp.concatenate([meta_pos, t % GRID_W]).astype(f32)
    half = HEAD_DIM // 2
    inv = ROPE_THETA ** (-jnp.arange(0, half, 2, dtype=f32) / half)
    ang = jnp.concatenate([row[:, None] * inv, col[:, None] * inv], axis=-1)
    ang = jnp.tile(jnp.repeat(ang, 2, axis=-1), (1, QK_WIDTH // HEAD_DIM))
    ang = jnp.pad(ang, ((pad, 0), (0, 0)))
    return jnp.cos(ang), jnp.sin(ang)


def _pair_swap_matrix(width):
    i = jnp.arange(width)
    p = jnp.zeros((width, width), f32)
    p = p.at[i[1::2], i[0::2]].set(-1.0)
    p = p.at[i[0::2], i[1::2]].set(1.0)
    return p.astype(bf16)


def _group_mean_matrix(width, group):
    i = jnp.arange(width)
    return ((i[:, None] // group == i[None, :] // group).astype(f32) / group).astype(bf16)


def _t5_bucket(rel):
    nb = REL_BUCKETS // 2
    max_exact = nb // 2
    ret = (rel > 0).astype(jnp.int32) * nb
    n = jnp.abs(rel)
    nf = jnp.maximum(n, 1).astype(f32)
    large = max_exact + (jnp.log(nf / max_exact) / math.log(REL_MAX_DIST / max_exact)
                         * (nb - max_exact)).astype(jnp.int32)
    large = jnp.minimum(large, nb - 1)
    return ret + jnp.where(n < max_exact, n, large)


def _swa_bias_tables(rel_bias, lp):
    def lookup(bucket):
        out = jnp.zeros((rel_bias.shape[1],) + bucket.shape, f32)
        for b in range(REL_BUCKETS):
            out = jnp.where((bucket == b)[None], rel_bias[b].astype(f32)[:, None, None], out)
        return out

    qi = jnp.arange(TILE)
    ki = jnp.arange(3 * TILE)
    rel = ki[None, :] - TILE - qi[:, None]
    band = jnp.where((jnp.abs(rel) <= WINDOW)[None], lookup(_t5_bucket(rel)), NEG)
    pos = jnp.arange(lp) - (TILE - N_META)
    rel_m = jnp.arange(N_META)[None, :] - pos[:, None]
    meta = lookup(_t5_bucket(rel_m))
    nt = lp // TILE
    band_t = jnp.transpose(band, (2, 0, 1)).reshape(3 * TILE, 4 * TILE) * LOG2E
    meta_t = (meta.reshape(4, nt, TILE, N_META).transpose(1, 3, 0, 2).reshape(nt, N_META, 4 * TILE)
              * LOG2E)
    return band_t, meta_t


def _row(v, width=None):
    v = v.astype(f32).reshape(1, -1)
    if width is not None and v.shape[1] < width:
        v = jnp.pad(v, ((0, 0), (0, width - v.shape[1])))
    return v


def kernel(x, meta_tokens, rel_bias, norm_mix_w, norm_ffn_w, w_in, ssd_conv_w, ssd_conv_b, ssd_dt_bias, ssd_a_log, ssd_d, ssd_norm_w, gla_gate_w2, gla_gate_b, gla_norm_w, swa_sink, gqa_q_norm_w, gqa_k_norm_w, w_out, ffn_w_gate, ffn_w_up, ffn_w_down, moe_router, moe_w_gate, moe_w_up, moe_w_down, final_norm_w):
    bsz, seq, d = x.shape
    depth = w_in.shape[0]
    pad = (-(seq + N_META)) % TILE
    assert pad == TILE - N_META and seq % TILE == 0
    lp = pad + N_META + seq
    n = bsz * lp

    meta = jnp.broadcast_to(meta_tokens[None].astype(x.dtype), (bsz, N_META, d))
    h = jnp.concatenate([jnp.zeros((bsz, pad, d), x.dtype), meta, x], axis=1).reshape(n, d)

    cos, sin = _rope_tables(seq, pad)
    rot = _pair_swap_matrix(QK_WIDTH)
    gavg = _group_mean_matrix(QK_WIDTH, HEAD_DIM)
    bias_band, bias_meta = _swa_bias_tables(rel_bias, lp)
    offs = [0]
    for s in IN_SIZES:
        offs.append(offs[-1] + s)

    for i in range(depth):
        wi = w_in[i]
        cols = [wi[:, offs[j]:offs[j + 1]] for j in PACK_ORDER]
        w_pack = jnp.concatenate(cols + [jnp.zeros((d, PACK_WIDTH - offs[-1]), wi.dtype)], axis=1).astype(bf16)
        o_ssd, o_small, o_gla, sk, ak, aq, av, sq, sv = _inproj(
            h, _row(norm_mix_w[i]), w_pack, cos, sin, rot, gavg,
            _row(jnp.concatenate([jnp.tile(gqa_q_norm_w[i], 4), jnp.tile(gqa_k_norm_w[i], 2)])), bsz, lp)

        convw = jnp.pad(ssd_conv_w[i].astype(f32), ((0, 8 - SSD_CONV), (0, 0)))
        convb = _row(ssd_conv_b[i])
        dtb = _row(ssd_dt_bias[i].reshape(-1), 128)
        alog = _row(ssd_a_log[i].reshape(-1), 128)
        yf, xbc = _ssd(False, o_ssd, o_small, convw, convb, dtb, alog,
                       _row(jnp.repeat(ssd_d[i], SSD_HEAD_DIM)), bsz, lp)
        y_ssd = _ssd(True, o_ssd, o_small, convw, convb, dtb, alog, (yf, xbc, _row(ssd_norm_w[i])), bsz, lp)

        def gate_w(direction):
            lo = SMALL_GA + GLA_GATE_RANK * direction
            full = jnp.zeros((128, GLA_KEY), f32).at[lo:lo + GLA_GATE_RANK].set(gla_gate_w2[i, direction].astype(f32))
            return full.astype(bf16)

        of = _gla(False, o_gla, o_small, gate_w(0), _row(gla_gate_b[i, 0]), None, bsz, lp)
        y_gla = _gla(True, o_gla, o_small, gate_w(1), _row(gla_gate_b[i, 1]),
                     (of, _row(jnp.tile(gla_norm_w[i], GLA_HEADS)), gavg), bsz, lp)

        sink = _row(jnp.repeat(swa_sink[i].astype(f32), TILE)) * LOG2E
        y_swa = _swa(sq, sk, sv, bias_band, bias_meta, sink, bsz, lp)
        y_g2 = _flash(aq, ak, av, bsz, lp)

        mix = (y_ssd, y_gla, y_swa, y_g2)
        wo = w_out[i].astype(bf16)
        j = i // 2
        if i % 2 == 0:
            h = _ffn(mix, h, wo, _row(norm_ffn_w[i]), ffn_w_gate[j].astype(bf16), ffn_w_up[j].astype(bf16),
                     ffn_w_down[j].astype(bf16))
            if i == depth - 1:
                h = _final_norm(h, _row(final_norm_w))
        else:
            router = jnp.pad(moe_router[j].astype(f32), ((0, 0), (0, 128 - N_EXPERTS)))
            h = _moe(mix, h, wo, _row(norm_ffn_w[i]), router, moe_w_gate[j].astype(bf16),
                     moe_w_up[j].astype(bf16), moe_w_down[j].astype(bf16), _row(final_norm_w), i == depth - 1)
    return h.reshape(bsz, lp, d)[:, pad + N_META:]
```

```python
import functools
import math

import jax
import jax.numpy as jnp
import numpy as np
from jax import lax
from jax.experimental import pallas as pl
from jax.experimental.pallas import tpu as pltpu

f32 = jnp.float32
bf16 = jnp.bfloat16

N_META = 16
HEAD_DIM = 64
GRID_W = 64
EPS = 1e-6
ROPE_THETA = 10000.0
TILE = 128
SSD_HEADS = 4
SSD_HEAD_DIM = 64
SSD_INNER = 256
SSD_STATE = 128
SSD_CONV = 5
SSD_CONV_DIM = 768
GLA_HEADS = 4
GLA_DK = 32
GLA_DV = 64
GLA_KEY = 128
GLA_VAL = 256
GLA_GATE_RANK = 16
GLA_GATE_NORM = 16.0
GLA_CHUNK = 64
WINDOW = 128
REL_BUCKETS = 32
REL_MAX_DIST = 128
N_EXPERTS = 8
NEG = -1e30
LOG2E = math.log2(math.e)
QK_WIDTH = 384
CONV_HALO = 16
VMEM_LIMIT = 56 * 1024 * 1024

IN_SIZES = (256, 768, 8, 128, 128, 256, 256, 32, 256, 128, 128, 256, 128, 128)
(_Z, _XBC, _DT, _GQ, _GK, _GV, _GR, _GA, _SQ, _SK, _SV, _AQ, _AK, _AV) = range(14)
PACK_ORDER = (_XBC, _Z, _GQ, _GK, _GV, _GR, _SQ, _SK, _SV, _AQ, _AK, _AV, _DT, _GA)
PACK_WIDTH = 2944
C_SSD, C_GLA, C_SWA, C_AQ, C_AK, C_AV, C_SMALL = 0, 1024, 1792, 2304, 2560, 2688, 2816
SMALL_DT, SMALL_GA = 0, 8


def _cparams(*sem):
    return pltpu.CompilerParams(dimension_semantics=sem, vmem_limit_bytes=VMEM_LIMIT)


def _dot(a, b):
    return jnp.dot(a, b, preferred_element_type=f32)


def _dot_nt(a, b):
    return lax.dot_general(a, b, (((1,), (1,)), ((), ())), preferred_element_type=f32)


def _dot_tn(a, b):
    return lax.dot_general(a, b, (((0,), (0,)), ((), ())), preferred_element_type=f32)


def _split(a):
    hi = a.astype(bf16)
    lo = (a - hi.astype(f32)).astype(bf16)
    return hi, lo


def _dot_split_lhs(a, b):
    hi, lo = _split(a)
    return _dot(hi, b) + _dot(lo, b)


def _dot_split_rhs(t, x):
    hi, lo = _split(x)
    return _dot(t, hi) + _dot(t, lo)


def _rms(x, w):
    return x * lax.rsqrt(jnp.mean(x * x, axis=-1, keepdims=True) + EPS) * w


def _silu(x):
    return x * (0.5 + 0.5 * jnp.tanh(0.5 * x))


def _softplus(x):
    return jnp.maximum(x, 0.0) + jnp.log(1.0 + jnp.exp(-jnp.abs(x)))


def _log_sigmoid(x):
    return jnp.minimum(x, 0.0) - jnp.log(1.0 + jnp.exp(-jnp.abs(x)))


def _tri(n, rev):
    r = lax.broadcasted_iota(jnp.int32, (n, n), 0)
    c = lax.broadcasted_iota(jnp.int32, (n, n), 1)
    return (r <= c) if rev else (r >= c)


def _valid_rows(tile, n):
    rows = lax.broadcasted_iota(jnp.int32, (n, 1), 0)
    return jnp.logical_or(tile > 0, rows >= TILE - N_META)


def _inproj_body(h_ref, nw_ref, w_ref, cos_ref, sin_ref, rot_ref, gavg_ref, qkw_ref,
                 ssd_ref, small_ref, gla_ref, sk_ref, k_ref, q_ref, v_ref, sq_ref, sv_ref):
    u = _rms(h_ref[...], nw_ref[...]).astype(bf16)

    def mm(lo, hi):
        return _dot(u, w_ref[:, lo:hi])

    ssd_ref[...] = mm(C_SSD, C_GLA).astype(bf16)
    gla_ref[...] = mm(C_GLA, C_SWA).astype(bf16)
    small_ref[...] = mm(C_SMALL, PACK_WIDTH)
    akv = mm(C_AK, C_SMALL)
    v_ref[...] = akv[:, 128:].T.astype(bf16)

    skv = mm(C_SWA + 256, C_AQ)
    sk_ref[...] = skv[:, :128].astype(bf16)
    sv_ref[...] = skv[:, 128:].T.astype(bf16)
    sqt = (mm(C_SWA, C_SWA + 256) * (HEAD_DIM ** -0.5 * LOG2E)).T.astype(bf16)
    sq_ref[...] = jnp.zeros_like(sq_ref)
    for t in range(sq_ref.shape[0]):
        for hh in range(4):
            lo = HEAD_DIM * (hh // 2)
            sq_ref[t, lo:lo + HEAD_DIM, TILE * hh:TILE * (hh + 1)] = (
                sqt[HEAD_DIM * hh:HEAD_DIM * (hh + 1), TILE * t:TILE * (t + 1)])

    t = jnp.concatenate([mm(C_AQ, C_AK), akv[:, :128]], axis=1)
    ms = _dot_split_lhs(t * t, gavg_ref[...])
    tn = t * lax.rsqrt(ms + EPS) * qkw_ref[...]
    tr = _dot(tn.astype(bf16), rot_ref[...])
    qk = tn * cos_ref[...] + tr * sin_ref[...]
    q_ref[...] = (qk[:, :256] * (HEAD_DIM ** -0.5 * LOG2E)).T.astype(bf16)
    k_ref[...] = qk[:, 256:].astype(bf16)


def _inproj(h, nw, w, cos, sin, rot, gavg, qkw, bsz, lp):
    n = h.shape[0]
    d = h.shape[1]
    tm = TILE * _largest_divisor(lp // TILE, 3)
    per = lp // tm
    row = lambda b, i: (b * per + i, 0)
    const = lambda b, i: (0, 0)
    tab = lambda b, i: (i, 0)
    outs = ((1024, bf16), (128, f32), (768, bf16), (128, bf16), (128, bf16))
    slab = lambda b, i: (b * per + i, 0, 0)
    sub = tm // TILE
    return pl.pallas_call(
        _inproj_body,
        grid=(bsz, per),
        in_specs=[pl.BlockSpec((tm, d), row), pl.BlockSpec((1, d), const),
                  pl.BlockSpec((d, PACK_WIDTH), const),
                  pl.BlockSpec((tm, QK_WIDTH), tab), pl.BlockSpec((tm, QK_WIDTH), tab),
                  pl.BlockSpec((QK_WIDTH, QK_WIDTH), const), pl.BlockSpec((QK_WIDTH, QK_WIDTH), const),
                  pl.BlockSpec((1, QK_WIDTH), const)],
        out_specs=([pl.BlockSpec((tm, c), row) for c, _ in outs]
                   + [pl.BlockSpec((None, 256, tm), slab), pl.BlockSpec((None, 128, tm), slab),
                      pl.BlockSpec((sub, 128, 4 * TILE), slab), pl.BlockSpec((None, 128, tm), slab)]),
        out_shape=([jax.ShapeDtypeStruct((n, c), t) for c, t in outs]
                   + [jax.ShapeDtypeStruct((n // tm, 256, tm), bf16),
                      jax.ShapeDtypeStruct((n // tm, 128, tm), bf16),
                      jax.ShapeDtypeStruct((n // TILE, 128, 4 * TILE), bf16),
                      jax.ShapeDtypeStruct((n // tm, 128, tm), bf16)]),
        compiler_params=_cparams("parallel", "parallel"),
        name="inproj",
    )(h, nw, w, cos, sin, rot, gavg, qkw)


def _largest_divisor(n, cap):
    return max(k for k in range(1, cap + 1) if n % k == 0)


def _for_each_batch(chain, batched, n_scratch):
    def body(*refs):
        nb = next(r.shape[0] for r, flag in zip(refs, batched) if flag)
        io, scratch = refs[:len(batched)], refs[len(batched):]
        assert len(scratch) == n_scratch * nb
        for b in range(nb):
            chain(*[r.at[b] if flag else r for r, flag in zip(io, batched)],
                  *[scratch[k * nb + b] for k in range(n_scratch)])
    return body


def _per_batch_scratch(nb, *shapes):
    return [pltpu.VMEM(shape, dtype) for shape, dtype in shapes for _ in range(nb)]


def _ssd_scan_tile(rev, xs, bmat, cmat, dt, cum, tot, st):
    col0 = SMALL_DT + (SSD_HEADS if rev else 0)
    causal = _tri(TILE, rev)
    cum_t = cum.T
    e_cum = jnp.exp(cum)
    e_tot = jnp.exp(tot)
    head_of = lax.broadcasted_iota(jnp.int32, (TILE, SSD_INNER), 1) // SSD_HEAD_DIM
    head_of_row = lax.broadcasted_iota(jnp.int32, (1, SSD_INNER), 1) // SSD_HEAD_DIM

    def widen(cols, like):
        out = jnp.zeros(like.shape, f32)
        for hh in range(SSD_HEADS):
            out = jnp.where(like == hh, cols[:, col0 + hh:col0 + hh + 1], out)
        return out

    def own_head(r):
        out = jnp.zeros((TILE, SSD_INNER), f32)
        for hh in range(SSD_HEADS):
            out = jnp.where(head_of == hh, r[TILE * hh:TILE * (hh + 1)], out)
        return out

    xd = (xs * widen(dt, head_of)).astype(bf16)
    scores, bws = [], []
    for g in range(2):
        bg = bmat[:, SSD_STATE * g:SSD_STATE * (g + 1)]
        cbg = _dot_nt(cmat[:, SSD_STATE * g:SSD_STATE * (g + 1)], bg)
        for hh in (2 * g, 2 * g + 1):
            col = col0 + hh
            a_col = cum[:, col:col + 1]
            decay = jnp.where(causal, jnp.exp(a_col - cum_t[col:col + 1, :]), 0.0)
            scores.append((cbg * decay).astype(bf16))
            bws.append((bg.astype(f32) * jnp.exp(tot[:, col:col + 1] - a_col)).astype(bf16))
    y = own_head(_dot(jnp.concatenate(scores, axis=0), xd))
    y = y + _dot(cmat, st.astype(bf16)) * widen(e_cum, head_of)
    upd = _dot_tn(jnp.concatenate(bws, axis=1), xd)
    rows = [jnp.where(head_of == 2 * g, upd[SSD_STATE * 2 * g:SSD_STATE * (2 * g + 1)],
                      jnp.where(head_of == 2 * g + 1,
                                upd[SSD_STATE * (2 * g + 1):SSD_STATE * (2 * g + 2)], 0.0))
            for g in range(2)]
    return y, st * widen(e_tot, head_of_row) + jnp.concatenate(rows, axis=0)


def _ssd_body(rev, nt, *refs):
    if rev:
        xbc_ref, small_ref, dtb_ref, alog_ref, yf_ref, z_ref, normw_ref, out_ref, state_ref = refs
    else:
        (cur_ref, prev_ref, next_ref, small_ref, convw_ref, convb_ref, dtb_ref, alog_ref, dskip_ref,
         out_ref, xbc_out_ref, state_ref) = refs[:12]
        ext_refs = refs[12:]
    c = pl.program_id(0)
    tile = (nt - 1 - c) if rev else c
    nb = small_ref.shape[0]
    valid = _valid_rows(tile, TILE)

    a = -jnp.exp(alog_ref[...])
    dts = [jnp.where(valid, _softplus(small_ref[b] + dtb_ref[...]), 0.0) for b in range(nb)]
    cum_all = _dot_split_rhs(_tri(TILE, rev).astype(bf16),
                             jnp.concatenate([dt * a for dt in dts], axis=1))
    for b in range(nb):
        dt = dts[b]
        if rev:
            xbc = xbc_ref[b]
            xs = xbc[:, :SSD_INNER].astype(f32)
            bc = xbc[:, SSD_INNER:]
        else:
            ext_ref = ext_refs[b]
            ext_ref[0:CONV_HALO, :] = jnp.where(tile > 0, prev_ref[b].astype(f32), 0.0)
            ext_ref[CONV_HALO:CONV_HALO + TILE, :] = jnp.where(valid, cur_ref[b].astype(f32), 0.0)
            ext_ref[CONV_HALO + TILE:, :] = jnp.where(tile < nt - 1, next_ref[b].astype(f32), 0.0)
            first = CONV_HALO - (SSD_CONV - 1) // 2
            strips = []
            for lo in range(0, SSD_CONV_DIM, 128):
                cols = slice(lo, lo + 128)
                acc = jnp.zeros((TILE, 128), f32) + convb_ref[:, cols]
                for k in range(SSD_CONV):
                    acc = acc + convw_ref[k:k + 1, cols] * ext_ref[first + k:first + k + TILE, cols]
                strips.append(jnp.where(valid, _silu(acc), 0.0))
            xbc = jnp.concatenate(strips, axis=1)
            xbc_out_ref[b] = xbc.astype(bf16)
            xs = xbc[:, :SSD_INNER]
            bc = xbc[:, SSD_INNER:].astype(bf16)
        st = jnp.where(c == 0, 0.0, state_ref[b])
        y, st = _ssd_scan_tile(rev, xs, bc[:, :2 * SSD_STATE], bc[:, 2 * SSD_STATE:], dt,
                               cum_all[:, 128 * b:128 * (b + 1)],
                               jnp.sum(dt * a, axis=0, keepdims=True), st)
        state_ref[b] = st
        if rev:
            y = (yf_ref[b] + y) * _silu(z_ref[b].astype(f32))
            out_ref[b] = jnp.where(valid, _rms(y, normw_ref[...]), 0.0).astype(out_ref.dtype)
        else:
            out_ref[b] = y + dskip_ref[...] * xs


def _ssd(rev, o_ssd, o_small, convw, convb, dtb, alog, extra, bsz, lp):
    n = o_ssd.shape[0]
    nt = lp // TILE
    hb = TILE // CONV_HALO
    o_ssd = o_ssd.reshape(bsz, lp, -1)
    o_small = o_small.reshape(bsz, lp, -1)

    def tile_of(c):
        return (nt - 1 - c) if rev else c

    cur = lambda c: (0, tile_of(c), 0)
    prev = lambda c: (0, jnp.maximum(tile_of(c) * hb - 1, 0), 0)
    nxt = lambda c: (0, jnp.minimum((tile_of(c) + 1) * hb, lp // CONV_HALO - 1), 0)
    zcol = lambda c: (0, tile_of(c), SSD_CONV_DIM // SSD_INNER)
    const = lambda c: (0, 0)
    tile3 = lambda width: pl.BlockSpec((bsz, TILE, width), cur)
    state = pltpu.VMEM((bsz, 2 * SSD_STATE, SSD_INNER), f32)
    if rev:
        yf, xbc, normw = extra
        out = pl.pallas_call(
            functools.partial(_ssd_body, rev, nt),
            grid=(nt,),
            in_specs=[tile3(SSD_CONV_DIM), tile3(128), pl.BlockSpec((1, 128), const),
                      pl.BlockSpec((1, 128), const), tile3(SSD_INNER),
                      pl.BlockSpec((bsz, TILE, SSD_INNER), zcol), pl.BlockSpec((1, SSD_INNER), const)],
            out_specs=tile3(SSD_INNER),
            out_shape=jax.ShapeDtypeStruct((bsz, lp, SSD_INNER), bf16),
            scratch_shapes=[state],
            compiler_params=_cparams("arbitrary"),
            name="ssd_rev",
        )(xbc, o_small, dtb, alog, yf, o_ssd, normw)
        return out.reshape(n, SSD_INNER)
    halo = lambda im: pl.BlockSpec((bsz, CONV_HALO, SSD_CONV_DIM), im)
    return pl.pallas_call(
        functools.partial(_ssd_body, rev, nt),
        grid=(nt,),
        in_specs=[tile3(SSD_CONV_DIM), halo(prev), halo(nxt), tile3(128),
                  pl.BlockSpec((8, SSD_CONV_DIM), const), pl.BlockSpec((1, SSD_CONV_DIM), const),
                  pl.BlockSpec((1, 128), const), pl.BlockSpec((1, 128), const),
                  pl.BlockSpec((1, SSD_INNER), const)],
        out_specs=[tile3(SSD_INNER), tile3(SSD_CONV_DIM)],
        out_shape=[jax.ShapeDtypeStruct((bsz, lp, SSD_INNER), f32),
                   jax.ShapeDtypeStruct((bsz, lp, SSD_CONV_DIM), bf16)],
        scratch_shapes=[state] + _per_batch_scratch(bsz, ((TILE + 2 * CONV_HALO, SSD_CONV_DIM), f32)),
        compiler_params=_cparams("arbitrary"),
        name="ssd_fwd",
    )(o_ssd, o_ssd, o_ssd, o_small, convw, convb, dtb, alog, extra)


def _gla_body(rev, nt, *refs):
    if rev:
        x_ref, small_ref, wg_ref, gb_ref, of_ref, normw_ref, gavg_ref, out_ref, st_ref = refs
    else:
        x_ref, small_ref, wg_ref, gb_ref, out_ref, st_ref = refs
    c = pl.program_id(0)
    tile = (nt - 1 - c) if rev else c
    nb = x_ref.shape[0]
    n = GLA_CHUNK
    valid = _valid_rows(tile, TILE)

    pre = _dot(small_ref[...].reshape(nb * TILE, 128).astype(bf16), wg_ref[...]) + gb_ref[...]
    g_all = _log_sigmoid(pre) / GLA_GATE_NORM
    gs = [jnp.where(valid, g_all[TILE * b:TILE * (b + 1)], 0.0) for b in range(nb)]
    row = lax.broadcasted_iota(jnp.int32, (TILE, TILE), 0)
    col = lax.broadcasted_iota(jnp.int32, (TILE, TILE), 1)
    same_chunk = (row // n) == (col // n)
    cum_mat = jnp.where(jnp.logical_and(same_chunk, (row <= col) if rev else (row >= col)), 1.0, 0.0)
    bc_all = _dot_split_rhs(cum_mat.astype(bf16), jnp.concatenate(gs, axis=1))

    first = slice(n, 2 * n) if rev else slice(0, n)
    second = slice(0, n) if rev else slice(n, 2 * n)
    rows = lax.broadcasted_iota(jnp.int32, (TILE, 1), 0)
    in_first = (rows >= n) if rev else (rows < n)
    tri = _tri(n, rev)
    tri4 = jnp.concatenate([tri] * GLA_HEADS, axis=0)
    lane_head = lax.broadcasted_iota(jnp.int32, (n, GLA_KEY), 1) // GLA_DK
    out_head = lax.broadcasted_iota(jnp.int32, (n, GLA_VAL), 1) // GLA_DV
    blockdiag = (lax.broadcasted_iota(jnp.int32, (GLA_VAL, GLA_KEY), 0) // GLA_DV
                 == lax.broadcasted_iota(jnp.int32, (GLA_VAL, GLA_KEY), 1) // GLA_DK)

    def stack_heads(a):
        return jnp.concatenate([jnp.where(lane_head == hh, a, 0.0) for hh in range(GLA_HEADS)],
                               axis=0).astype(bf16)

    def own_head(r):
        out = jnp.zeros((n, GLA_VAL), f32)
        for hh in range(GLA_HEADS):
            out = jnp.where(out_head == hh, r[n * hh:n * (hh + 1)], out)
        return out

    both = []
    for b in range(nb):
        x = x_ref[b].astype(f32)
        g = gs[b]
        bc = bc_all[:, 128 * b:128 * (b + 1)]
        q = jnp.where(valid, x[:, :GLA_KEY], 0.0) * GLA_DK ** -0.5
        k = jnp.where(valid, x[:, GLA_KEY:2 * GLA_KEY], 0.0)
        v = jnp.where(valid, x[:, 2 * GLA_KEY:2 * GLA_KEY + GLA_VAL], 0.0).astype(bf16)
        bl_first = jnp.sum(g[first], axis=0, keepdims=True)
        bl_second = jnp.sum(g[second], axis=0, keepdims=True)
        qt = q * jnp.exp(bc)
        kt = k * jnp.exp(-bc)
        kw = k * jnp.exp(jnp.where(in_first, bl_first, bl_second) - bc)

        att_f = jnp.where(tri4, _dot_nt(stack_heads(qt[first]), kt[first].astype(bf16)), 0.0)
        o_f = own_head(_dot(att_f.astype(bf16), v[first]))
        keys = jnp.concatenate([kw[first], kt[second]], axis=0).astype(bf16)
        vals = jnp.concatenate([v[first], v[second]], axis=0)
        att_s = _dot_nt(stack_heads(qt[second]), keys)
        att_s = jnp.concatenate([att_s[:, :n], jnp.where(tri4, att_s[:, n:], 0.0)], axis=1)
        o_s = own_head(_dot(att_s.astype(bf16), vals))

        st = jnp.where(c == 0, 0.0, st_ref[b])
        q_in = qt * jnp.exp(jnp.where(in_first, 0.0, bl_first))
        o = _dot_nt(q_in.astype(bf16), st.astype(bf16)) + jnp.concatenate(
            [o_s, o_f] if rev else [o_f, o_s], axis=0)
        k_out = (kw * jnp.exp(jnp.where(in_first, bl_second, 0.0))).astype(bf16)
        st_ref[b] = st * jnp.exp(bl_first + bl_second) + jnp.where(blockdiag, _dot_tn(v, k_out), 0.0)

        if rev:
            both.append(of_ref[b] + o)
        else:
            out_ref[b] = o

    if rev:
        o = jnp.concatenate(both, axis=0)
        ms = _dot_split_lhs(o * o, gavg_ref[...])
        on = o * lax.rsqrt(ms + EPS) * normw_ref[...]
        for b in range(nb):
            r = x_ref[b, :, 2 * GLA_KEY + GLA_VAL:].astype(f32)
            out_ref[b] = jnp.where(valid, on[TILE * b:TILE * (b + 1)] * _silu(r), 0.0).astype(out_ref.dtype)


def _gla(rev, o_gla, o_small, wg, gb, extra, bsz, lp):
    n = o_gla.shape[0]
    nt = lp // TILE
    cur = lambda c: (0, (nt - 1 - c) if rev else c, 0)
    const = lambda c: (0, 0)
    in_specs = [pl.BlockSpec((bsz, TILE, 768), cur), pl.BlockSpec((bsz, TILE, 128), cur),
                pl.BlockSpec((128, GLA_KEY), const), pl.BlockSpec((1, GLA_KEY), const)]
    args = [o_gla.reshape(bsz, lp, -1), o_small.reshape(bsz, lp, -1), wg, gb]
    if rev:
        of, normw, gavg = extra
        in_specs += [pl.BlockSpec((bsz, TILE, GLA_VAL), cur), pl.BlockSpec((1, GLA_VAL), const),
                     pl.BlockSpec((GLA_VAL, GLA_VAL), const)]
        args += [of.reshape(bsz, lp, -1), normw, gavg]
    out = pl.pallas_call(
        functools.partial(_gla_body, rev, nt),
        grid=(nt,),
        in_specs=in_specs,
        out_specs=pl.BlockSpec((bsz, TILE, GLA_VAL), cur),
        out_shape=jax.ShapeDtypeStruct((bsz, lp, GLA_VAL), bf16 if rev else f32),
        scratch_shapes=[pltpu.VMEM((bsz, GLA_VAL, GLA_KEY), f32)],
        compiler_params=_cparams("arbitrary"),
        name="gla_rev" if rev else "gla_fwd",
    )(*args)
    return out.reshape(n, GLA_VAL)


def _swa_body(nt, qp_ref, kp_ref, kc_ref, kn_ref, km_ref, vp_ref, vc_ref, vn_ref, vm_ref,
              bias_ref, bmeta_ref, sink_ref, out_ref):
    c = pl.program_id(0)
    krow = lax.broadcasted_iota(jnp.int32, (3 * TILE, 1), 0)
    pen = jnp.where(krow < TILE, jnp.where(c >= 2, 0.0, NEG),
                    jnp.where(krow < 2 * TILE, jnp.where(c >= 1, 0.0, NEG),
                              jnp.where(c <= nt - 2, 0.0, NEG)))
    qp = qp_ref[...]
    kcat = jnp.concatenate([kp_ref[...], kc_ref[...], kn_ref[...]], axis=0)
    s = _dot(kcat, qp) + bias_ref[...] + pen
    sm = _dot(km_ref[...], qp) + bmeta_ref[...]
    sk = sink_ref[...]
    m = jnp.maximum(jnp.maximum(jnp.max(s, axis=0, keepdims=True),
                                jnp.max(sm, axis=0, keepdims=True)), sk)
    p = jnp.exp2(s - m)
    pm = jnp.exp2(sm - m)
    inv = 1.0 / (jnp.sum(p, axis=0, keepdims=True) + jnp.sum(pm, axis=0, keepdims=True)
                 + jnp.exp2(sk - m))
    pb = p.astype(bf16)
    pmb = jnp.concatenate([jnp.zeros((TILE - N_META, 4 * TILE), bf16), pm.astype(bf16)], axis=0)
    vcat = jnp.concatenate([vp_ref[...], vc_ref[...], vn_ref[...]], axis=1)
    heads = []
    for g in range(2):
        rows = slice(HEAD_DIM * g, HEAD_DIM * (g + 1))
        cols = slice(2 * TILE * g, 2 * TILE * (g + 1))
        pv = (_dot(vcat[rows, :], pb[:, cols]) + _dot(vm_ref[rows, :], pmb[:, cols])) * inv[:, cols]
        heads += [pv[:, :TILE], pv[:, TILE:]]
    o = jnp.concatenate(heads, axis=0).T
    out_ref[...] = jnp.where(_valid_rows(c, TILE), o, 0.0).astype(out_ref.dtype)


def _swa(qp, k, vt, bias_band, bias_meta, sink, bsz, lp):
    n = k.shape[0]
    nt = lp // TILE
    per, _, tm = vt.shape[0] // bsz, vt.shape[1], vt.shape[2]
    sub = tm // TILE
    qp = qp.reshape(bsz, nt, 128, 4 * TILE)
    k = k.reshape(bsz, lp, 128)
    vt = vt.reshape(bsz, per, 128, tm)
    prev = lambda c: jnp.maximum(c - 1, 0)
    nxt = lambda c: jnp.minimum(c + 1, nt - 1)
    kspec = lambda tile: pl.BlockSpec((bsz, TILE, 128), lambda c: (0, tile(c), 0))
    vspec = lambda tile: pl.BlockSpec((bsz, None, 128, TILE), lambda c: (0, tile(c) // sub, 0, tile(c) % sub))
    same = lambda c: c
    first = lambda c: 0
    out = pl.pallas_call(
        _for_each_batch(functools.partial(_swa_body, nt), [True] * 9 + [False] * 3 + [True], 0),
        grid=(nt,),
        in_specs=[pl.BlockSpec((bsz, None, 128, 4 * TILE), lambda c: (0, c, 0, 0)),
                  kspec(prev), kspec(same), kspec(nxt),
                  pl.BlockSpec((bsz, N_META, 128), lambda c: (0, TILE // N_META - 1, 0)),
                  vspec(prev), vspec(same), vspec(nxt), vspec(first),
                  pl.BlockSpec((3 * TILE, 4 * TILE), lambda c: (0, 0)),
                  pl.BlockSpec((None, N_META, 4 * TILE), lambda c: (c, 0, 0)),
                  pl.BlockSpec((1, 4 * TILE), lambda c: (0, 0))],
        out_specs=pl.BlockSpec((bsz, TILE, 256), lambda c: (0, c, 0)),
        out_shape=jax.ShapeDtypeStruct((bsz, lp, 256), bf16),
        compiler_params=_cparams("arbitrary"),
        name="swa",
    )(qp, k, k, k, k, vt, vt, vt, vt, bias_band, bias_meta, sink)
    return out.reshape(n, 256)


def _flash_body(nk, tq, tk, qt_ref, k_ref, vt_ref, out_ref, qpad_ref, m_ref, l_ref, acc_ref,
                sa_ref, sb_ref, ca_ref, cb_ref):
    i = pl.program_id(1)
    buf_a, buf_b = (sa_ref, ca_ref), (sb_ref, cb_ref)
    krow = lax.broadcasted_iota(jnp.int32, (tk, 1), 0)
    qpad_ref[...] = jnp.zeros_like(qpad_ref)
    for hh in range(4):
        lo = HEAD_DIM * (hh // 2)
        qpad_ref[lo:lo + HEAD_DIM, tq * hh:tq * (hh + 1)] = qt_ref[HEAD_DIM * hh:HEAD_DIM * (hh + 1), :]
    m_ref[...] = jnp.full_like(m_ref, NEG)
    l_ref[...] = jnp.zeros_like(l_ref)
    acc_ref[...] = jnp.zeros_like(acc_ref)

    def scores(j, buf, first=False):
        s_ref, cmax_ref = buf
        s = _dot(k_ref[pl.ds(pl.multiple_of(j * tk, tk), tk), :], qpad_ref[...])
        if first:
            s = jnp.where(krow >= TILE - N_META, s, NEG)
        s_ref[...] = s
        cmax_ref[...] = jnp.max(s, axis=0, keepdims=True)

    def absorb(j, buf):
        s_ref, cmax_ref = buf
        s = s_ref[...]
        m_old = m_ref[...]
        m_new = jnp.maximum(m_old, cmax_ref[...])
        alpha = jnp.exp2(m_old - m_new)
        p = jnp.exp2(s - m_new)
        l_ref[...] = alpha * l_ref[...] + jnp.sum(p, axis=0, keepdims=True)
        m_ref[...] = m_new
        pb = p.astype(bf16)
        for g in range(2):
            pv = _dot(vt_ref[j, HEAD_DIM * g:HEAD_DIM * (g + 1), :], pb[:, 2 * g * tq:(2 * g + 2) * tq])
            for r in range(2):
                hh = 2 * g + r
                rows = slice(HEAD_DIM * hh, HEAD_DIM * (hh + 1))
                acc_ref[rows, :] = (alpha[:, tq * hh:tq * (hh + 1)] * acc_ref[rows, :]
                                    + pv[:, tq * r:tq * (r + 1)])

    scores(0, buf_a, first=True)

    def body(jj, carry):
        j = 2 * jj
        scores(j + 1, buf_b)
        absorb(j, buf_a)
        scores(j + 2, buf_a)
        absorb(j + 1, buf_b)
        return carry

    lax.fori_loop(0, (nk - 1) // 2, body, 0, unroll=True)
    if nk % 2 == 1:
        absorb(nk - 1, buf_a)
    else:
        scores(nk - 1, buf_b)
        absorb(nk - 2, buf_a)
        absorb(nk - 1, buf_b)
    linv = 1.0 / l_ref[...]
    for hh in range(4):
        rows = slice(HEAD_DIM * hh, HEAD_DIM * (hh + 1))
        acc_ref[rows, :] = acc_ref[rows, :] * linv[:, tq * hh:tq * (hh + 1)]
    rows = lax.broadcasted_iota(jnp.int32, (tq, 1), 0)
    valid = jnp.logical_or(i > 0, rows >= TILE - N_META)
    out_ref[...] = jnp.where(valid, acc_ref[...].T, 0.0).astype(out_ref.dtype)


def _flash(qt, k, vt, bsz, lp):
    n = k.shape[0]
    tq = qt.shape[2]
    per = lp // tq
    return pl.pallas_call(
        functools.partial(_flash_body, per, tq, tq),
        grid=(bsz, per),
        in_specs=[pl.BlockSpec((None, 256, tq), lambda b, i: (b * per + i, 0, 0)),
                  pl.BlockSpec((lp, 128), lambda b, i: (b, 0)),
                  pl.BlockSpec((per, 128, tq), lambda b, i: (b, 0, 0))],
        out_specs=pl.BlockSpec((tq, 256), lambda b, i: (b * per + i, 0)),
        out_shape=jax.ShapeDtypeStruct((n, 256), bf16),
        scratch_shapes=[pltpu.VMEM((128, 4 * tq), bf16), pltpu.VMEM((1, 4 * tq), f32),
                        pltpu.VMEM((1, 4 * tq), f32), pltpu.VMEM((256, tq), f32),
                        pltpu.VMEM((tq, 4 * tq), f32), pltpu.VMEM((tq, 4 * tq), f32),
                        pltpu.VMEM((1, 4 * tq), f32), pltpu.VMEM((1, 4 * tq), f32)],
        compiler_params=_cparams("parallel", "parallel"),
        name="gqa_full",
    )(qt, k, vt)


def _mix_residual(y_refs, h_ref, wo_ref):
    mixed = jnp.concatenate([y_ref[...] for y_ref in y_refs], axis=1)
    return h_ref[...] + _dot(mixed, wo_ref[...])


def _ffn_body(y0_ref, y1_ref, y2_ref, y3_ref, h_ref, wo_ref, nw_ref, wg_ref, wu_ref, wd_ref,
              out_ref, hn_ref, u_ref, acc_ref):
    j = pl.program_id(1)

    @pl.when(j == 0)
    def _():
        hn = _mix_residual((y0_ref, y1_ref, y2_ref, y3_ref), h_ref, wo_ref)
        hn_ref[...] = hn
        u_ref[...] = _rms(hn, nw_ref[...]).astype(bf16)
        acc_ref[...] = jnp.zeros_like(acc_ref)

    u = u_ref[...]
    t = _silu(_dot(u, wg_ref[...])) * _dot(u, wu_ref[...])
    acc_ref[...] += _dot(t.astype(bf16), wd_ref[...])

    @pl.when(j == pl.num_programs(1) - 1)
    def _():
        out_ref[...] = hn_ref[...] + acc_ref[...]


def _ffn(ys, h, wo, nw, wg, wu, wd):
    n, d = h.shape
    ff = wg.shape[1]
    tm = TILE * _largest_divisor(n // TILE, 4)
    tf = 128 * _largest_divisor(ff // 128, 11)
    row = lambda i, j: (i, 0)
    const = lambda i, j: (0, 0)
    return pl.pallas_call(
        _ffn_body,
        grid=(n // tm, ff // tf),
        in_specs=[pl.BlockSpec((tm, 256), row)] * 4
        + [pl.BlockSpec((tm, d), row), pl.BlockSpec((d, d), const), pl.BlockSpec((1, d), const),
           pl.BlockSpec((d, tf), lambda i, j: (0, j)), pl.BlockSpec((d, tf), lambda i, j: (0, j)),
           pl.BlockSpec((tf, d), lambda i, j: (j, 0))],
        out_specs=pl.BlockSpec((tm, d), row),
        out_shape=jax.ShapeDtypeStruct((n, d), f32),
        scratch_shapes=[pltpu.VMEM((tm, d), f32), pltpu.VMEM((tm, d), bf16), pltpu.VMEM((tm, d), f32)],
        compiler_params=_cparams("parallel", "arbitrary"),
        name="ffn",
    )(*ys, h, wo, nw, wg, wu, wd)


MOE_TM = 1024
ROUTE_E, ROUTE_RANK, ROUTE_GATE, ROUTE_W = 0, 2, 4, 8


def _router_body(y0_ref, y1_ref, y2_ref, y3_ref, h_ref, wo_ref, nw_ref, r_ref,
                 hn_ref, route_ref, cnt_ref, base_ref):
    @pl.when(pl.program_id(0) == 0)
    def _():
        base_ref[...] = jnp.zeros_like(base_ref)

    hn = _mix_residual((y0_ref, y1_ref, y2_ref, y3_ref), h_ref, wo_ref)
    hn_ref[...] = hn
    u = _rms(hn, nw_ref[...])
    u_hi, u_lo = _split(u)
    r_hi, r_lo = _split(r_ref[...])
    logits = _dot(u_hi, r_hi) + _dot(u_lo, r_hi) + _dot(u_hi, r_lo)
    tm = logits.shape[0]
    lane = lax.broadcasted_iota(jnp.int32, logits.shape, 1)
    logits = jnp.where(lane < N_EXPERTS, logits, NEG)
    m1 = jnp.max(logits, axis=-1, keepdims=True)
    i1 = jnp.min(jnp.where(logits == m1, lane, 128), axis=-1, keepdims=True)
    rest = jnp.where(lane == i1, NEG, logits)
    m2 = jnp.max(rest, axis=-1, keepdims=True)
    i2 = jnp.min(jnp.where(rest == m2, lane, 128), axis=-1, keepdims=True)
    e2 = jnp.exp(m2 - m1)
    g1 = 1.0 / (1.0 + e2)
    g2 = e2 * g1

    sel1 = lane == i1
    sel2 = lane == i2
    onehot = jnp.where(sel1, 1.0, jnp.where(sel2, 1.0, 0.0))
    strict = jnp.where(lax.broadcasted_iota(jnp.int32, (tm, tm), 0)
                       > lax.broadcasted_iota(jnp.int32, (tm, tm), 1), 1.0, 0.0).astype(bf16)
    before = _dot(strict, onehot.astype(bf16)) + base_ref[0:1, :]
    r1 = jnp.sum(jnp.where(sel1, before, 0.0), axis=-1, keepdims=True)
    r2 = jnp.sum(jnp.where(sel2, before, 0.0), axis=-1, keepdims=True)
    route = jnp.zeros(logits.shape, f32)
    for k, val in enumerate((i1.astype(f32), i2.astype(f32), r1, r2, g1, g2)):
        route = jnp.where(lane == k, val, route)
    route_ref[...] = route[:, :ROUTE_W]
    base_ref[0:1, :] = base_ref[0:1, :] + jnp.sum(onehot, axis=0, keepdims=True)
    cnt_ref[...] = base_ref[...]


def _router(ys, h, wo, nw, router):
    n, d = h.shape
    tm = TILE * _largest_divisor(n // TILE, 8)
    row = lambda i: (i, 0)
    const = lambda i: (0, 0)
    return pl.pallas_call(
        _router_body,
        grid=(n // tm,),
        in_specs=[pl.BlockSpec((tm, 256), row)] * 4
        + [pl.BlockSpec((tm, d), row), pl.BlockSpec((d, d), const), pl.BlockSpec((1, d), const),
           pl.BlockSpec((d, 128), const)],
        out_specs=[pl.BlockSpec((tm, d), row), pl.BlockSpec((tm, ROUTE_W), row), pl.BlockSpec((8, 128), const)],
        out_shape=[jax.ShapeDtypeStruct((n, d), f32), jax.ShapeDtypeStruct((n, ROUTE_W), f32),
                   jax.ShapeDtypeStruct((8, 128), f32)],
        scratch_shapes=[pltpu.VMEM((8, 128), f32)],
        compiler_params=_cparams("arbitrary"),
        name="moe_router",
    )(*ys, h, wo, nw, router)


LANES = 128
ROW_DMA_UNROLL = 8


def _row_copy(src_ref, src_row, dst_ref, dst_row, sem):
    return pltpu.make_async_copy(src_ref.at[src_row], dst_ref.at[dst_row], sem)


def _to_slabs(dst_ref, val):
    dst_ref[...] = val.reshape(dst_ref.shape)


def _from_slabs(src_ref):
    rows, chunks, lanes = src_ref.shape
    return src_ref[...].reshape(rows, chunks * lanes)


def _dispatch_body(fill_ref, h_ref, nw_ref, dest_ref, xs_ref, u_ref, idx_ref, zero_ref,
                   sem_idx, sem_row, sem_fill):
    i = pl.program_id(0)
    tt = u_ref.shape[1]

    @pl.when(i == 0)
    def _():
        zero_ref[...] = jnp.zeros_like(zero_ref)
        fills = [pltpu.make_async_copy(zero_ref, xs_ref.at[pl.ds(fill_ref[e], MOE_TM)], sem_fill)
                 for e in range(N_EXPERTS)]
        for cp in fills:
            cp.start()
        for cp in fills:
            cp.wait()
        last = xs_ref.shape[0] // MOE_TM - 1
        for j in range(last - N_EXPERTS, last + 1):
            @pl.when(j >= fill_ref[N_EXPERTS])
            def _():
                cp = pltpu.make_async_copy(zero_ref, xs_ref.at[pl.ds(j * MOE_TM, MOE_TM)], sem_fill)
                cp.start()
                cp.wait()

    slot = i % 2
    rows_ref = u_ref.at[slot]
    sem = sem_row.at[slot]
    idx_copy = pltpu.make_async_copy(dest_ref.at[pl.ds(i * 2 * tt, 2 * tt)], idx_ref, sem_idx)
    idx_copy.start()
    _to_slabs(rows_ref, _rms(h_ref[...], nw_ref[...]))
    idx_copy.wait()

    def issue(r, carry):
        _row_copy(rows_ref, r, xs_ref, idx_ref[r], sem).start()
        _row_copy(rows_ref, r, xs_ref, idx_ref[tt + r], sem).start()
        return carry

    lax.fori_loop(0, tt, issue, 0, unroll=ROW_DMA_UNROLL)

    def drain(s):
        for _ in range(2):
            pltpu.make_async_copy(u_ref.at[s], xs_ref.at[pl.ds(0, tt)], sem_row.at[s]).wait()

    @pl.when(i > 0)
    def _():
        drain(1 - slot)

    @pl.when(i == pl.num_programs(0) - 1)
    def _():
        drain(slot)


def _dispatch(h, nw, dest_flat, fill_rows, rows, tt):
    n, d = h.shape
    return pl.pallas_call(
        _dispatch_body,
        grid_spec=pltpu.PrefetchScalarGridSpec(
            num_scalar_prefetch=1,
            grid=(n // tt,),
            in_specs=[pl.BlockSpec((tt, d), lambda i, fr: (i, 0)), pl.BlockSpec((1, d), lambda i, fr: (0, 0)),
                      pl.BlockSpec(memory_space=pl.ANY)],
            out_specs=pl.BlockSpec(memory_space=pl.ANY),
            scratch_shapes=[pltpu.VMEM((2, tt, d // LANES, LANES), f32), pltpu.SMEM((2 * tt,), jnp.int32),
                            pltpu.VMEM((MOE_TM, d // LANES, LANES), f32), pltpu.SemaphoreType.DMA(()),
                            pltpu.SemaphoreType.DMA((2,)), pltpu.SemaphoreType.DMA(())]),
        out_shape=jax.ShapeDtypeStruct((rows, d // LANES, LANES), f32),
        compiler_params=_cparams("arbitrary"),
        name="moe_dispatch",
    )(fill_rows, h, nw, dest_flat)


def _experts_body(te_ref, nu_ref, x_ref, wg_ref, wu_ref, wd_ref, y_ref, xb_ref, acc_ref):
    j = pl.program_id(0)
    f = pl.program_id(1)

    @pl.when(j < nu_ref[0])
    def _():
        @pl.when(f == 0)
        def _():
            xb_ref[...] = _from_slabs(x_ref).astype(bf16)
            acc_ref[...] = jnp.zeros_like(acc_ref)

        x = xb_ref[...]
        t = _silu(_dot(x, wg_ref[...])) * _dot(x, wu_ref[...])
        acc_ref[...] += _dot(t.astype(bf16), wd_ref[...])

        @pl.when(f == pl.num_programs(1) - 1)
        def _():
            _to_slabs(y_ref, acc_ref[...])

    @pl.when(jnp.logical_and(j >= nu_ref[0], f == pl.num_programs(1) - 1))
    def _():
        y_ref[...] = jnp.zeros_like(y_ref)


def _experts(xs, tile_expert, n_used, wg, wu, wd, n_tiles):
    d = wg.shape[1]
    slab = (MOE_TM, d // LANES, LANES)
    ff = wg.shape[2]
    tf = 256 * _largest_divisor(ff // 256, 2)
    nf = ff // tf
    tile = lambda j, nu: jnp.minimum(j, nu[0] - 1)
    chunk = lambda j, f, nu: jnp.where(j < nu[0], f, nf - 1)
    return pl.pallas_call(
        _experts_body,
        grid_spec=pltpu.PrefetchScalarGridSpec(
            num_scalar_prefetch=2,
            grid=(n_tiles, nf),
            in_specs=[pl.BlockSpec(slab, lambda j, f, te, nu: (tile(j, nu), 0, 0)),
                      pl.BlockSpec((None, d, tf), lambda j, f, te, nu: (te[tile(j, nu)], 0, chunk(j, f, nu))),
                      pl.BlockSpec((None, d, tf), lambda j, f, te, nu: (te[tile(j, nu)], 0, chunk(j, f, nu))),
                      pl.BlockSpec((None, tf, d), lambda j, f, te, nu: (te[tile(j, nu)], chunk(j, f, nu), 0))],
            out_specs=pl.BlockSpec(slab, lambda j, f, te, nu: (j, 0, 0)),
            scratch_shapes=[pltpu.VMEM((MOE_TM, d), bf16), pltpu.VMEM((MOE_TM, d), f32)]),
        out_shape=jax.ShapeDtypeStruct((n_tiles * MOE_TM, d // LANES, LANES), f32),
        compiler_params=_cparams("arbitrary", "arbitrary"),
        name="moe_experts",
    )(tile_expert, n_used, xs, wg, wu, wd)


def _combine_body(final, h_ref, route_ref, fnw_ref, dest_ref, ys_ref, out_ref, buf_ref, idx_ref,
                  sem_idx, sem_row):
    i = pl.program_id(0)
    tt = h_ref.shape[0]
    slot = i % 2

    def gather(step, s):
        idx_copy = pltpu.make_async_copy(dest_ref.at[pl.ds(step * 2 * tt, 2 * tt)], idx_ref, sem_idx)
        idx_copy.start()
        idx_copy.wait()

        def issue(r, carry):
            _row_copy(ys_ref, idx_ref[r], buf_ref.at[s, 0], r, sem_row.at[s]).start()
            _row_copy(ys_ref, idx_ref[tt + r], buf_ref.at[s, 1], r, sem_row.at[s]).start()
            return carry

        lax.fori_loop(0, tt, issue, 0, unroll=ROW_DMA_UNROLL)

    @pl.when(i == 0)
    def _():
        gather(0, 0)

    @pl.when(i + 1 < pl.num_programs(0))
    def _():
        gather(i + 1, 1 - slot)

    for k in range(2):
        pltpu.make_async_copy(ys_ref.at[pl.ds(0, tt)], buf_ref.at[slot, k], sem_row.at[slot]).wait()
    route = route_ref[...]
    g1 = route[:, ROUTE_GATE:ROUTE_GATE + 1]
    g2 = route[:, ROUTE_GATE + 1:ROUTE_GATE + 2]
    y = h_ref[...] + g1 * _from_slabs(buf_ref.at[slot, 0]) + g2 * _from_slabs(buf_ref.at[slot, 1])
    out_ref[...] = _rms(y, fnw_ref[...]) if final else y


def _combine(h, route, fnw, dest_flat, ys, tt, final):
    n, d = h.shape
    return pl.pallas_call(
        functools.partial(_combine_body, final),
        grid=(n // tt,),
        in_specs=[pl.BlockSpec((tt, d), lambda i: (i, 0)), pl.BlockSpec((tt, ROUTE_W), lambda i: (i, 0)),
                  pl.BlockSpec((1, d), lambda i: (0, 0)),
                  pl.BlockSpec(memory_space=pl.ANY), pl.BlockSpec(memory_space=pl.ANY)],
        out_specs=pl.BlockSpec((tt, d), lambda i: (i, 0)),
        out_shape=jax.ShapeDtypeStruct((n, d), f32),
        scratch_shapes=[pltpu.VMEM((2, 2, tt, d // LANES, LANES), f32), pltpu.SMEM((2 * tt,), jnp.int32),
                        pltpu.SemaphoreType.DMA(()), pltpu.SemaphoreType.DMA((2,))],
        compiler_params=_cparams("arbitrary"),
        name="moe_combine",
    )(h, route, fnw, dest_flat, ys)


def _moe(mix, h, wo, nw, router, wg, wu, wd, fnw, final):
    n, d = h.shape
    tt = TILE * _largest_divisor(n // TILE, 4)
    h, route, counts = _router(mix, h, wo, nw, router)

    cnt = counts[0, :N_EXPERTS].astype(jnp.int32)
    padded = (cnt + MOE_TM - 1) // MOE_TM * MOE_TM
    ends = jnp.cumsum(padded)
    off = ends - padded
    n_tiles = -(-2 * n // MOE_TM) + N_EXPERTS
    tile_expert = jnp.minimum(
        jnp.sum((jnp.arange(n_tiles)[:, None] * MOE_TM >= ends[None, :]).astype(jnp.int32), axis=1),
        N_EXPERTS - 1).astype(jnp.int32)
    n_used = (ends[-1:] // MOE_TM).astype(jnp.int32)
    sel = route[:, ROUTE_E:ROUTE_E + 2].astype(jnp.int32)
    rank = route[:, ROUTE_RANK:ROUTE_RANK + 2].astype(jnp.int32)
    dest = jnp.sum(jnp.where(sel[..., None] == jnp.arange(N_EXPERTS), off, 0), axis=-1) + rank
    dest_flat = dest.reshape(n // tt, tt, 2).transpose(0, 2, 1).reshape(-1)

    fill_rows = jnp.concatenate([off + cnt, n_used]).astype(jnp.int32)
    xs = _dispatch(h, nw, dest_flat, fill_rows, (n_tiles + 1) * MOE_TM, tt)
    ys = _experts(xs, tile_expert, n_used, wg, wu, wd, n_tiles)
    return _combine(h, route, fnw, dest_flat, ys, tt, final)


def _final_norm_body(h_ref, w_ref, out_ref):
    out_ref[...] = _rms(h_ref[...], w_ref[...])


def _final_norm(h, w):
    n, d = h.shape
    tm = TILE * _largest_divisor(n // TILE, 8)
    return pl.pallas_call(
        _final_norm_body,
        grid=(n // tm,),
        in_specs=[pl.BlockSpec((tm, d), lambda i: (i, 0)), pl.BlockSpec((1, d), lambda i: (0, 0))],
        out_specs=pl.BlockSpec((tm, d), lambda i: (i, 0)),
        out_shape=jax.ShapeDtypeStruct((n, d), f32),
        compiler_params=_cparams("parallel"),
        name="final_norm",
    )(h, w)


def _rope_tables(seq, pad):
    t = np.arange(seq)
    meta_pos = np.arange(N_META) - N_META
    row = np.concatenate([meta_pos, t // GRID_W]).astype(np.float32)
    col = np.concatenate([meta_pos, t % GRID_W]).astype(np.float32)
    half = HEAD_DIM // 2
    inv = np.float32(ROPE_THETA) ** (-np.arange(0, half, 2, dtype=np.float32) / np.float32(half))
    ang = np.concatenate([row[:, None] * inv, col[:, None] * inv], axis=-1)
    ang = np.tile(np.repeat(ang, 2, axis=-1), (1, QK_WIDTH // HEAD_DIM))
    ang = np.pad(ang, ((pad, 0), (0, 0))).astype(np.float32)
    return jnp.asarray(np.cos(ang)), jnp.asarray(np.sin(ang))


def _pair_swap_matrix(width):
    i = jnp.arange(width)
    p = jnp.zeros((width, width), f32)
    p = p.at[i[1::2], i[0::2]].set(-1.0)
    p = p.at[i[0::2], i[1::2]].set(1.0)
    return p.astype(bf16)


def _group_mean_matrix(width, group):
    i = jnp.arange(width)
    return ((i[:, None] // group == i[None, :] // group).astype(f32) / group).astype(bf16)


def _t5_bucket(rel):
    nb = REL_BUCKETS // 2
    max_exact = nb // 2
    ret = (rel > 0).astype(jnp.int32) * nb
    n = jnp.abs(rel)
    nf = jnp.maximum(n, 1).astype(f32)
    large = max_exact + (jnp.log(nf / max_exact) / math.log(REL_MAX_DIST / max_exact)
                         * (nb - max_exact)).astype(jnp.int32)
    large = jnp.minimum(large, nb - 1)
    return ret + jnp.where(n < max_exact, n, large)


def _swa_bias_tables(rel_bias, lp):
    def lookup(bucket):
        out = jnp.zeros((rel_bias.shape[1],) + bucket.shape, f32)
        for b in range(REL_BUCKETS):
            out = jnp.where((bucket == b)[None], rel_bias[b].astype(f32)[:, None, None], out)
        return out

    qi = jnp.arange(TILE)
    ki = jnp.arange(3 * TILE)
    rel = ki[None, :] - TILE - qi[:, None]
    band = jnp.where((jnp.abs(rel) <= WINDOW)[None], lookup(_t5_bucket(rel)), NEG)
    pos = jnp.arange(lp) - (TILE - N_META)
    rel_m = jnp.arange(N_META)[None, :] - pos[:, None]
    meta = lookup(_t5_bucket(rel_m))
    nt = lp // TILE
    band_t = jnp.transpose(band, (2, 0, 1)).reshape(3 * TILE, 4 * TILE) * LOG2E
    meta_t = (meta.reshape(4, nt, TILE, N_META).transpose(1, 3, 0, 2).reshape(nt, N_META, 4 * TILE)
              * LOG2E)
    return band_t, meta_t


def _row(v, width=None):
    v = v.astype(f32).reshape(1, -1)
    if width is not None and v.shape[1] < width:
        v = jnp.pad(v, ((0, 0), (0, width - v.shape[1])))
    return v


def kernel(x, meta_tokens, rel_bias, norm_mix_w, norm_ffn_w, w_in, ssd_conv_w, ssd_conv_b, ssd_dt_bias, ssd_a_log, ssd_d, ssd_norm_w, gla_gate_w2, gla_gate_b, gla_norm_w, swa_sink, gqa_q_norm_w, gqa_k_norm_w, w_out, ffn_w_gate, ffn_w_up, ffn_w_down, moe_router, moe_w_gate, moe_w_up, moe_w_down, final_norm_w):
    bsz, seq, d = x.shape
    depth = w_in.shape[0]
    pad = (-(seq + N_META)) % TILE
    assert pad == TILE - N_META and seq % TILE == 0
    lp = pad + N_META + seq
    n = bsz * lp

    meta = jnp.broadcast_to(meta_tokens[None].astype(x.dtype), (bsz, N_META, d))
    h = jnp.concatenate([jnp.zeros((bsz, pad, d), x.dtype), meta, x], axis=1).reshape(n, d)

    cos, sin = _rope_tables(seq, pad)
    rot = _pair_swap_matrix(QK_WIDTH)
    gavg = _group_mean_matrix(QK_WIDTH, HEAD_DIM)
    bias_band, bias_meta = _swa_bias_tables(rel_bias, lp)
    offs = [0]
    for s in IN_SIZES:
        offs.append(offs[-1] + s)

    for i in range(depth):
        wi = w_in[i]
        cols = [wi[:, offs[j]:offs[j + 1]] for j in PACK_ORDER]
        w_pack = jnp.concatenate(cols + [jnp.zeros((d, PACK_WIDTH - offs[-1]), wi.dtype)], axis=1).astype(bf16)
        o_ssd, o_small, o_gla, sk, ak, aq, av, sq, sv = _inproj(
            h, _row(norm_mix_w[i]), w_pack, cos, sin, rot, gavg,
            _row(jnp.concatenate([jnp.tile(gqa_q_norm_w[i], 4), jnp.tile(gqa_k_norm_w[i], 2)])), bsz, lp)

        convw = jnp.pad(ssd_conv_w[i].astype(f32), ((0, 8 - SSD_CONV), (0, 0)))
        convb = _row(ssd_conv_b[i])
        dtb = _row(ssd_dt_bias[i].reshape(-1), 128)
        alog = _row(ssd_a_log[i].reshape(-1), 128)
        yf, xbc = _ssd(False, o_ssd, o_small, convw, convb, dtb, alog,
                       _row(jnp.repeat(ssd_d[i], SSD_HEAD_DIM)), bsz, lp)
        y_ssd = _ssd(True, o_ssd, o_small, convw, convb, dtb, alog, (yf, xbc, _row(ssd_norm_w[i])), bsz, lp)

        def gate_w(direction):
            lo = SMALL_GA + GLA_GATE_RANK * direction
            full = jnp.zeros((128, GLA_KEY), f32).at[lo:lo + GLA_GATE_RANK].set(gla_gate_w2[i, direction].astype(f32))
            return full.astype(bf16)

        of = _gla(False, o_gla, o_small, gate_w(0), _row(gla_gate_b[i, 0]), None, bsz, lp)
        y_gla = _gla(True, o_gla, o_small, gate_w(1), _row(gla_gate_b[i, 1]),
                     (of, _row(jnp.tile(gla_norm_w[i], GLA_HEADS)), gavg), bsz, lp)

        sink = _row(jnp.repeat(swa_sink[i].astype(f32), TILE)) * LOG2E
        y_swa = _swa(sq, sk, sv, bias_band, bias_meta, sink, bsz, lp)
        y_g2 = _flash(aq, ak, av, bsz, lp)

        mix = (y_ssd, y_gla, y_swa, y_g2)
        wo = w_out[i].astype(bf16)
        j = i // 2
        if i % 2 == 0:
            h = _ffn(mix, h, wo, _row(norm_ffn_w[i]), ffn_w_gate[j].astype(bf16), ffn_w_up[j].astype(bf16),
                     ffn_w_down[j].astype(bf16))
            if i == depth - 1:
                h = _final_norm(h, _row(final_norm_w))
        else:
            router = jnp.pad(moe_router[j].astype(f32), ((0, 0), (0, 128 - N_EXPERTS)))
            h = _moe(mix, h, wo, _row(norm_ffn_w[i]), router, moe_w_gate[j].astype(bf16),
                     moe_w_up[j].astype(bf16), moe_w_down[j].astype(bf16), _row(final_norm_w), i == depth - 1)
    return h.reshape(bsz, lp, d)[:, pad + N_META:]
```

```python
import functools
import math

import jax
import jax.numpy as jnp
import numpy as np
from jax import lax
from jax.experimental import pallas as pl
from jax.experimental.pallas import tpu as pltpu

f32 = jnp.float32
bf16 = jnp.bfloat16

N_META = 16
HEAD_DIM = 64
GRID_W = 64
EPS = 1e-6
ROPE_THETA = 10000.0
TILE = 128
SSD_HEADS = 4
SSD_HEAD_DIM = 64
SSD_INNER = 256
SSD_STATE = 128
SSD_CONV = 5
SSD_CONV_DIM = 768
GLA_HEADS = 4
GLA_DK = 32
GLA_DV = 64
GLA_KEY = 128
GLA_VAL = 256
GLA_GATE_RANK = 16
GLA_GATE_NORM = 16.0
GLA_CHUNK = 64
WINDOW = 128
REL_BUCKETS = 32
REL_MAX_DIST = 128
N_EXPERTS = 8
NEG = -1e30
LOG2E = math.log2(math.e)
QK_WIDTH = 384
FLASH_TQ = 384
CONV_HALO = 16
VMEM_LIMIT = 56 * 1024 * 1024

IN_SIZES = (256, 768, 8, 128, 128, 256, 256, 32, 256, 128, 128, 256, 128, 128)
(_Z, _XBC, _DT, _GQ, _GK, _GV, _GR, _GA, _SQ, _SK, _SV, _AQ, _AK, _AV) = range(14)
PACK_ORDER = (_XBC, _Z, _GQ, _GK, _GV, _GR, _SQ, _SK, _SV, _AQ, _AK, _AV, _DT, _GA)
PACK_WIDTH = 2944
C_SSD, C_GLA, C_SWA, C_AQ, C_AK, C_AV, C_SMALL = 0, 1024, 1792, 2304, 2560, 2688, 2816
SMALL_DT, SMALL_GA = 0, 8


def _cparams(*sem):
    return pltpu.CompilerParams(dimension_semantics=sem, vmem_limit_bytes=VMEM_LIMIT)


def _dot(a, b):
    return jnp.dot(a, b, preferred_element_type=f32)


def _dot_nt(a, b):
    return lax.dot_general(a, b, (((1,), (1,)), ((), ())), preferred_element_type=f32)


def _dot_tn(a, b):
    return lax.dot_general(a, b, (((0,), (0,)), ((), ())), preferred_element_type=f32)


def _split(a):
    hi = a.astype(bf16)
    lo = (a - hi.astype(f32)).astype(bf16)
    return hi, lo


def _dot_split_lhs(a, b):
    hi, lo = _split(a)
    return _dot(hi, b) + _dot(lo, b)


def _dot_split_rhs(t, x):
    hi, lo = _split(x)
    return _dot(t, hi) + _dot(t, lo)


def _rms(x, w):
    return x * lax.rsqrt(jnp.mean(x * x, axis=-1, keepdims=True) + EPS) * w


def _silu(x):
    return x * (0.5 + 0.5 * jnp.tanh(0.5 * x))


def _softplus(x):
    return jnp.maximum(x, 0.0) + jnp.log(1.0 + jnp.exp(-jnp.abs(x)))


def _log_sigmoid(x):
    return jnp.minimum(x, 0.0) - jnp.log(1.0 + jnp.exp(-jnp.abs(x)))


def _tri(n, rev):
    r = lax.broadcasted_iota(jnp.int32, (n, n), 0)
    c = lax.broadcasted_iota(jnp.int32, (n, n), 1)
    return (r <= c) if rev else (r >= c)


def _valid_rows(tile, n):
    rows = lax.broadcasted_iota(jnp.int32, (n, 1), 0)
    return jnp.logical_or(tile > 0, rows >= TILE - N_META)


def _inproj_body(h_ref, nw_ref, w_ref, cos_ref, sin_ref, rot_ref, gavg_ref, qkw_ref,
                 ssd_ref, small_ref, gla_ref, sk_ref, k_ref, q_ref, v_ref, sq_ref, sv_ref):
    u = _rms(h_ref[...], nw_ref[...]).astype(bf16)

    def mm(lo, hi):
        return _dot(u, w_ref[:, lo:hi])

    ssd_ref[...] = mm(C_SSD, C_GLA).astype(bf16)
    gla_ref[...] = mm(C_GLA, C_SWA).astype(bf16)
    small_ref[...] = mm(C_SMALL, PACK_WIDTH)
    akv = mm(C_AK, C_SMALL)
    v_ref[...] = akv[:, 128:].T.astype(bf16)

    skv = mm(C_SWA + 256, C_AQ)
    sk_ref[...] = skv[:, :128].astype(bf16)
    sv_ref[...] = skv[:, 128:].T.astype(bf16)
    sqt = (mm(C_SWA, C_SWA + 256) * (HEAD_DIM ** -0.5 * LOG2E)).T.astype(bf16)
    sq_ref[...] = jnp.zeros_like(sq_ref)
    for t in range(sq_ref.shape[0]):
        for hh in range(4):
            lo = HEAD_DIM * (hh // 2)
            sq_ref[t, lo:lo + HEAD_DIM, TILE * hh:TILE * (hh + 1)] = (
                sqt[HEAD_DIM * hh:HEAD_DIM * (hh + 1), TILE * t:TILE * (t + 1)])

    t = jnp.concatenate([mm(C_AQ, C_AK), akv[:, :128]], axis=1)
    ms = _dot_split_lhs(t * t, gavg_ref[...])
    tn = t * lax.rsqrt(ms + EPS) * qkw_ref[...]
    tr = _dot(tn.astype(bf16), rot_ref[...])
    qk = tn * cos_ref[...] + tr * sin_ref[...]
    q_ref[...] = (qk[:, :256] * (HEAD_DIM ** -0.5 * LOG2E)).T.astype(bf16)
    k_ref[...] = qk[:, 256:].astype(bf16)


def _inproj(h, nw, w, cos, sin, rot, gavg, qkw, bsz, lp):
    n = h.shape[0]
    d = h.shape[1]
    tm = TILE * _largest_divisor(lp // TILE, 3)
    per = lp // tm
    row = lambda b, i: (b * per + i, 0)
    const = lambda b, i: (0, 0)
    tab = lambda b, i: (i, 0)
    outs = ((1024, bf16), (128, f32), (768, bf16), (128, bf16), (128, bf16))
    slab = lambda b, i: (b * per + i, 0, 0)
    sub = tm // TILE
    return pl.pallas_call(
        _inproj_body,
        grid=(bsz, per),
        in_specs=[pl.BlockSpec((tm, d), row), pl.BlockSpec((1, d), const),
                  pl.BlockSpec((d, PACK_WIDTH), const),
                  pl.BlockSpec((tm, QK_WIDTH), tab), pl.BlockSpec((tm, QK_WIDTH), tab),
                  pl.BlockSpec((QK_WIDTH, QK_WIDTH), const), pl.BlockSpec((QK_WIDTH, QK_WIDTH), const),
                  pl.BlockSpec((1, QK_WIDTH), const)],
        out_specs=([pl.BlockSpec((tm, c), row) for c, _ in outs]
                   + [pl.BlockSpec((None, 256, tm), slab), pl.BlockSpec((None, 128, tm), slab),
                      pl.BlockSpec((sub, 128, 4 * TILE), slab), pl.BlockSpec((None, 128, tm), slab)]),
        out_shape=([jax.ShapeDtypeStruct((n, c), t) for c, t in outs]
                   + [jax.ShapeDtypeStruct((n // tm, 256, tm), bf16),
                      jax.ShapeDtypeStruct((n // tm, 128, tm), bf16),
                      jax.ShapeDtypeStruct((n // TILE, 128, 4 * TILE), bf16),
                      jax.ShapeDtypeStruct((n // tm, 128, tm), bf16)]),
        compiler_params=_cparams("parallel", "parallel"),
        name="inproj",
    )(h, nw, w, cos, sin, rot, gavg, qkw)


def _largest_divisor(n, cap):
    return max(k for k in range(1, cap + 1) if n % k == 0)


def _for_each_batch(chain, batched, n_scratch):
    def body(*refs):
        nb = next(r.shape[0] for r, flag in zip(refs, batched) if flag)
        io, scratch = refs[:len(batched)], refs[len(batched):]
        assert len(scratch) == n_scratch * nb
        for b in range(nb):
            chain(*[r.at[b] if flag else r for r, flag in zip(io, batched)],
                  *[scratch[k * nb + b] for k in range(n_scratch)])
    return body


def _per_batch_scratch(nb, *shapes):
    return [pltpu.VMEM(shape, dtype) for shape, dtype in shapes for _ in range(nb)]


def _ssd_scan_tile(rev, xs, bmat, cmat, dt, cum, tot, st):
    col0 = SMALL_DT + (SSD_HEADS if rev else 0)
    causal = _tri(TILE, rev)
    cum_t = cum.T
    e_cum = jnp.exp(cum)
    e_tot = jnp.exp(tot)
    head_of = lax.broadcasted_iota(jnp.int32, (TILE, SSD_INNER), 1) // SSD_HEAD_DIM
    head_of_row = lax.broadcasted_iota(jnp.int32, (1, SSD_INNER), 1) // SSD_HEAD_DIM

    def widen(cols, like):
        out = jnp.zeros(like.shape, f32)
        for hh in range(SSD_HEADS):
            out = jnp.where(like == hh, cols[:, col0 + hh:col0 + hh + 1], out)
        return out

    def own_head(r):
        out = jnp.zeros((TILE, SSD_INNER), f32)
        for hh in range(SSD_HEADS):
            out = jnp.where(head_of == hh, r[TILE * hh:TILE * (hh + 1)], out)
        return out

    xd = (xs * widen(dt, head_of)).astype(bf16)
    scores, bws = [], []
    for g in range(2):
        bg = bmat[:, SSD_STATE * g:SSD_STATE * (g + 1)]
        cbg = _dot_nt(cmat[:, SSD_STATE * g:SSD_STATE * (g + 1)], bg)
        for hh in (2 * g, 2 * g + 1):
            col = col0 + hh
            a_col = cum[:, col:col + 1]
            decay = jnp.where(causal, jnp.exp(a_col - cum_t[col:col + 1, :]), 0.0)
            scores.append((cbg * decay).astype(bf16))
            bws.append((bg.astype(f32) * jnp.exp(tot[:, col:col + 1] - a_col)).astype(bf16))
    y = own_head(_dot(jnp.concatenate(scores, axis=0), xd))
    y = y + _dot(cmat, st.astype(bf16)) * widen(e_cum, head_of)
    upd = _dot_tn(jnp.concatenate(bws, axis=1), xd)
    rows = [jnp.where(head_of == 2 * g, upd[SSD_STATE * 2 * g:SSD_STATE * (2 * g + 1)],
                      jnp.where(head_of == 2 * g + 1,
                                upd[SSD_STATE * (2 * g + 1):SSD_STATE * (2 * g + 2)], 0.0))
            for g in range(2)]
    return y, st * widen(e_tot, head_of_row) + jnp.concatenate(rows, axis=0)


def _ssd_body(rev, nt, *refs):
    if rev:
        xbc_ref, small_ref, dtb_ref, alog_ref, yf_ref, z_ref, normw_ref, out_ref, state_ref = refs
    else:
        (cur_ref, prev_ref, next_ref, small_ref, convw_ref, convb_ref, dtb_ref, alog_ref, dskip_ref,
         out_ref, xbc_out_ref, state_ref) = refs[:12]
        ext_refs = refs[12:]
    c = pl.program_id(0)
    tile = (nt - 1 - c) if rev else c
    nb = small_ref.shape[0]
    valid = _valid_rows(tile, TILE)

    a = -jnp.exp(alog_ref[...])
    dts = [jnp.where(valid, _softplus(small_ref[b] + dtb_ref[...]), 0.0) for b in range(nb)]
    cum_all = _dot_split_rhs(_tri(TILE, rev).astype(bf16),
                             jnp.concatenate([dt * a for dt in dts], axis=1))
    for b in range(nb):
        dt = dts[b]
        if rev:
            xbc = xbc_ref[b]
            xs = xbc[:, :SSD_INNER].astype(f32)
            bc = xbc[:, SSD_INNER:]
        else:
            ext_ref = ext_refs[b]
            ext_ref[0:CONV_HALO, :] = jnp.where(tile > 0, prev_ref[b].astype(f32), 0.0)
            ext_ref[CONV_HALO:CONV_HALO + TILE, :] = jnp.where(valid, cur_ref[b].astype(f32), 0.0)
            ext_ref[CONV_HALO + TILE:, :] = jnp.where(tile < nt - 1, next_ref[b].astype(f32), 0.0)
            first = CONV_HALO - (SSD_CONV - 1) // 2
            strips = []
            for lo in range(0, SSD_CONV_DIM, 128):
                cols = slice(lo, lo + 128)
                acc = jnp.zeros((TILE, 128), f32) + convb_ref[:, cols]
                for k in range(SSD_CONV):
                    acc = acc + convw_ref[k:k + 1, cols] * ext_ref[first + k:first + k + TILE, cols]
                strips.append(jnp.where(valid, _silu(acc), 0.0))
            xbc = jnp.concatenate(strips, axis=1)
            xbc_out_ref[b] = xbc.astype(bf16)
            xs = xbc[:, :SSD_INNER]
            bc = xbc[:, SSD_INNER:].astype(bf16)
        st = jnp.where(c == 0, 0.0, state_ref[b])
        y, st = _ssd_scan_tile(rev, xs, bc[:, :2 * SSD_STATE], bc[:, 2 * SSD_STATE:], dt,
                               cum_all[:, 128 * b:128 * (b + 1)],
                               jnp.sum(dt * a, axis=0, keepdims=True), st)
        state_ref[b] = st
        if rev:
            y = (yf_ref[b] + y) * _silu(z_ref[b].astype(f32))
            out_ref[b] = jnp.where(valid, _rms(y, normw_ref[...]), 0.0).astype(out_ref.dtype)
        else:
            out_ref[b] = y + dskip_ref[...] * xs


def _ssd(rev, o_ssd, o_small, convw, convb, dtb, alog, extra, bsz, lp):
    n = o_ssd.shape[0]
    nt = lp // TILE
    hb = TILE // CONV_HALO
    o_ssd = o_ssd.reshape(bsz, lp, -1)
    o_small = o_small.reshape(bsz, lp, -1)

    def tile_of(c):
        return (nt - 1 - c) if rev else c

    cur = lambda c: (0, tile_of(c), 0)
    prev = lambda c: (0, jnp.maximum(tile_of(c) * hb - 1, 0), 0)
    nxt = lambda c: (0, jnp.minimum((tile_of(c) + 1) * hb, lp // CONV_HALO - 1), 0)
    zcol = lambda c: (0, tile_of(c), SSD_CONV_DIM // SSD_INNER)
    const = lambda c: (0, 0)
    tile3 = lambda width: pl.BlockSpec((bsz, TILE, width), cur)
    state = pltpu.VMEM((bsz, 2 * SSD_STATE, SSD_INNER), f32)
    if rev:
        yf, xbc, normw = extra
        out = pl.pallas_call(
            functools.partial(_ssd_body, rev, nt),
            grid=(nt,),
            in_specs=[tile3(SSD_CONV_DIM), tile3(128), pl.BlockSpec((1, 128), const),
                      pl.BlockSpec((1, 128), const), tile3(SSD_INNER),
                      pl.BlockSpec((bsz, TILE, SSD_INNER), zcol), pl.BlockSpec((1, SSD_INNER), const)],
            out_specs=tile3(SSD_INNER),
            out_shape=jax.ShapeDtypeStruct((bsz, lp, SSD_INNER), bf16),
            scratch_shapes=[state],
            compiler_params=_cparams("arbitrary"),
            name="ssd_rev",
        )(xbc, o_small, dtb, alog, yf, o_ssd, normw)
        return out.reshape(n, SSD_INNER)
    halo = lambda im: pl.BlockSpec((bsz, CONV_HALO, SSD_CONV_DIM), im)
    return pl.pallas_call(
        functools.partial(_ssd_body, rev, nt),
        grid=(nt,),
        in_specs=[tile3(SSD_CONV_DIM), halo(prev), halo(nxt), tile3(128),
                  pl.BlockSpec((8, SSD_CONV_DIM), const), pl.BlockSpec((1, SSD_CONV_DIM), const),
                  pl.BlockSpec((1, 128), const), pl.BlockSpec((1, 128), const),
                  pl.BlockSpec((1, SSD_INNER), const)],
        out_specs=[tile3(SSD_INNER), tile3(SSD_CONV_DIM)],
        out_shape=[jax.ShapeDtypeStruct((bsz, lp, SSD_INNER), f32),
                   jax.ShapeDtypeStruct((bsz, lp, SSD_CONV_DIM), bf16)],
        scratch_shapes=[state] + _per_batch_scratch(bsz, ((TILE + 2 * CONV_HALO, SSD_CONV_DIM), f32)),
        compiler_params=_cparams("arbitrary"),
        name="ssd_fwd",
    )(o_ssd, o_ssd, o_ssd, o_small, convw, convb, dtb, alog, extra)


def _gla_body(rev, nt, *refs):
    if rev:
        x_ref, small_ref, wg_ref, gb_ref, of_ref, normw_ref, gavg_ref, out_ref, st_ref = refs
    else:
        x_ref, small_ref, wg_ref, gb_ref, out_ref, st_ref = refs
    c = pl.program_id(0)
    tile = (nt - 1 - c) if rev else c
    nb = x_ref.shape[0]
    n = GLA_CHUNK
    valid = _valid_rows(tile, TILE)

    pre = _dot(small_ref[...].reshape(nb * TILE, 128).astype(bf16), wg_ref[...]) + gb_ref[...]
    g_all = _log_sigmoid(pre) / GLA_GATE_NORM
    gs = [jnp.where(valid, g_all[TILE * b:TILE * (b + 1)], 0.0) for b in range(nb)]
    row = lax.broadcasted_iota(jnp.int32, (TILE, TILE), 0)
    col = lax.broadcasted_iota(jnp.int32, (TILE, TILE), 1)
    same_chunk = (row // n) == (col // n)
    cum_mat = jnp.where(jnp.logical_and(same_chunk, (row <= col) if rev else (row >= col)), 1.0, 0.0)
    bc_all = _dot_split_rhs(cum_mat.astype(bf16), jnp.concatenate(gs, axis=1))

    first = slice(n, 2 * n) if rev else slice(0, n)
    second = slice(0, n) if rev else slice(n, 2 * n)
    rows = lax.broadcasted_iota(jnp.int32, (TILE, 1), 0)
    in_first = (rows >= n) if rev else (rows < n)
    tri = _tri(n, rev)
    tri4 = jnp.concatenate([tri] * GLA_HEADS, axis=0)
    lane_head = lax.broadcasted_iota(jnp.int32, (n, GLA_KEY), 1) // GLA_DK
    out_head = lax.broadcasted_iota(jnp.int32, (n, GLA_VAL), 1) // GLA_DV
    blockdiag = (lax.broadcasted_iota(jnp.int32, (GLA_VAL, GLA_KEY), 0) // GLA_DV
                 == lax.broadcasted_iota(jnp.int32, (GLA_VAL, GLA_KEY), 1) // GLA_DK)

    def stack_heads(a):
        return jnp.concatenate([jnp.where(lane_head == hh, a, 0.0) for hh in range(GLA_HEADS)],
                               axis=0).astype(bf16)

    def own_head(r):
        out = jnp.zeros((n, GLA_VAL), f32)
        for hh in range(GLA_HEADS):
            out = jnp.where(out_head == hh, r[n * hh:n * (hh + 1)], out)
        return out

    both = []
    for b in range(nb):
        x = x_ref[b].astype(f32)
        g = gs[b]
        bc = bc_all[:, 128 * b:128 * (b + 1)]
        q = jnp.where(valid, x[:, :GLA_KEY], 0.0) * GLA_DK ** -0.5
        k = jnp.where(valid, x[:, GLA_KEY:2 * GLA_KEY], 0.0)
        v = jnp.where(valid, x[:, 2 * GLA_KEY:2 * GLA_KEY + GLA_VAL], 0.0).astype(bf16)
        bl_first = jnp.sum(g[first], axis=0, keepdims=True)
        bl_second = jnp.sum(g[second], axis=0, keepdims=True)
        qt = q * jnp.exp(bc)
        kt = k * jnp.exp(-bc)
        kw = k * jnp.exp(jnp.where(in_first, bl_first, bl_second) - bc)

        att_f = jnp.where(tri4, _dot_nt(stack_heads(qt[first]), kt[first].astype(bf16)), 0.0)
        o_f = own_head(_dot(att_f.astype(bf16), v[first]))
        keys = jnp.concatenate([kw[first], kt[second]], axis=0).astype(bf16)
        vals = jnp.concatenate([v[first], v[second]], axis=0)
        att_s = _dot_nt(stack_heads(qt[second]), keys)
        att_s = jnp.concatenate([att_s[:, :n], jnp.where(tri4, att_s[:, n:], 0.0)], axis=1)
        o_s = own_head(_dot(att_s.astype(bf16), vals))

        st = jnp.where(c == 0, 0.0, st_ref[b])
        q_in = qt * jnp.exp(jnp.where(in_first, 0.0, bl_first))
        o = _dot_nt(q_in.astype(bf16), st.astype(bf16)) + jnp.concatenate(
            [o_s, o_f] if rev else [o_f, o_s], axis=0)
        k_out = (kw * jnp.exp(jnp.where(in_first, bl_second, 0.0))).astype(bf16)
        st_ref[b] = st * jnp.exp(bl_first + bl_second) + jnp.where(blockdiag, _dot_tn(v, k_out), 0.0)

        if rev:
            both.append(of_ref[b] + o)
        else:
            out_ref[b] = o

    if rev:
        o = jnp.concatenate(both, axis=0)
        ms = _dot_split_lhs(o * o, gavg_ref[...])
        on = o * lax.rsqrt(ms + EPS) * normw_ref[...]
        for b in range(nb):
            r = x_ref[b, :, 2 * GLA_KEY + GLA_VAL:].astype(f32)
            out_ref[b] = jnp.where(valid, on[TILE * b:TILE * (b + 1)] * _silu(r), 0.0).astype(out_ref.dtype)


def _gla(rev, o_gla, o_small, wg, gb, extra, bsz, lp):
    n = o_gla.shape[0]
    nt = lp // TILE
    cur = lambda c: (0, (nt - 1 - c) if rev else c, 0)
    const = lambda c: (0, 0)
    in_specs = [pl.BlockSpec((bsz, TILE, 768), cur), pl.BlockSpec((bsz, TILE, 128), cur),
                pl.BlockSpec((128, GLA_KEY), const), pl.BlockSpec((1, GLA_KEY), const)]
    args = [o_gla.reshape(bsz, lp, -1), o_small.reshape(bsz, lp, -1), wg, gb]
    if rev:
        of, normw, gavg = extra
        in_specs += [pl.BlockSpec((bsz, TILE, GLA_VAL), cur), pl.BlockSpec((1, GLA_VAL), const),
                     pl.BlockSpec((GLA_VAL, GLA_VAL), const)]
        args += [of.reshape(bsz, lp, -1), normw, gavg]
    out = pl.pallas_call(
        functools.partial(_gla_body, rev, nt),
        grid=(nt,),
        in_specs=in_specs,
        out_specs=pl.BlockSpec((bsz, TILE, GLA_VAL), cur),
        out_shape=jax.ShapeDtypeStruct((bsz, lp, GLA_VAL), bf16 if rev else f32),
        scratch_shapes=[pltpu.VMEM((bsz, GLA_VAL, GLA_KEY), f32)],
        compiler_params=_cparams("arbitrary"),
        name="gla_rev" if rev else "gla_fwd",
    )(*args)
    return out.reshape(n, GLA_VAL)


def _swa_body(nt, qp_ref, kp_ref, kc_ref, kn_ref, km_ref, vp_ref, vc_ref, vn_ref, vm_ref,
              bias_ref, bmeta_ref, sink_ref, out_ref):
    c = pl.program_id(0)
    krow = lax.broadcasted_iota(jnp.int32, (3 * TILE, 1), 0)
    pen = jnp.where(krow < TILE, jnp.where(c >= 2, 0.0, NEG),
                    jnp.where(krow < 2 * TILE, jnp.where(c >= 1, 0.0, NEG),
                              jnp.where(c <= nt - 2, 0.0, NEG)))
    qp = qp_ref[...]
    kcat = jnp.concatenate([kp_ref[...], kc_ref[...], kn_ref[...]], axis=0)
    s = _dot(kcat, qp) + bias_ref[...] + pen
    sm = _dot(km_ref[...], qp) + bmeta_ref[...]
    sk = sink_ref[...]
    m = jnp.maximum(jnp.maximum(jnp.max(s, axis=0, keepdims=True),
                                jnp.max(sm, axis=0, keepdims=True)), sk)
    p = jnp.exp2(s - m)
    pm = jnp.exp2(sm - m)
    inv = 1.0 / (jnp.sum(p, axis=0, keepdims=True) + jnp.sum(pm, axis=0, keepdims=True)
                 + jnp.exp2(sk - m))
    pb = p.astype(bf16)
    pmb = jnp.concatenate([jnp.zeros((TILE - N_META, 4 * TILE), bf16), pm.astype(bf16)], axis=0)
    vcat = jnp.concatenate([vp_ref[...], vc_ref[...], vn_ref[...]], axis=1)
    heads = []
    for g in range(2):
        rows = slice(HEAD_DIM * g, HEAD_DIM * (g + 1))
        cols = slice(2 * TILE * g, 2 * TILE * (g + 1))
        pv = (_dot(vcat[rows, :], pb[:, cols]) + _dot(vm_ref[rows, :], pmb[:, cols])) * inv[:, cols]
        heads += [pv[:, :TILE], pv[:, TILE:]]
    o = jnp.concatenate(heads, axis=0).T
    out_ref[...] = jnp.where(_valid_rows(c, TILE), o, 0.0).astype(out_ref.dtype)


def _swa(qp, k, vt, bias_band, bias_meta, sink, bsz, lp):
    n = k.shape[0]
    nt = lp // TILE
    per, _, tm = vt.shape[0] // bsz, vt.shape[1], vt.shape[2]
    sub = tm // TILE
    qp = qp.reshape(bsz, nt, 128, 4 * TILE)
    k = k.reshape(bsz, lp, 128)
    vt = vt.reshape(bsz, per, 128, tm)
    prev = lambda c: jnp.maximum(c - 1, 0)
    nxt = lambda c: jnp.minimum(c + 1, nt - 1)
    kspec = lambda tile: pl.BlockSpec((bsz, TILE, 128), lambda c: (0, tile(c), 0))
    vspec = lambda tile: pl.BlockSpec((bsz, None, 128, TILE), lambda c: (0, tile(c) // sub, 0, tile(c) % sub))
    same = lambda c: c
    first = lambda c: 0
    out = pl.pallas_call(
        _for_each_batch(functools.partial(_swa_body, nt), [True] * 9 + [False] * 3 + [True], 0),
        grid=(nt,),
        in_specs=[pl.BlockSpec((bsz, None, 128, 4 * TILE), lambda c: (0, c, 0, 0)),
                  kspec(prev), kspec(same), kspec(nxt),
                  pl.BlockSpec((bsz, N_META, 128), lambda c: (0, TILE // N_META - 1, 0)),
                  vspec(prev), vspec(same), vspec(nxt), vspec(first),
                  pl.BlockSpec((3 * TILE, 4 * TILE), lambda c: (0, 0)),
                  pl.BlockSpec((None, N_META, 4 * TILE), lambda c: (c, 0, 0)),
                  pl.BlockSpec((1, 4 * TILE), lambda c: (0, 0))],
        out_specs=pl.BlockSpec((bsz, TILE, 256), lambda c: (0, c, 0)),
        out_shape=jax.ShapeDtypeStruct((bsz, lp, 256), bf16),
        compiler_params=_cparams("arbitrary"),
        name="swa",
    )(qp, k, k, k, k, vt, vt, vt, vt, bias_band, bias_meta, sink)
    return out.reshape(n, 256)


def _flash_body(nk, tq, tk, qt_ref, k_ref, vt_ref, out_ref, qpad_ref, m_ref, l_ref, acc_ref,
                sa_ref, sb_ref, ca_ref, cb_ref):
    i = pl.program_id(1)
    buf_a, buf_b = (sa_ref, ca_ref), (sb_ref, cb_ref)
    krow = lax.broadcasted_iota(jnp.int32, (tk, 1), 0)
    qpad_ref[...] = jnp.zeros_like(qpad_ref)
    for hh in range(4):
        lo = HEAD_DIM * (hh // 2)
        qpad_ref[lo:lo + HEAD_DIM, tq * hh:tq * (hh + 1)] = qt_ref[HEAD_DIM * hh:HEAD_DIM * (hh + 1), :]
    m_ref[...] = jnp.full_like(m_ref, NEG)
    l_ref[...] = jnp.zeros_like(l_ref)
    acc_ref[...] = jnp.zeros_like(acc_ref)

    def scores(j, buf, first=False):
        s_ref, cmax_ref = buf
        s = _dot(k_ref[pl.ds(pl.multiple_of(j * tk, tk), tk), :], qpad_ref[...])
        if first:
            s = jnp.where(krow >= TILE - N_META, s, NEG)
        s_ref[...] = s
        cmax_ref[...] = jnp.max(s, axis=0, keepdims=True)

    def absorb(j, buf):
        s_ref, cmax_ref = buf
        s = s_ref[...]
        m_old = m_ref[...]
        m_new = jnp.maximum(m_old, cmax_ref[...])
        alpha = jnp.exp2(m_old - m_new)
        p = jnp.exp2(s - m_new)
        l_ref[...] = alpha * l_ref[...] + jnp.sum(p, axis=0, keepdims=True)
        m_ref[...] = m_new
        pb = p.astype(bf16)
        for g in range(2):
            pv = _dot(vt_ref[j, HEAD_DIM * g:HEAD_DIM * (g + 1), :], pb[:, 2 * g * tq:(2 * g + 2) * tq])
            for r in range(2):
                hh = 2 * g + r
                rows = slice(HEAD_DIM * hh, HEAD_DIM * (hh + 1))
                acc_ref[rows, :] = (alpha[:, tq * hh:tq * (hh + 1)] * acc_ref[rows, :]
                                    + pv[:, tq * r:tq * (r + 1)])

    scores(0, buf_a, first=True)

    def body(jj, carry):
        j = 2 * jj
        scores(j + 1, buf_b)
        absorb(j, buf_a)
        scores(j + 2, buf_a)
        absorb(j + 1, buf_b)
        return carry

    lax.fori_loop(0, (nk - 1) // 2, body, 0, unroll=True)
    if nk % 2 == 1:
        absorb(nk - 1, buf_a)
    else:
        scores(nk - 1, buf_b)
        absorb(nk - 2, buf_a)
        absorb(nk - 1, buf_b)
    linv = 1.0 / l_ref[...]
    for hh in range(4):
        rows = slice(HEAD_DIM * hh, HEAD_DIM * (hh + 1))
        acc_ref[rows, :] = acc_ref[rows, :] * linv[:, tq * hh:tq * (hh + 1)]
    rows = lax.broadcasted_iota(jnp.int32, (tq, 1), 0)
    valid = jnp.logical_or(i > 0, rows >= TILE - N_META)
    out_ref[...] = jnp.where(valid, acc_ref[...].T, 0.0).astype(out_ref.dtype)


def _flash(qt, k, vt, bsz, lp):
    n = k.shape[0]
    tk = qt.shape[2]
    tq = FLASH_TQ if tk % FLASH_TQ == 0 else tk
    sub = tk // tq
    per = lp // tq
    nk = lp // tk
    return pl.pallas_call(
        functools.partial(_flash_body, nk, tq, tk),
        grid=(bsz, per),
        in_specs=[pl.BlockSpec((None, 256, tq), lambda b, i: (b * nk + i // sub, 0, i % sub)),
                  pl.BlockSpec((lp, 128), lambda b, i: (b, 0)),
                  pl.BlockSpec((nk, 128, tk), lambda b, i: (b, 0, 0))],
        out_specs=pl.BlockSpec((tq, 256), lambda b, i: (b * per + i, 0)),
        out_shape=jax.ShapeDtypeStruct((n, 256), bf16),
        scratch_shapes=[pltpu.VMEM((128, 4 * tq), bf16), pltpu.VMEM((1, 4 * tq), f32),
                        pltpu.VMEM((1, 4 * tq), f32), pltpu.VMEM((256, tq), f32),
                        pltpu.VMEM((tk, 4 * tq), f32), pltpu.VMEM((tk, 4 * tq), f32),
                        pltpu.VMEM((1, 4 * tq), f32), pltpu.VMEM((1, 4 * tq), f32)],
        compiler_params=_cparams("parallel", "parallel"),
        name="gqa_full",
    )(qt, k, vt)


def _mix_residual(y_refs, h_ref, wo_ref):
    mixed = jnp.concatenate([y_ref[...] for y_ref in y_refs], axis=1)
    return h_ref[...] + _dot(mixed, wo_ref[...])


def _ffn_body(y0_ref, y1_ref, y2_ref, y3_ref, h_ref, wo_ref, nw_ref, wg_ref, wu_ref, wd_ref,
              out_ref, hn_ref, u_ref, acc_ref):
    j = pl.program_id(1)

    @pl.when(j == 0)
    def _():
        hn = _mix_residual((y0_ref, y1_ref, y2_ref, y3_ref), h_ref, wo_ref)
        hn_ref[...] = hn
        u_ref[...] = _rms(hn, nw_ref[...]).astype(bf16)
        acc_ref[...] = jnp.zeros_like(acc_ref)

    u = u_ref[...]
    t = _silu(_dot(u, wg_ref[...])) * _dot(u, wu_ref[...])
    acc_ref[...] += _dot(t.astype(bf16), wd_ref[...])

    @pl.when(j == pl.num_programs(1) - 1)
    def _():
        out_ref[...] = hn_ref[...] + acc_ref[...]


def _ffn(ys, h, wo, nw, wg, wu, wd):
    n, d = h.shape
    ff = wg.shape[1]
    tm = TILE * _largest_divisor(n // TILE, 6)
    tf = 128 * _largest_divisor(ff // 128, 11)
    row = lambda i, j: (i, 0)
    const = lambda i, j: (0, 0)
    return pl.pallas_call(
        _ffn_body,
        grid=(n // tm, ff // tf),
        in_specs=[pl.BlockSpec((tm, 256), row)] * 4
        + [pl.BlockSpec((tm, d), row), pl.BlockSpec((d, d), const), pl.BlockSpec((1, d), const),
           pl.BlockSpec((d, tf), lambda i, j: (0, j)), pl.BlockSpec((d, tf), lambda i, j: (0, j)),
           pl.BlockSpec((tf, d), lambda i, j: (j, 0))],
        out_specs=pl.BlockSpec((tm, d), row),
        out_shape=jax.ShapeDtypeStruct((n, d), f32),
        scratch_shapes=[pltpu.VMEM((tm, d), f32), pltpu.VMEM((tm, d), bf16), pltpu.VMEM((tm, d), f32)],
        compiler_params=_cparams("parallel", "arbitrary"),
        name="ffn",
    )(*ys, h, wo, nw, wg, wu, wd)


MOE_TM = 1024
ROUTE_E, ROUTE_RANK, ROUTE_GATE, ROUTE_W = 0, 2, 4, 8


def _router_body(y0_ref, y1_ref, y2_ref, y3_ref, h_ref, wo_ref, nw_ref, r_ref, strict_ref,
                 hn_ref, route_ref, cnt_ref, base_ref):
    @pl.when(pl.program_id(0) == 0)
    def _():
        base_ref[...] = jnp.zeros_like(base_ref)

    hn = _mix_residual((y0_ref, y1_ref, y2_ref, y3_ref), h_ref, wo_ref)
    hn_ref[...] = hn
    u = _rms(hn, nw_ref[...])
    u_hi, u_lo = _split(u)
    r_hi, r_lo = _split(r_ref[...])
    logits = _dot(u_hi, r_hi) + _dot(u_lo, r_hi) + _dot(u_hi, r_lo)
    tm = logits.shape[0]
    lane = lax.broadcasted_iota(jnp.int32, logits.shape, 1)
    logits = jnp.where(lane < N_EXPERTS, logits, NEG)
    m1 = jnp.max(logits, axis=-1, keepdims=True)
    i1 = jnp.min(jnp.where(logits == m1, lane, 128), axis=-1, keepdims=True)
    rest = jnp.where(lane == i1, NEG, logits)
    m2 = jnp.max(rest, axis=-1, keepdims=True)
    i2 = jnp.min(jnp.where(rest == m2, lane, 128), axis=-1, keepdims=True)
    e2 = jnp.exp(m2 - m1)
    g1 = 1.0 / (1.0 + e2)
    g2 = e2 * g1

    sel1 = lane == i1
    sel2 = lane == i2
    onehot = jnp.where(sel1, 1.0, jnp.where(sel2, 1.0, 0.0))
    before = _dot(strict_ref[...], onehot.astype(bf16)) + base_ref[0:1, :]
    r1 = jnp.sum(jnp.where(sel1, before, 0.0), axis=-1, keepdims=True)
    r2 = jnp.sum(jnp.where(sel2, before, 0.0), axis=-1, keepdims=True)
    route = jnp.zeros(logits.shape, f32)
    for k, val in enumerate((i1.astype(f32), i2.astype(f32), r1, r2, g1, g2)):
        route = jnp.where(lane == k, val, route)
    route_ref[...] = route[:, :ROUTE_W]
    base_ref[0:1, :] = base_ref[0:1, :] + jnp.sum(onehot, axis=0, keepdims=True)
    cnt_ref[...] = base_ref[...]


def _router(ys, h, wo, nw, router):
    n, d = h.shape
    tm = TILE * _largest_divisor(n // TILE, 8)
    row = lambda i: (i, 0)
    const = lambda i: (0, 0)
    strict = jnp.asarray(np.tril(np.ones((tm, tm), np.float32), -1), bf16)
    return pl.pallas_call(
        _router_body,
        grid=(n // tm,),
        in_specs=[pl.BlockSpec((tm, 256), row)] * 4
        + [pl.BlockSpec((tm, d), row), pl.BlockSpec((d, d), const), pl.BlockSpec((1, d), const),
           pl.BlockSpec((d, 128), const), pl.BlockSpec((tm, tm), const)],
        out_specs=[pl.BlockSpec((tm, d), row), pl.BlockSpec((tm, ROUTE_W), row), pl.BlockSpec((8, 128), const)],
        out_shape=[jax.ShapeDtypeStruct((n, d), f32), jax.ShapeDtypeStruct((n, ROUTE_W), f32),
                   jax.ShapeDtypeStruct((8, 128), f32)],
        scratch_shapes=[pltpu.VMEM((8, 128), f32)],
        compiler_params=_cparams("arbitrary"),
        name="moe_router",
    )(*ys, h, wo, nw, router, strict)


LANES = 128
ROW_DMA_UNROLL = 8


def _row_copy(src_ref, src_row, dst_ref, dst_row, sem):
    return pltpu.make_async_copy(src_ref.at[src_row], dst_ref.at[dst_row], sem)


def _to_slabs(dst_ref, val):
    dst_ref[...] = val.reshape(dst_ref.shape)


def _from_slabs(src_ref):
    rows, chunks, lanes = src_ref.shape
    return src_ref[...].reshape(rows, chunks * lanes)


def _dispatch_body(fill_ref, h_ref, nw_ref, dest_ref, xs_ref, u_ref, idx_ref, zero_ref,
                   sem_idx, sem_row, sem_fill):
    i = pl.program_id(0)
    tt = u_ref.shape[1]

    @pl.when(i == 0)
    def _():
        zero_ref[...] = jnp.zeros_like(zero_ref)
        fills = [pltpu.make_async_copy(zero_ref, xs_ref.at[pl.ds(fill_ref[e], MOE_TM)], sem_fill)
                 for e in range(N_EXPERTS)]
        for cp in fills:
            cp.start()
        for cp in fills:
            cp.wait()
        last = xs_ref.shape[0] // MOE_TM - 1
        for j in range(last - N_EXPERTS, last + 1):
            @pl.when(j >= fill_ref[N_EXPERTS])
            def _():
                cp = pltpu.make_async_copy(zero_ref, xs_ref.at[pl.ds(j * MOE_TM, MOE_TM)], sem_fill)
                cp.start()
                cp.wait()

    slot = i % 2
    rows_ref = u_ref.at[slot]
    sem = sem_row.at[slot]
    idx_copy = pltpu.make_async_copy(dest_ref.at[pl.ds(i * 2 * tt, 2 * tt)], idx_ref, sem_idx)
    idx_copy.start()
    _to_slabs(rows_ref, _rms(h_ref[...], nw_ref[...]))
    idx_copy.wait()

    def issue(r, carry):
        _row_copy(rows_ref, r, xs_ref, idx_ref[r], sem).start()
        _row_copy(rows_ref, r, xs_ref, idx_ref[tt + r], sem).start()
        return carry

    lax.fori_loop(0, tt, issue, 0, unroll=ROW_DMA_UNROLL)

    def drain(s):
        for _ in range(2):
            pltpu.make_async_copy(u_ref.at[s], xs_ref.at[pl.ds(0, tt)], sem_row.at[s]).wait()

    @pl.when(i > 0)
    def _():
        drain(1 - slot)

    @pl.when(i == pl.num_programs(0) - 1)
    def _():
        drain(slot)


def _dispatch(h, nw, dest_flat, fill_rows, rows, tt):
    n, d = h.shape
    return pl.pallas_call(
        _dispatch_body,
        grid_spec=pltpu.PrefetchScalarGridSpec(
            num_scalar_prefetch=1,
            grid=(n // tt,),
            in_specs=[pl.BlockSpec((tt, d), lambda i, fr: (i, 0)), pl.BlockSpec((1, d), lambda i, fr: (0, 0)),
                      pl.BlockSpec(memory_space=pl.ANY)],
            out_specs=pl.BlockSpec(memory_space=pl.ANY),
            scratch_shapes=[pltpu.VMEM((2, tt, d // LANES, LANES), f32), pltpu.SMEM((2 * tt,), jnp.int32),
                            pltpu.VMEM((MOE_TM, d // LANES, LANES), f32), pltpu.SemaphoreType.DMA(()),
                            pltpu.SemaphoreType.DMA((2,)), pltpu.SemaphoreType.DMA(())]),
        out_shape=jax.ShapeDtypeStruct((rows, d // LANES, LANES), f32),
        compiler_params=_cparams("arbitrary"),
        name="moe_dispatch",
    )(fill_rows, h, nw, dest_flat)


def _experts_body(te_ref, nu_ref, x_ref, wg_ref, wu_ref, wd_ref, y_ref, xb_ref, acc_ref):
    j = pl.program_id(0)
    f = pl.program_id(1)

    @pl.when(j < nu_ref[0])
    def _():
        @pl.when(f == 0)
        def _():
            xb_ref[...] = _from_slabs(x_ref).astype(bf16)
            acc_ref[...] = jnp.zeros_like(acc_ref)

        x = xb_ref[...]
        t = _silu(_dot(x, wg_ref[...])) * _dot(x, wu_ref[...])
        acc_ref[...] += _dot(t.astype(bf16), wd_ref[...])

        @pl.when(f == pl.num_programs(1) - 1)
        def _():
            _to_slabs(y_ref, acc_ref[...])

    @pl.when(jnp.logical_and(j >= nu_ref[0], f == pl.num_programs(1) - 1))
    def _():
        y_ref[...] = jnp.zeros_like(y_ref)


def _experts(xs, tile_expert, n_used, wg, wu, wd, n_tiles):
    d = wg.shape[1]
    slab = (MOE_TM, d // LANES, LANES)
    ff = wg.shape[2]
    tf = 256 * _largest_divisor(ff // 256, 2)
    nf = ff // tf
    tile = lambda j, nu: jnp.minimum(j, nu[0] - 1)
    chunk = lambda j, f, nu: jnp.where(j < nu[0], f, nf - 1)
    return pl.pallas_call(
        _experts_body,
        grid_spec=pltpu.PrefetchScalarGridSpec(
            num_scalar_prefetch=2,
            grid=(n_tiles, nf),
            in_specs=[pl.BlockSpec(slab, lambda j, f, te, nu: (tile(j, nu), 0, 0)),
                      pl.BlockSpec((None, d, tf), lambda j, f, te, nu: (te[tile(j, nu)], 0, chunk(j, f, nu))),
                      pl.BlockSpec((None, d, tf), lambda j, f, te, nu: (te[tile(j, nu)], 0, chunk(j, f, nu))),
                      pl.BlockSpec((None, tf, d), lambda j, f, te, nu: (te[tile(j, nu)], chunk(j, f, nu), 0))],
            out_specs=pl.BlockSpec(slab, lambda j, f, te, nu: (j, 0, 0)),
            scratch_shapes=[pltpu.VMEM((MOE_TM, d), bf16), pltpu.VMEM((MOE_TM, d), f32)]),
        out_shape=jax.ShapeDtypeStruct((n_tiles * MOE_TM, d // LANES, LANES), f32),
        compiler_params=_cparams("arbitrary", "arbitrary"),
        name="moe_experts",
    )(tile_expert, n_used, xs, wg, wu, wd)


def _combine_body(final, h_ref, route_ref, fnw_ref, dest_ref, ys_ref, out_ref, buf_ref, idx_ref,
                  sem_idx, sem_row):
    i = pl.program_id(0)
    tt = h_ref.shape[0]
    slot = i % 2

    def gather(step, s):
        idx_copy = pltpu.make_async_copy(dest_ref.at[pl.ds(step * 2 * tt, 2 * tt)], idx_ref, sem_idx)
        idx_copy.start()
        idx_copy.wait()

        def issue(r, carry):
            _row_copy(ys_ref, idx_ref[r], buf_ref.at[s, 0], r, sem_row.at[s]).start()
            _row_copy(ys_ref, idx_ref[tt + r], buf_ref.at[s, 1], r, sem_row.at[s]).start()
            return carry

        lax.fori_loop(0, tt, issue, 0, unroll=ROW_DMA_UNROLL)

    @pl.when(i == 0)
    def _():
        gather(0, 0)

    @pl.when(i + 1 < pl.num_programs(0))
    def _():
        gather(i + 1, 1 - slot)

    for k in range(2):
        pltpu.make_async_copy(ys_ref.at[pl.ds(0, tt)], buf_ref.at[slot, k], sem_row.at[slot]).wait()
    route = route_ref[...]
    g1 = route[:, ROUTE_GATE:ROUTE_GATE + 1]
    g2 = route[:, ROUTE_GATE + 1:ROUTE_GATE + 2]
    y = h_ref[...] + g1 * _from_slabs(buf_ref.at[slot, 0]) + g2 * _from_slabs(buf_ref.at[slot, 1])
    out_ref[...] = _rms(y, fnw_ref[...]) if final else y


def _combine(h, route, fnw, dest_flat, ys, tt, final):
    n, d = h.shape
    return pl.pallas_call(
        functools.partial(_combine_body, final),
        grid=(n // tt,),
        in_specs=[pl.BlockSpec((tt, d), lambda i: (i, 0)), pl.BlockSpec((tt, ROUTE_W), lambda i: (i, 0)),
                  pl.BlockSpec((1, d), lambda i: (0, 0)),
                  pl.BlockSpec(memory_space=pl.ANY), pl.BlockSpec(memory_space=pl.ANY)],
        out_specs=pl.BlockSpec((tt, d), lambda i: (i, 0)),
        out_shape=jax.ShapeDtypeStruct((n, d), f32),
        scratch_shapes=[pltpu.VMEM((2, 2, tt, d // LANES, LANES), f32), pltpu.SMEM((2 * tt,), jnp.int32),
                        pltpu.SemaphoreType.DMA(()), pltpu.SemaphoreType.DMA((2,))],
        compiler_params=_cparams("arbitrary"),
        name="moe_combine",
    )(h, route, fnw, dest_flat, ys)


def _moe(mix, h, wo, nw, router, wg, wu, wd, fnw, final):
    n, d = h.shape
    tt = TILE * _largest_divisor(n // TILE, 4)
    h, route, counts = _router(mix, h, wo, nw, router)

    cnt = counts[0, :N_EXPERTS].astype(jnp.int32)
    padded = (cnt + MOE_TM - 1) // MOE_TM * MOE_TM
    ends = jnp.cumsum(padded)
    off = ends - padded
    n_tiles = -(-2 * n // MOE_TM) + N_EXPERTS
    tile_expert = jnp.minimum(
        jnp.sum((jnp.arange(n_tiles)[:, None] * MOE_TM >= ends[None, :]).astype(jnp.int32), axis=1),
        N_EXPERTS - 1).astype(jnp.int32)
    n_used = (ends[-1:] // MOE_TM).astype(jnp.int32)
    sel = route[:, ROUTE_E:ROUTE_E + 2].astype(jnp.int32)
    rank = route[:, ROUTE_RANK:ROUTE_RANK + 2].astype(jnp.int32)
    dest = jnp.sum(jnp.where(sel[..., None] == jnp.arange(N_EXPERTS), off, 0), axis=-1) + rank
    dest_flat = dest.reshape(n // tt, tt, 2).transpose(0, 2, 1).reshape(-1)

    fill_rows = jnp.concatenate([off + cnt, n_used]).astype(jnp.int32)
    xs = _dispatch(h, nw, dest_flat, fill_rows, (n_tiles + 1) * MOE_TM, tt)
    ys = _experts(xs, tile_expert, n_used, wg, wu, wd, n_tiles)
    return _combine(h, route, fnw, dest_flat, ys, tt, final)


def _final_norm_body(h_ref, w_ref, out_ref):
    out_ref[...] = _rms(h_ref[...], w_ref[...])


def _final_norm(h, w):
    n, d = h.shape
    tm = TILE * _largest_divisor(n // TILE, 8)
    return pl.pallas_call(
        _final_norm_body,
        grid=(n // tm,),
        in_specs=[pl.BlockSpec((tm, d), lambda i: (i, 0)), pl.BlockSpec((1, d), lambda i: (0, 0))],
        out_specs=pl.BlockSpec((tm, d), lambda i: (i, 0)),
        out_shape=jax.ShapeDtypeStruct((n, d), f32),
        compiler_params=_cparams("parallel"),
        name="final_norm",
    )(h, w)


def _rope_tables(seq, pad):
    t = np.arange(seq)
    meta_pos = np.arange(N_META) - N_META
    row = np.concatenate([meta_pos, t // GRID_W]).astype(np.float32)
    col = np.concatenate([meta_pos, t % GRID_W]).astype(np.float32)
    half = HEAD_DIM // 2
    inv = np.float32(ROPE_THETA) ** (-np.arange(0, half, 2, dtype=np.float32) / np.float32(half))
    ang = np.concatenate([row[:, None] * inv, col[:, None] * inv], axis=-1)
    ang = np.tile(np.repeat(ang, 2, axis=-1), (1, QK_WIDTH // HEAD_DIM))
    ang = np.pad(ang, ((pad, 0), (0, 0))).astype(np.float32)
    return jnp.asarray(np.cos(ang)), jnp.asarray(np.sin(ang))


def _pair_swap_matrix(width):
    i = jnp.arange(width)
    p = jnp.zeros((width, width), f32)
    p = p.at[i[1::2], i[0::2]].set(-1.0)
    p = p.at[i[0::2], i[1::2]].set(1.0)
    return p.astype(bf16)


def _group_mean_matrix(width, group):
    i = jnp.arange(width)
    return ((i[:, None] // group == i[None, :] // group).astype(f32) / group).astype(bf16)


def _t5_bucket(rel):
    nb = REL_BUCKETS // 2
    max_exact = nb // 2
    ret = (rel > 0).astype(jnp.int32) * nb
    n = jnp.abs(rel)
    nf = jnp.maximum(n, 1).astype(f32)
    large = max_exact + (jnp.log(nf / max_exact) / math.log(REL_MAX_DIST / max_exact)
                         * (nb - max_exact)).astype(jnp.int32)
    large = jnp.minimum(large, nb - 1)
    return ret + jnp.where(n < max_exact, n, large)


def _swa_bias_tables(rel_bias, lp):
    def lookup(bucket):
        out = jnp.zeros((rel_bias.shape[1],) + bucket.shape, f32)
        for b in range(REL_BUCKETS):
            out = jnp.where((bucket == b)[None], rel_bias[b].astype(f32)[:, None, None], out)
        return out

    qi = jnp.arange(TILE)
    ki = jnp.arange(3 * TILE)
    rel = ki[None, :] - TILE - qi[:, None]
    band = jnp.where((jnp.abs(rel) <= WINDOW)[None], lookup(_t5_bucket(rel)), NEG)
    pos = jnp.arange(lp) - (TILE - N_META)
    rel_m = jnp.arange(N_META)[None, :] - pos[:, None]
    meta = lookup(_t5_bucket(rel_m))
    nt = lp // TILE
    band_t = jnp.transpose(band, (2, 0, 1)).reshape(3 * TILE, 4 * TILE) * LOG2E
    meta_t = (meta.reshape(4, nt, TILE, N_META).transpose(1, 3, 0, 2).reshape(nt, N_META, 4 * TILE)
              * LOG2E)
    return band_t, meta_t


def _row(v, width=None):
    v = v.astype(f32).reshape(1, -1)
    if width is not None and v.shape[1] < width:
        v = jnp.pad(v, ((0, 0), (0, width - v.shape[1])))
    return v


def kernel(x, meta_tokens, rel_bias, norm_mix_w, norm_ffn_w, w_in, ssd_conv_w, ssd_conv_b, ssd_dt_bias, ssd_a_log, ssd_d, ssd_norm_w, gla_gate_w2, gla_gate_b, gla_norm_w, swa_sink, gqa_q_norm_w, gqa_k_norm_w, w_out, ffn_w_gate, ffn_w_up, ffn_w_down, moe_router, moe_w_gate, moe_w_up, moe_w_down, final_norm_w):
    bsz, seq, d = x.shape
    depth = w_in.shape[0]
    pad = (-(seq + N_META)) % TILE
    assert pad == TILE - N_META and seq % TILE == 0
    lp = pad + N_META + seq
    n = bsz * lp

    meta = jnp.broadcast_to(meta_tokens[None].astype(x.dtype), (bsz, N_META, d))
    h = jnp.concatenate([jnp.zeros((bsz, pad, d), x.dtype), meta, x], axis=1).reshape(n, d)

    cos, sin = _rope_tables(seq, pad)
    rot = _pair_swap_matrix(QK_WIDTH)
    gavg = _group_mean_matrix(QK_WIDTH, HEAD_DIM)
    bias_band, bias_meta = _swa_bias_tables(rel_bias, lp)
    offs = [0]
    for s in IN_SIZES:
        offs.append(offs[-1] + s)

    for i in range(depth):
        wi = w_in[i]
        cols = [wi[:, offs[j]:offs[j + 1]] for j in PACK_ORDER]
        w_pack = jnp.concatenate(cols + [jnp.zeros((d, PACK_WIDTH - offs[-1]), wi.dtype)], axis=1).astype(bf16)
        o_ssd, o_small, o_gla, sk, ak, aq, av, sq, sv = _inproj(
            h, _row(norm_mix_w[i]), w_pack, cos, sin, rot, gavg,
            _row(jnp.concatenate([jnp.tile(gqa_q_norm_w[i], 4), jnp.tile(gqa_k_norm_w[i], 2)])), bsz, lp)

        convw = jnp.pad(ssd_conv_w[i].astype(f32), ((0, 8 - SSD_CONV), (0, 0)))
        convb = _row(ssd_conv_b[i])
        dtb = _row(ssd_dt_bias[i].reshape(-1), 128)
        alog = _row(ssd_a_log[i].reshape(-1), 128)
        yf, xbc = _ssd(False, o_ssd, o_small, convw, convb, dtb, alog,
                       _row(jnp.repeat(ssd_d[i], SSD_HEAD_DIM)), bsz, lp)
        y_ssd = _ssd(True, o_ssd, o_small, convw, convb, dtb, alog, (yf, xbc, _row(ssd_norm_w[i])), bsz, lp)

        def gate_w(direction):
            lo = SMALL_GA + GLA_GATE_RANK * direction
            full = jnp.zeros((128, GLA_KEY), f32).at[lo:lo + GLA_GATE_RANK].set(gla_gate_w2[i, direction].astype(f32))
            return full.astype(bf16)

        of = _gla(False, o_gla, o_small, gate_w(0), _row(gla_gate_b[i, 0]), None, bsz, lp)
        y_gla = _gla(True, o_gla, o_small, gate_w(1), _row(gla_gate_b[i, 1]),
                     (of, _row(jnp.tile(gla_norm_w[i], GLA_HEADS)), gavg), bsz, lp)

        sink = _row(jnp.repeat(swa_sink[i].astype(f32), TILE)) * LOG2E
        y_swa = _swa(sq, sk, sv, bias_band, bias_meta, sink, bsz, lp)
        y_g2 = _flash(aq, ak, av, bsz, lp)

        mix = (y_ssd, y_gla, y_swa, y_g2)
        wo = w_out[i].astype(bf16)
        j = i // 2
        if i % 2 == 0:
            h = _ffn(mix, h, wo, _row(norm_ffn_w[i]), ffn_w_gate[j].astype(bf16), ffn_w_up[j].astype(bf16),
                     ffn_w_down[j].astype(bf16))
            if i == depth - 1:
                h = _final_norm(h, _row(final_norm_w))
        else:
            router = jnp.pad(moe_router[j].astype(f32), ((0, 0), (0, 128 - N_EXPERTS)))
            h = _moe(mix, h, wo, _row(norm_ffn_w[i]), router, moe_w_gate[j].astype(bf16),
                     moe_w_up[j].astype(bf16), moe_w_down[j].astype(bf16), _row(final_norm_w), i == depth - 1)
    return h.reshape(bsz, lp, d)[:, pad + N_META:]
```

```python
import functools
import math

import jax
import jax.numpy as jnp
import numpy as np
from jax import lax
from jax.experimental import pallas as pl
from jax.experimental.pallas import tpu as pltpu

f32 = jnp.float32
bf16 = jnp.bfloat16

N_META = 16
HEAD_DIM = 64
GRID_W = 64
EPS = 1e-6
ROPE_THETA = 10000.0
TILE = 128
SSD_HEADS = 4
SSD_HEAD_DIM = 64
SSD_INNER = 256
SSD_STATE = 128
SSD_CONV = 5
SSD_CONV_DIM = 768
GLA_HEADS = 4
GLA_DK = 32
GLA_DV = 64
GLA_KEY = 128
GLA_VAL = 256
GLA_GATE_RANK = 16
GLA_GATE_NORM = 16.0
GLA_CHUNK = 64
WINDOW = 128
REL_BUCKETS = 32
REL_MAX_DIST = 128
N_EXPERTS = 8
NEG = -1e30
LOG2E = math.log2(math.e)
QK_WIDTH = 384
FLASH_TQ = 384
CONV_HALO = 16
VMEM_LIMIT = 56 * 1024 * 1024

IN_SIZES = (256, 768, 8, 128, 128, 256, 256, 32, 256, 128, 128, 256, 128, 128)
(_Z, _XBC, _DT, _GQ, _GK, _GV, _GR, _GA, _SQ, _SK, _SV, _AQ, _AK, _AV) = range(14)
PACK_ORDER = (_XBC, _Z, _GQ, _GK, _GV, _GR, _SQ, _SK, _SV, _AQ, _AK, _AV, _DT, _GA)
PACK_WIDTH = 2944
C_SSD, C_GLA, C_SWA, C_AQ, C_AK, C_AV, C_SMALL = 0, 1024, 1792, 2304, 2560, 2688, 2816
SMALL_DT, SMALL_GA = 0, 8


def _cparams(*sem):
    return pltpu.CompilerParams(dimension_semantics=sem, vmem_limit_bytes=VMEM_LIMIT)


def _dot(a, b):
    return jnp.dot(a, b, preferred_element_type=f32)


def _dot_nt(a, b):
    return lax.dot_general(a, b, (((1,), (1,)), ((), ())), preferred_element_type=f32)


def _dot_tn(a, b):
    return lax.dot_general(a, b, (((0,), (0,)), ((), ())), preferred_element_type=f32)


def _split(a):
    hi = a.astype(bf16)
    lo = (a - hi.astype(f32)).astype(bf16)
    return hi, lo


def _dot_split_lhs(a, b):
    hi, lo = _split(a)
    return _dot(hi, b) + _dot(lo, b)


def _dot_split_rhs(t, x):
    hi, lo = _split(x)
    return _dot(t, hi) + _dot(t, lo)


def _rms(x, w):
    return x * lax.rsqrt(jnp.mean(x * x, axis=-1, keepdims=True) + EPS) * w


def _silu(x):
    return x * (0.5 + 0.5 * jnp.tanh(0.5 * x))


def _softplus(x):
    return jnp.maximum(x, 0.0) + jnp.log(1.0 + jnp.exp(-jnp.abs(x)))


def _log_sigmoid(x):
    return jnp.minimum(x, 0.0) - jnp.log(1.0 + jnp.exp(-jnp.abs(x)))


def _tri(n, rev):
    r = lax.broadcasted_iota(jnp.int32, (n, n), 0)
    c = lax.broadcasted_iota(jnp.int32, (n, n), 1)
    return (r <= c) if rev else (r >= c)


def _valid_rows(tile, n):
    rows = lax.broadcasted_iota(jnp.int32, (n, 1), 0)
    return jnp.logical_or(tile > 0, rows >= TILE - N_META)


def _inproj_body(h_ref, nw_ref, w_ref, cos_ref, sin_ref, rot_ref, gavg_ref, qkw_ref,
                 ssd_ref, small_ref, gla_ref, sk_ref, k_ref, q_ref, v_ref, sq_ref, sv_ref):
    u = _rms(h_ref[...], nw_ref[...]).astype(bf16)

    def mm(lo, hi):
        return _dot(u, w_ref[:, lo:hi])

    ssd_ref[...] = mm(C_SSD, C_GLA).astype(bf16)
    gla_ref[...] = mm(C_GLA, C_SWA).astype(bf16)
    small_ref[...] = mm(C_SMALL, PACK_WIDTH)
    akv = mm(C_AK, C_SMALL)
    v_ref[...] = akv[:, 128:].T.astype(bf16)

    skv = mm(C_SWA + 256, C_AQ)
    sk_ref[...] = skv[:, :128].astype(bf16)
    sv_ref[...] = skv[:, 128:].T.astype(bf16)
    sqt = (mm(C_SWA, C_SWA + 256) * (HEAD_DIM ** -0.5 * LOG2E)).T.astype(bf16)
    sq_ref[...] = jnp.zeros_like(sq_ref)
    for t in range(sq_ref.shape[0]):
        for hh in range(4):
            lo = HEAD_DIM * (hh // 2)
            sq_ref[t, lo:lo + HEAD_DIM, TILE * hh:TILE * (hh + 1)] = (
                sqt[HEAD_DIM * hh:HEAD_DIM * (hh + 1), TILE * t:TILE * (t + 1)])

    t = jnp.concatenate([mm(C_AQ, C_AK), akv[:, :128]], axis=1)
    ms = _dot_split_lhs(t * t, gavg_ref[...])
    tn = t * lax.rsqrt(ms + EPS) * qkw_ref[...]
    tr = _dot(tn.astype(bf16), rot_ref[...])
    qk = tn * cos_ref[...] + tr * sin_ref[...]
    q_ref[...] = (qk[:, :256] * (HEAD_DIM ** -0.5 * LOG2E)).T.astype(bf16)
    k_ref[...] = qk[:, 256:].astype(bf16)


def _inproj(h, nw, w, cos, sin, rot, gavg, qkw, bsz, lp):
    n = h.shape[0]
    d = h.shape[1]
    tm = TILE * _largest_divisor(lp // TILE, 3)
    per = lp // tm
    row = lambda b, i: (b * per + i, 0)
    const = lambda b, i: (0, 0)
    tab = lambda b, i: (i, 0)
    outs = ((1024, bf16), (128, f32), (768, bf16), (128, bf16), (128, bf16))
    slab = lambda b, i: (b * per + i, 0, 0)
    sub = tm // TILE
    return pl.pallas_call(
        _inproj_body,
        grid=(bsz, per),
        in_specs=[pl.BlockSpec((tm, d), row), pl.BlockSpec((1, d), const),
                  pl.BlockSpec((d, PACK_WIDTH), const),
                  pl.BlockSpec((tm, QK_WIDTH), tab), pl.BlockSpec((tm, QK_WIDTH), tab),
                  pl.BlockSpec((QK_WIDTH, QK_WIDTH), const), pl.BlockSpec((QK_WIDTH, QK_WIDTH), const),
                  pl.BlockSpec((1, QK_WIDTH), const)],
        out_specs=([pl.BlockSpec((tm, c), row) for c, _ in outs]
                   + [pl.BlockSpec((None, 256, tm), slab), pl.BlockSpec((None, 128, tm), slab),
                      pl.BlockSpec((sub, 128, 4 * TILE), slab), pl.BlockSpec((None, 128, tm), slab)]),
        out_shape=([jax.ShapeDtypeStruct((n, c), t) for c, t in outs]
                   + [jax.ShapeDtypeStruct((n // tm, 256, tm), bf16),
                      jax.ShapeDtypeStruct((n // tm, 128, tm), bf16),
                      jax.ShapeDtypeStruct((n // TILE, 128, 4 * TILE), bf16),
                      jax.ShapeDtypeStruct((n // tm, 128, tm), bf16)]),
        compiler_params=_cparams("parallel", "parallel"),
        name="inproj",
    )(h, nw, w, cos, sin, rot, gavg, qkw)


def _largest_divisor(n, cap):
    return max(k for k in range(1, cap + 1) if n % k == 0)


def _for_each_batch(chain, batched, n_scratch):
    def body(*refs):
        nb = next(r.shape[0] for r, flag in zip(refs, batched) if flag)
        io, scratch = refs[:len(batched)], refs[len(batched):]
        assert len(scratch) == n_scratch * nb
        for b in range(nb):
            chain(*[r.at[b] if flag else r for r, flag in zip(io, batched)],
                  *[scratch[k * nb + b] for k in range(n_scratch)])
    return body


def _per_batch_scratch(nb, *shapes):
    return [pltpu.VMEM(shape, dtype) for shape, dtype in shapes for _ in range(nb)]


def _ssd_scan_tile(rev, xs, bmat, cmat, dt, cum, tot, st):
    col0 = SMALL_DT + (SSD_HEADS if rev else 0)
    causal = _tri(TILE, rev)
    cum_t = cum.T
    e_cum = jnp.exp(cum)
    e_tot = jnp.exp(tot)
    head_of = lax.broadcasted_iota(jnp.int32, (TILE, SSD_INNER), 1) // SSD_HEAD_DIM
    head_of_row = lax.broadcasted_iota(jnp.int32, (1, SSD_INNER), 1) // SSD_HEAD_DIM

    def widen(cols, like):
        out = jnp.zeros(like.shape, f32)
        for hh in range(SSD_HEADS):
            out = jnp.where(like == hh, cols[:, col0 + hh:col0 + hh + 1], out)
        return out

    def own_head(r):
        out = jnp.zeros((TILE, SSD_INNER), f32)
        for hh in range(SSD_HEADS):
            out = jnp.where(head_of == hh, r[TILE * hh:TILE * (hh + 1)], out)
        return out

    xd = (xs * widen(dt, head_of)).astype(bf16)
    scores, bws = [], []
    for g in range(2):
        bg = bmat[:, SSD_STATE * g:SSD_STATE * (g + 1)]
        cbg = _dot_nt(cmat[:, SSD_STATE * g:SSD_STATE * (g + 1)], bg)
        for hh in (2 * g, 2 * g + 1):
            col = col0 + hh
            a_col = cum[:, col:col + 1]
            decay = jnp.where(causal, jnp.exp(a_col - cum_t[col:col + 1, :]), 0.0)
            scores.append((cbg * decay).astype(bf16))
            bws.append((bg.astype(f32) * jnp.exp(tot[:, col:col + 1] - a_col)).astype(bf16))
    y = own_head(_dot(jnp.concatenate(scores, axis=0), xd))
    y = y + _dot(cmat, st.astype(bf16)) * widen(e_cum, head_of)
    upd = _dot_tn(jnp.concatenate(bws, axis=1), xd)
    rows = [jnp.where(head_of == 2 * g, upd[SSD_STATE * 2 * g:SSD_STATE * (2 * g + 1)],
                      jnp.where(head_of == 2 * g + 1,
                                upd[SSD_STATE * (2 * g + 1):SSD_STATE * (2 * g + 2)], 0.0))
            for g in range(2)]
    return y, st * widen(e_tot, head_of_row) + jnp.concatenate(rows, axis=0)


def _ssd_body(rev, nt, *refs):
    if rev:
        xbc_ref, small_ref, dtb_ref, alog_ref, yf_ref, z_ref, normw_ref, out_ref, state_ref = refs
    else:
        (cur_ref, prev_ref, next_ref, small_ref, convw_ref, convb_ref, dtb_ref, alog_ref, dskip_ref,
         out_ref, xbc_out_ref, state_ref) = refs[:12]
        ext_refs = refs[12:]
    c = pl.program_id(0)
    tile = (nt - 1 - c) if rev else c
    nb = small_ref.shape[0]
    valid = _valid_rows(tile, TILE)

    a = -jnp.exp(alog_ref[...])
    dts = [jnp.where(valid, _softplus(small_ref[b] + dtb_ref[...]), 0.0) for b in range(nb)]
    cum_all = _dot_split_rhs(_tri(TILE, rev).astype(bf16),
                             jnp.concatenate([dt * a for dt in dts], axis=1))
    for b in range(nb):
        dt = dts[b]
        if rev:
            xbc = xbc_ref[b]
            xs = xbc[:, :SSD_INNER].astype(f32)
            bc = xbc[:, SSD_INNER:]
        else:
            ext_ref = ext_refs[b]
            ext_ref[0:CONV_HALO, :] = jnp.where(tile > 0, prev_ref[b].astype(f32), 0.0)
            ext_ref[CONV_HALO:CONV_HALO + TILE, :] = jnp.where(valid, cur_ref[b].astype(f32), 0.0)
            ext_ref[CONV_HALO + TILE:, :] = jnp.where(tile < nt - 1, next_ref[b].astype(f32), 0.0)
            first = CONV_HALO - (SSD_CONV - 1) // 2
            strips = []
            for lo in range(0, SSD_CONV_DIM, 128):
                cols = slice(lo, lo + 128)
                acc = jnp.zeros((TILE, 128), f32) + convb_ref[:, cols]
                for k in range(SSD_CONV):
                    acc = acc + convw_ref[k:k + 1, cols] * ext_ref[first + k:first + k + TILE, cols]
                strips.append(jnp.where(valid, _silu(acc), 0.0))
            xbc = jnp.concatenate(strips, axis=1)
            xbc_out_ref[b] = xbc.astype(bf16)
            xs = xbc[:, :SSD_INNER]
            bc = xbc[:, SSD_INNER:].astype(bf16)
        st = jnp.where(c == 0, 0.0, state_ref[b])
        y, st = _ssd_scan_tile(rev, xs, bc[:, :2 * SSD_STATE], bc[:, 2 * SSD_STATE:], dt,
                               cum_all[:, 128 * b:128 * (b + 1)],
                               jnp.sum(dt * a, axis=0, keepdims=True), st)
        state_ref[b] = st
        if rev:
            y = (yf_ref[b] + y) * _silu(z_ref[b].astype(f32))
            out_ref[b] = jnp.where(valid, _rms(y, normw_ref[...]), 0.0).astype(out_ref.dtype)
        else:
            out_ref[b] = y + dskip_ref[...] * xs


def _ssd(rev, o_ssd, o_small, convw, convb, dtb, alog, extra, bsz, lp):
    n = o_ssd.shape[0]
    nt = lp // TILE
    hb = TILE // CONV_HALO
    o_ssd = o_ssd.reshape(bsz, lp, -1)
    o_small = o_small.reshape(bsz, lp, -1)

    def tile_of(c):
        return (nt - 1 - c) if rev else c

    cur = lambda c: (0, tile_of(c), 0)
    prev = lambda c: (0, jnp.maximum(tile_of(c) * hb - 1, 0), 0)
    nxt = lambda c: (0, jnp.minimum((tile_of(c) + 1) * hb, lp // CONV_HALO - 1), 0)
    zcol = lambda c: (0, tile_of(c), SSD_CONV_DIM // SSD_INNER)
    const = lambda c: (0, 0)
    tile3 = lambda width: pl.BlockSpec((bsz, TILE, width), cur)
    state = pltpu.VMEM((bsz, 2 * SSD_STATE, SSD_INNER), f32)
    if rev:
        yf, xbc, normw = extra
        out = pl.pallas_call(
            functools.partial(_ssd_body, rev, nt),
            grid=(nt,),
            in_specs=[tile3(SSD_CONV_DIM), tile3(128), pl.BlockSpec((1, 128), const),
                      pl.BlockSpec((1, 128), const), tile3(SSD_INNER),
                      pl.BlockSpec((bsz, TILE, SSD_INNER), zcol), pl.BlockSpec((1, SSD_INNER), const)],
            out_specs=tile3(SSD_INNER),
            out_shape=jax.ShapeDtypeStruct((bsz, lp, SSD_INNER), bf16),
            scratch_shapes=[state],
            compiler_params=_cparams("arbitrary"),
            name="ssd_rev",
        )(xbc, o_small, dtb, alog, yf, o_ssd, normw)
        return out.reshape(n, SSD_INNER)
    halo = lambda im: pl.BlockSpec((bsz, CONV_HALO, SSD_CONV_DIM), im)
    return pl.pallas_call(
        functools.partial(_ssd_body, rev, nt),
        grid=(nt,),
        in_specs=[tile3(SSD_CONV_DIM), halo(prev), halo(nxt), tile3(128),
                  pl.BlockSpec((8, SSD_CONV_DIM), const), pl.BlockSpec((1, SSD_CONV_DIM), const),
                  pl.BlockSpec((1, 128), const), pl.BlockSpec((1, 128), const),
                  pl.BlockSpec((1, SSD_INNER), const)],
        out_specs=[tile3(SSD_INNER), tile3(SSD_CONV_DIM)],
        out_shape=[jax.ShapeDtypeStruct((bsz, lp, SSD_INNER), f32),
                   jax.ShapeDtypeStruct((bsz, lp, SSD_CONV_DIM), bf16)],
        scratch_shapes=[state] + _per_batch_scratch(bsz, ((TILE + 2 * CONV_HALO, SSD_CONV_DIM), f32)),
        compiler_params=_cparams("arbitrary"),
        name="ssd_fwd",
    )(o_ssd, o_ssd, o_ssd, o_small, convw, convb, dtb, alog, extra)


def _gla_body(rev, nt, *refs):
    if rev:
        x_ref, small_ref, wg_ref, gb_ref, of_ref, normw_ref, gavg_ref, out_ref, st_ref = refs
    else:
        x_ref, small_ref, wg_ref, gb_ref, out_ref, st_ref = refs
    c = pl.program_id(0)
    tile = (nt - 1 - c) if rev else c
    nb = x_ref.shape[0]
    n = GLA_CHUNK
    valid = _valid_rows(tile, TILE)

    pre = _dot(small_ref[...].reshape(nb * TILE, 128).astype(bf16), wg_ref[...]) + gb_ref[...]
    g_all = _log_sigmoid(pre) / GLA_GATE_NORM
    gs = [jnp.where(valid, g_all[TILE * b:TILE * (b + 1)], 0.0) for b in range(nb)]
    row = lax.broadcasted_iota(jnp.int32, (TILE, TILE), 0)
    col = lax.broadcasted_iota(jnp.int32, (TILE, TILE), 1)
    same_chunk = (row // n) == (col // n)
    cum_mat = jnp.where(jnp.logical_and(same_chunk, (row <= col) if rev else (row >= col)), 1.0, 0.0)
    bc_all = _dot_split_rhs(cum_mat.astype(bf16), jnp.concatenate(gs, axis=1))

    first = slice(n, 2 * n) if rev else slice(0, n)
    second = slice(0, n) if rev else slice(n, 2 * n)
    rows = lax.broadcasted_iota(jnp.int32, (TILE, 1), 0)
    in_first = (rows >= n) if rev else (rows < n)
    tri = _tri(n, rev)
    tri4 = jnp.concatenate([tri] * GLA_HEADS, axis=0)
    lane_head = lax.broadcasted_iota(jnp.int32, (n, GLA_KEY), 1) // GLA_DK
    out_head = lax.broadcasted_iota(jnp.int32, (n, GLA_VAL), 1) // GLA_DV
    blockdiag = (lax.broadcasted_iota(jnp.int32, (GLA_VAL, GLA_KEY), 0) // GLA_DV
                 == lax.broadcasted_iota(jnp.int32, (GLA_VAL, GLA_KEY), 1) // GLA_DK)

    def stack_heads(a):
        return jnp.concatenate([jnp.where(lane_head == hh, a, 0.0) for hh in range(GLA_HEADS)],
                               axis=0).astype(bf16)

    def own_head(r):
        out = jnp.zeros((n, GLA_VAL), f32)
        for hh in range(GLA_HEADS):
            out = jnp.where(out_head == hh, r[n * hh:n * (hh + 1)], out)
        return out

    both = []
    for b in range(nb):
        x = x_ref[b].astype(f32)
        g = gs[b]
        bc = bc_all[:, 128 * b:128 * (b + 1)]
        q = jnp.where(valid, x[:, :GLA_KEY], 0.0) * GLA_DK ** -0.5
        k = jnp.where(valid, x[:, GLA_KEY:2 * GLA_KEY], 0.0)
        v = jnp.where(valid, x[:, 2 * GLA_KEY:2 * GLA_KEY + GLA_VAL], 0.0).astype(bf16)
        bl_first = jnp.sum(g[first], axis=0, keepdims=True)
        bl_second = jnp.sum(g[second], axis=0, keepdims=True)
        qt = q * jnp.exp(bc)
        kt = k * jnp.exp(-bc)
        kw = k * jnp.exp(jnp.where(in_first, bl_first, bl_second) - bc)

        att_f = jnp.where(tri4, _dot_nt(stack_heads(qt[first]), kt[first].astype(bf16)), 0.0)
        o_f = own_head(_dot(att_f.astype(bf16), v[first]))
        keys = jnp.concatenate([kw[first], kt[second]], axis=0).astype(bf16)
        vals = jnp.concatenate([v[first], v[second]], axis=0)
        att_s = _dot_nt(stack_heads(qt[second]), keys)
        att_s = jnp.concatenate([att_s[:, :n], jnp.where(tri4, att_s[:, n:], 0.0)], axis=1)
        o_s = own_head(_dot(att_s.astype(bf16), vals))

        st = jnp.where(c == 0, 0.0, st_ref[b])
        q_in = qt * jnp.exp(jnp.where(in_first, 0.0, bl_first))
        o = _dot_nt(q_in.astype(bf16), st.astype(bf16)) + jnp.concatenate(
            [o_s, o_f] if rev else [o_f, o_s], axis=0)
        k_out = (kw * jnp.exp(jnp.where(in_first, bl_second, 0.0))).astype(bf16)
        st_ref[b] = st * jnp.exp(bl_first + bl_second) + jnp.where(blockdiag, _dot_tn(v, k_out), 0.0)

        if rev:
            both.append(of_ref[b] + o)
        else:
            out_ref[b] = o

    if rev:
        o = jnp.concatenate(both, axis=0)
        ms = _dot_split_lhs(o * o, gavg_ref[...])
        on = o * lax.rsqrt(ms + EPS) * normw_ref[...]
        for b in range(nb):
            r = x_ref[b, :, 2 * GLA_KEY + GLA_VAL:].astype(f32)
            out_ref[b] = jnp.where(valid, on[TILE * b:TILE * (b + 1)] * _silu(r), 0.0).astype(out_ref.dtype)


def _gla(rev, o_gla, o_small, wg, gb, extra, bsz, lp):
    n = o_gla.shape[0]
    nt = lp // TILE
    cur = lambda c: (0, (nt - 1 - c) if rev else c, 0)
    const = lambda c: (0, 0)
    in_specs = [pl.BlockSpec((bsz, TILE, 768), cur), pl.BlockSpec((bsz, TILE, 128), cur),
                pl.BlockSpec((128, GLA_KEY), const), pl.BlockSpec((1, GLA_KEY), const)]
    args = [o_gla.reshape(bsz, lp, -1), o_small.reshape(bsz, lp, -1), wg, gb]
    if rev:
        of, normw, gavg = extra
        in_specs += [pl.BlockSpec((bsz, TILE, GLA_VAL), cur), pl.BlockSpec((1, GLA_VAL), const),
                     pl.BlockSpec((GLA_VAL, GLA_VAL), const)]
        args += [of.reshape(bsz, lp, -1), normw, gavg]
    out = pl.pallas_call(
        functools.partial(_gla_body, rev, nt),
        grid=(nt,),
        in_specs=in_specs,
        out_specs=pl.BlockSpec((bsz, TILE, GLA_VAL), cur),
        out_shape=jax.ShapeDtypeStruct((bsz, lp, GLA_VAL), bf16 if rev else f32),
        scratch_shapes=[pltpu.VMEM((bsz, GLA_VAL, GLA_KEY), f32)],
        compiler_params=_cparams("arbitrary"),
        name="gla_rev" if rev else "gla_fwd",
    )(*args)
    return out.reshape(n, GLA_VAL)


def _swa_body(nt, qp_ref, kp_ref, kc_ref, kn_ref, km_ref, vp_ref, vc_ref, vn_ref, vm_ref,
              bias_ref, bmeta_ref, sink_ref, out_ref):
    c = pl.program_id(0)
    krow = lax.broadcasted_iota(jnp.int32, (3 * TILE, 1), 0)
    pen = jnp.where(krow < TILE, jnp.where(c >= 2, 0.0, NEG),
                    jnp.where(krow < 2 * TILE, jnp.where(c >= 1, 0.0, NEG),
                              jnp.where(c <= nt - 2, 0.0, NEG)))
    qp = qp_ref[...]
    kcat = jnp.concatenate([kp_ref[...], kc_ref[...], kn_ref[...]], axis=0)
    s = _dot(kcat, qp) + bias_ref[...] + pen
    sm = _dot(km_ref[...], qp) + bmeta_ref[...]
    sk = sink_ref[...]
    m = jnp.maximum(jnp.maximum(jnp.max(s, axis=0, keepdims=True),
                                jnp.max(sm, axis=0, keepdims=True)), sk)
    p = jnp.exp2(s - m)
    pm = jnp.exp2(sm - m)
    inv = 1.0 / (jnp.sum(p, axis=0, keepdims=True) + jnp.sum(pm, axis=0, keepdims=True)
                 + jnp.exp2(sk - m))
    pb = p.astype(bf16)
    pmb = jnp.concatenate([jnp.zeros((TILE - N_META, 4 * TILE), bf16), pm.astype(bf16)], axis=0)
    vcat = jnp.concatenate([vp_ref[...], vc_ref[...], vn_ref[...]], axis=1)
    heads = []
    for g in range(2):
        rows = slice(HEAD_DIM * g, HEAD_DIM * (g + 1))
        cols = slice(2 * TILE * g, 2 * TILE * (g + 1))
        pv = (_dot(vcat[rows, :], pb[:, cols]) + _dot(vm_ref[rows, :], pmb[:, cols])) * inv[:, cols]
        heads += [pv[:, :TILE], pv[:, TILE:]]
    o = jnp.concatenate(heads, axis=0).T
    out_ref[...] = jnp.where(_valid_rows(c, TILE), o, 0.0).astype(out_ref.dtype)


def _swa(qp, k, vt, bias_band, bias_meta, sink, bsz, lp):
    n = k.shape[0]
    nt = lp // TILE
    per, _, tm = vt.shape[0] // bsz, vt.shape[1], vt.shape[2]
    sub = tm // TILE
    qp = qp.reshape(bsz, nt, 128, 4 * TILE)
    k = k.reshape(bsz, lp, 128)
    vt = vt.reshape(bsz, per, 128, tm)
    prev = lambda c: jnp.maximum(c - 1, 0)
    nxt = lambda c: jnp.minimum(c + 1, nt - 1)
    kspec = lambda tile: pl.BlockSpec((bsz, TILE, 128), lambda c: (0, tile(c), 0))
    vspec = lambda tile: pl.BlockSpec((bsz, None, 128, TILE), lambda c: (0, tile(c) // sub, 0, tile(c) % sub))
    same = lambda c: c
    first = lambda c: 0
    out = pl.pallas_call(
        _for_each_batch(functools.partial(_swa_body, nt), [True] * 9 + [False] * 3 + [True], 0),
        grid=(nt,),
        in_specs=[pl.BlockSpec((bsz, None, 128, 4 * TILE), lambda c: (0, c, 0, 0)),
                  kspec(prev), kspec(same), kspec(nxt),
                  pl.BlockSpec((bsz, N_META, 128), lambda c: (0, TILE // N_META - 1, 0)),
                  vspec(prev), vspec(same), vspec(nxt), vspec(first),
                  pl.BlockSpec((3 * TILE, 4 * TILE), lambda c: (0, 0)),
                  pl.BlockSpec((None, N_META, 4 * TILE), lambda c: (c, 0, 0)),
                  pl.BlockSpec((1, 4 * TILE), lambda c: (0, 0))],
        out_specs=pl.BlockSpec((bsz, TILE, 256), lambda c: (0, c, 0)),
        out_shape=jax.ShapeDtypeStruct((bsz, lp, 256), bf16),
        compiler_params=_cparams("arbitrary"),
        name="swa",
    )(qp, k, k, k, k, vt, vt, vt, vt, bias_band, bias_meta, sink)
    return out.reshape(n, 256)


def _flash_body(nk, tq, tk, qt_ref, k_ref, vt_ref, out_ref, qpad_ref, m_ref, l_ref, acc_ref,
                sa_ref, sb_ref, ca_ref, cb_ref):
    i = pl.program_id(1)
    buf_a, buf_b = (sa_ref, ca_ref), (sb_ref, cb_ref)
    krow = lax.broadcasted_iota(jnp.int32, (tk, 1), 0)
    qpad_ref[...] = jnp.zeros_like(qpad_ref)
    for hh in range(4):
        lo = HEAD_DIM * (hh // 2)
        qpad_ref[lo:lo + HEAD_DIM, tq * hh:tq * (hh + 1)] = qt_ref[HEAD_DIM * hh:HEAD_DIM * (hh + 1), :]
    m_ref[...] = jnp.full_like(m_ref, NEG)
    l_ref[...] = jnp.zeros_like(l_ref)
    acc_ref[...] = jnp.zeros_like(acc_ref)

    def scores(j, buf, first=False):
        s_ref, cmax_ref = buf
        s = _dot(k_ref[pl.ds(pl.multiple_of(j * tk, tk), tk), :], qpad_ref[...])
        if first:
            s = jnp.where(krow >= TILE - N_META, s, NEG)
        s_ref[...] = s
        cmax_ref[...] = jnp.max(s, axis=0, keepdims=True)

    def absorb(j, buf):
        s_ref, cmax_ref = buf
        s = s_ref[...]
        m_old = m_ref[...]
        m_new = jnp.maximum(m_old, cmax_ref[...])
        alpha = jnp.exp2(m_old - m_new)
        p = jnp.exp2(s - m_new)
        l_ref[...] = alpha * l_ref[...] + jnp.sum(p, axis=0, keepdims=True)
        m_ref[...] = m_new
        pb = p.astype(bf16)
        for g in range(2):
            pv = _dot(vt_ref[j, HEAD_DIM * g:HEAD_DIM * (g + 1), :], pb[:, 2 * g * tq:(2 * g + 2) * tq])
            for r in range(2):
                hh = 2 * g + r
                rows = slice(HEAD_DIM * hh, HEAD_DIM * (hh + 1))
                acc_ref[rows, :] = (alpha[:, tq * hh:tq * (hh + 1)] * acc_ref[rows, :]
                                    + pv[:, tq * r:tq * (r + 1)])

    scores(0, buf_a, first=True)

    def body(jj, carry):
        j = 2 * jj
        scores(j + 1, buf_b)
        absorb(j, buf_a)
        scores(j + 2, buf_a)
        absorb(j + 1, buf_b)
        return carry

    lax.fori_loop(0, (nk - 1) // 2, body, 0, unroll=True)
    if nk % 2 == 1:
        absorb(nk - 1, buf_a)
    else:
        scores(nk - 1, buf_b)
        absorb(nk - 2, buf_a)
        absorb(nk - 1, buf_b)
    linv = 1.0 / l_ref[...]
    for hh in range(4):
        rows = slice(HEAD_DIM * hh, HEAD_DIM * (hh + 1))
        acc_ref[rows, :] = acc_ref[rows, :] * linv[:, tq * hh:tq * (hh + 1)]
    rows = lax.broadcasted_iota(jnp.int32, (tq, 1), 0)
    valid = jnp.logical_or(i > 0, rows >= TILE - N_META)
    out_ref[...] = jnp.where(valid, acc_ref[...].T, 0.0).astype(out_ref.dtype)


def _flash(qt, k, vt, bsz, lp):
    n = k.shape[0]
    tk = qt.shape[2]
    tq = FLASH_TQ if tk % FLASH_TQ == 0 else tk
    sub = tk // tq
    per = lp // tq
    nk = lp // tk
    return pl.pallas_call(
        functools.partial(_flash_body, nk, tq, tk),
        grid=(bsz, per),
        in_specs=[pl.BlockSpec((None, 256, tq), lambda b, i: (b * nk + i // sub, 0, i % sub)),
                  pl.BlockSpec((lp, 128), lambda b, i: (b, 0)),
                  pl.BlockSpec((nk, 128, tk), lambda b, i: (b, 0, 0))],
        out_specs=pl.BlockSpec((tq, 256), lambda b, i: (b * per + i, 0)),
        out_shape=jax.ShapeDtypeStruct((n, 256), bf16),
        scratch_shapes=[pltpu.VMEM((128, 4 * tq), bf16), pltpu.VMEM((1, 4 * tq), f32),
                        pltpu.VMEM((1, 4 * tq), f32), pltpu.VMEM((256, tq), f32),
                        pltpu.VMEM((tk, 4 * tq), f32), pltpu.VMEM((tk, 4 * tq), f32),
                        pltpu.VMEM((1, 4 * tq), f32), pltpu.VMEM((1, 4 * tq), f32)],
        compiler_params=_cparams("parallel", "parallel"),
        name="gqa_full",
    )(qt, k, vt)


def _mix_residual(y_refs, h_ref, wo_ref):
    mixed = jnp.concatenate([y_ref[...] for y_ref in y_refs], axis=1)
    return h_ref[...] + _dot(mixed, wo_ref[...])


def _ffn_body(y0_ref, y1_ref, y2_ref, y3_ref, h_ref, wo_ref, nw_ref, wg_ref, wu_ref, wd_ref,
              out_ref, hn_ref, u_ref, acc_ref):
    j = pl.program_id(1)

    @pl.when(j == 0)
    def _():
        hn = _mix_residual((y0_ref, y1_ref, y2_ref, y3_ref), h_ref, wo_ref)
        hn_ref[...] = hn
        u_ref[...] = _rms(hn, nw_ref[...]).astype(bf16)
        acc_ref[...] = jnp.zeros_like(acc_ref)

    u = u_ref[...]
    t = _silu(_dot(u, wg_ref[...])) * _dot(u, wu_ref[...])
    acc_ref[...] += _dot(t.astype(bf16), wd_ref[...])

    @pl.when(j == pl.num_programs(1) - 1)
    def _():
        out_ref[...] = hn_ref[...] + acc_ref[...]


def _ffn(ys, h, wo, nw, wg, wu, wd):
    n, d = h.shape
    ff = wg.shape[1]
    tm = TILE * _largest_divisor(n // TILE, 6)
    tf = 128 * _largest_divisor(ff // 128, 11)
    row = lambda i, j: (i, 0)
    const = lambda i, j: (0, 0)
    return pl.pallas_call(
        _ffn_body,
        grid=(n // tm, ff // tf),
        in_specs=[pl.BlockSpec((tm, 256), row)] * 4
        + [pl.BlockSpec((tm, d), row), pl.BlockSpec((d, d), const), pl.BlockSpec((1, d), const),
           pl.BlockSpec((d, tf), lambda i, j: (0, j)), pl.BlockSpec((d, tf), lambda i, j: (0, j)),
           pl.BlockSpec((tf, d), lambda i, j: (j, 0))],
        out_specs=pl.BlockSpec((tm, d), row),
        out_shape=jax.ShapeDtypeStruct((n, d), f32),
        scratch_shapes=[pltpu.VMEM((tm, d), f32), pltpu.VMEM((tm, d), bf16), pltpu.VMEM((tm, d), f32)],
        compiler_params=_cparams("parallel", "arbitrary"),
        name="ffn",
    )(*ys, h, wo, nw, wg, wu, wd)


MOE_TM = 1024
ROUTE_E, ROUTE_RANK, ROUTE_GATE, ROUTE_W = 0, 2, 4, 8


def _router_body(y0_ref, y1_ref, y2_ref, y3_ref, h_ref, wo_ref, nw_ref, r_ref, strict_ref,
                 hn_ref, route_ref, cnt_ref, base_ref):
    @pl.when(pl.program_id(0) == 0)
    def _():
        base_ref[...] = jnp.zeros_like(base_ref)

    hn = _mix_residual((y0_ref, y1_ref, y2_ref, y3_ref), h_ref, wo_ref)
    hn_ref[...] = hn
    u = _rms(hn, nw_ref[...])
    u_hi, u_lo = _split(u)
    r_hi, r_lo = _split(r_ref[...])
    logits = _dot(u_hi, r_hi) + _dot(u_lo, r_hi) + _dot(u_hi, r_lo)
    tm = logits.shape[0]
    lane = lax.broadcasted_iota(jnp.int32, logits.shape, 1)
    logits = jnp.where(lane < N_EXPERTS, logits, NEG)
    m1 = jnp.max(logits, axis=-1, keepdims=True)
    i1 = jnp.min(jnp.where(logits == m1, lane, 128), axis=-1, keepdims=True)
    rest = jnp.where(lane == i1, NEG, logits)
    m2 = jnp.max(rest, axis=-1, keepdims=True)
    i2 = jnp.min(jnp.where(rest == m2, lane, 128), axis=-1, keepdims=True)
    e2 = jnp.exp(m2 - m1)
    g1 = 1.0 / (1.0 + e2)
    g2 = e2 * g1

    sel1 = lane == i1
    sel2 = lane == i2
    onehot = jnp.where(sel1, 1.0, jnp.where(sel2, 1.0, 0.0))
    before = _dot(strict_ref[...], onehot.astype(bf16)) + base_ref[0:1, :]
    r1 = jnp.sum(jnp.where(sel1, before, 0.0), axis=-1, keepdims=True)
    r2 = jnp.sum(jnp.where(sel2, before, 0.0), axis=-1, keepdims=True)
    route = jnp.zeros(logits.shape, f32)
    for k, val in enumerate((i1.astype(f32), i2.astype(f32), r1, r2, g1, g2)):
        route = jnp.where(lane == k, val, route)
    route_ref[...] = route[:, :ROUTE_W]
    base_ref[0:1, :] = base_ref[0:1, :] + jnp.sum(onehot, axis=0, keepdims=True)
    cnt_ref[...] = base_ref[...]


def _router(ys, h, wo, nw, router):
    n, d = h.shape
    tm = TILE * _largest_divisor(n // TILE, 8)
    row = lambda i: (i, 0)
    const = lambda i: (0, 0)
    strict = jnp.asarray(np.tril(np.ones((tm, tm), np.float32), -1), bf16)
    return pl.pallas_call(
        _router_body,
        grid=(n // tm,),
        in_specs=[pl.BlockSpec((tm, 256), row)] * 4
        + [pl.BlockSpec((tm, d), row), pl.BlockSpec((d, d), const), pl.BlockSpec((1, d), const),
           pl.BlockSpec((d, 128), const), pl.BlockSpec((tm, tm), const)],
        out_specs=[pl.BlockSpec((tm, d), row), pl.BlockSpec((tm, ROUTE_W), row), pl.BlockSpec((8, 128), const)],
        out_shape=[jax.ShapeDtypeStruct((n, d), f32), jax.ShapeDtypeStruct((n, ROUTE_W), f32),
                   jax.ShapeDtypeStruct((8, 128), f32)],
        scratch_shapes=[pltpu.VMEM((8, 128), f32)],
        compiler_params=_cparams("arbitrary"),
        name="moe_router",
    )(*ys, h, wo, nw, router, strict)


LANES = 128
ROW_DMA_UNROLL = 8


def _row_copy(src_ref, src_row, dst_ref, dst_row, sem):
    return pltpu.make_async_copy(src_ref.at[src_row], dst_ref.at[dst_row], sem)


def _to_slabs(dst_ref, val):
    dst_ref[...] = val.reshape(dst_ref.shape)


def _from_slabs(src_ref):
    rows, chunks, lanes = src_ref.shape
    return src_ref[...].reshape(rows, chunks * lanes)


def _dispatch_body(fill_ref, h_ref, nw_ref, dest_ref, xs_ref, u_ref, idx_ref, zero_ref,
                   sem_idx, sem_row, sem_fill):
    i = pl.program_id(0)
    tt = u_ref.shape[1]

    @pl.when(i == 0)
    def _():
        zero_ref[...] = jnp.zeros_like(zero_ref)
        fills = [pltpu.make_async_copy(zero_ref, xs_ref.at[pl.ds(fill_ref[e], MOE_TM)], sem_fill)
                 for e in range(N_EXPERTS)]
        for cp in fills:
            cp.start()
        for cp in fills:
            cp.wait()
        last = xs_ref.shape[0] // MOE_TM - 1
        for j in range(last - N_EXPERTS, last + 1):
            @pl.when(j >= fill_ref[N_EXPERTS])
            def _():
                cp = pltpu.make_async_copy(zero_ref, xs_ref.at[pl.ds(j * MOE_TM, MOE_TM)], sem_fill)
                cp.start()
                cp.wait()

    slot = i % 2
    rows_ref = u_ref.at[slot]
    sem = sem_row.at[slot]
    idx_copy = pltpu.make_async_copy(dest_ref.at[pl.ds(i * 2 * tt, 2 * tt)], idx_ref, sem_idx)
    idx_copy.start()
    _to_slabs(rows_ref, _rms(h_ref[...], nw_ref[...]))
    idx_copy.wait()

    def issue(r, carry):
        _row_copy(rows_ref, r, xs_ref, idx_ref[r], sem).start(priority=0)
        _row_copy(rows_ref, r, xs_ref, idx_ref[tt + r], sem).start(priority=1)
        return carry

    lax.fori_loop(0, tt, issue, 0, unroll=ROW_DMA_UNROLL)

    def drain(s):
        for _ in range(2):
            pltpu.make_async_copy(u_ref.at[s], xs_ref.at[pl.ds(0, tt)], sem_row.at[s]).wait()

    @pl.when(i > 0)
    def _():
        drain(1 - slot)

    @pl.when(i == pl.num_programs(0) - 1)
    def _():
        drain(slot)


def _dispatch(h, nw, dest_flat, fill_rows, rows, tt):
    n, d = h.shape
    return pl.pallas_call(
        _dispatch_body,
        grid_spec=pltpu.PrefetchScalarGridSpec(
            num_scalar_prefetch=1,
            grid=(n // tt,),
            in_specs=[pl.BlockSpec((tt, d), lambda i, fr: (i, 0)), pl.BlockSpec((1, d), lambda i, fr: (0, 0)),
                      pl.BlockSpec(memory_space=pl.ANY)],
            out_specs=pl.BlockSpec(memory_space=pl.ANY),
            scratch_shapes=[pltpu.VMEM((2, tt, d // LANES, LANES), f32), pltpu.SMEM((2 * tt,), jnp.int32),
                            pltpu.VMEM((MOE_TM, d // LANES, LANES), f32), pltpu.SemaphoreType.DMA(()),
                            pltpu.SemaphoreType.DMA((2,)), pltpu.SemaphoreType.DMA(())]),
        out_shape=jax.ShapeDtypeStruct((rows, d // LANES, LANES), f32),
        compiler_params=_cparams("arbitrary"),
        name="moe_dispatch",
    )(fill_rows, h, nw, dest_flat)


def _experts_body(te_ref, nu_ref, x_ref, wg_ref, wu_ref, wd_ref, y_ref, xb_ref, acc_ref):
    j = pl.program_id(0)
    f = pl.program_id(1)

    @pl.when(j < nu_ref[0])
    def _():
        @pl.when(f == 0)
        def _():
            xb_ref[...] = _from_slabs(x_ref).astype(bf16)
            acc_ref[...] = jnp.zeros_like(acc_ref)

        x = xb_ref[...]
        t = _silu(_dot(x, wg_ref[...])) * _dot(x, wu_ref[...])
        acc_ref[...] += _dot(t.astype(bf16), wd_ref[...])

        @pl.when(f == pl.num_programs(1) - 1)
        def _():
            _to_slabs(y_ref, acc_ref[...])

    @pl.when(jnp.logical_and(j >= nu_ref[0], f == pl.num_programs(1) - 1))
    def _():
        y_ref[...] = jnp.zeros_like(y_ref)


def _experts(xs, tile_expert, n_used, wg, wu, wd, n_tiles):
    d = wg.shape[1]
    slab = (MOE_TM, d // LANES, LANES)
    ff = wg.shape[2]
    tf = 256 * _largest_divisor(ff // 256, 2)
    nf = ff // tf
    tile = lambda j, nu: jnp.minimum(j, nu[0] - 1)
    chunk = lambda j, f, nu: jnp.where(j < nu[0], f, nf - 1)
    return pl.pallas_call(
        _experts_body,
        grid_spec=pltpu.PrefetchScalarGridSpec(
            num_scalar_prefetch=2,
            grid=(n_tiles, nf),
            in_specs=[pl.BlockSpec(slab, lambda j, f, te, nu: (tile(j, nu), 0, 0)),
                      pl.BlockSpec((None, d, tf), lambda j, f, te, nu: (te[tile(j, nu)], 0, chunk(j, f, nu))),
                      pl.BlockSpec((None, d, tf), lambda j, f, te, nu: (te[tile(j, nu)], 0, chunk(j, f, nu))),
                      pl.BlockSpec((None, tf, d), lambda j, f, te, nu: (te[tile(j, nu)], chunk(j, f, nu), 0))],
            out_specs=pl.BlockSpec(slab, lambda j, f, te, nu: (j, 0, 0)),
            scratch_shapes=[pltpu.VMEM((MOE_TM, d), bf16), pltpu.VMEM((MOE_TM, d), f32)]),
        out_shape=jax.ShapeDtypeStruct((n_tiles * MOE_TM, d // LANES, LANES), f32),
        compiler_params=_cparams("arbitrary", "arbitrary"),
        name="moe_experts",
    )(tile_expert, n_used, xs, wg, wu, wd)


def _combine_body(final, h_ref, route_ref, fnw_ref, dest_ref, ys_ref, out_ref, buf_ref, idx_ref,
                  sem_idx, sem_row):
    i = pl.program_id(0)
    tt = h_ref.shape[0]
    slot = i % 2

    def gather(step, s):
        idx_copy = pltpu.make_async_copy(dest_ref.at[pl.ds(step * 2 * tt, 2 * tt)], idx_ref, sem_idx)
        idx_copy.start()
        idx_copy.wait()

        def issue(r, carry):
            _row_copy(ys_ref, idx_ref[r], buf_ref.at[s, 0], r, sem_row.at[s]).start(priority=0)
            _row_copy(ys_ref, idx_ref[tt + r], buf_ref.at[s, 1], r, sem_row.at[s]).start(priority=1)
            return carry

        lax.fori_loop(0, tt, issue, 0, unroll=ROW_DMA_UNROLL)

    @pl.when(i == 0)
    def _():
        gather(0, 0)

    @pl.when(i + 1 < pl.num_programs(0))
    def _():
        gather(i + 1, 1 - slot)

    for k in range(2):
        pltpu.make_async_copy(ys_ref.at[pl.ds(0, tt)], buf_ref.at[slot, k], sem_row.at[slot]).wait()
    route = route_ref[...]
    g1 = route[:, ROUTE_GATE:ROUTE_GATE + 1]
    g2 = route[:, ROUTE_GATE + 1:ROUTE_GATE + 2]
    y = h_ref[...] + g1 * _from_slabs(buf_ref.at[slot, 0]) + g2 * _from_slabs(buf_ref.at[slot, 1])
    out_ref[...] = _rms(y, fnw_ref[...]) if final else y


def _combine(h, route, fnw, dest_flat, ys, tt, final):
    n, d = h.shape
    return pl.pallas_call(
        functools.partial(_combine_body, final),
        grid=(n // tt,),
        in_specs=[pl.BlockSpec((tt, d), lambda i: (i, 0)), pl.BlockSpec((tt, ROUTE_W), lambda i: (i, 0)),
                  pl.BlockSpec((1, d), lambda i: (0, 0)),
                  pl.BlockSpec(memory_space=pl.ANY), pl.BlockSpec(memory_space=pl.ANY)],
        out_specs=pl.BlockSpec((tt, d), lambda i: (i, 0)),
        out_shape=jax.ShapeDtypeStruct((n, d), f32),
        scratch_shapes=[pltpu.VMEM((2, 2, tt, d // LANES, LANES), f32), pltpu.SMEM((2 * tt,), jnp.int32),
                        pltpu.SemaphoreType.DMA(()), pltpu.SemaphoreType.DMA((2,))],
        compiler_params=_cparams("arbitrary"),
        name="moe_combine",
    )(h, route, fnw, dest_flat, ys)


def _moe(mix, h, wo, nw, router, wg, wu, wd, fnw, final):
    n, d = h.shape
    tt = TILE * _largest_divisor(n // TILE, 4)
    h, route, counts = _router(mix, h, wo, nw, router)

    cnt = counts[0, :N_EXPERTS].astype(jnp.int32)
    padded = (cnt + MOE_TM - 1) // MOE_TM * MOE_TM
    ends = jnp.cumsum(padded)
    off = ends - padded
    n_tiles = -(-2 * n // MOE_TM) + N_EXPERTS
    tile_expert = jnp.minimum(
        jnp.sum((jnp.arange(n_tiles)[:, None] * MOE_TM >= ends[None, :]).astype(jnp.int32), axis=1),
        N_EXPERTS - 1).astype(jnp.int32)
    n_used = (ends[-1:] // MOE_TM).astype(jnp.int32)
    sel = route[:, ROUTE_E:ROUTE_E + 2].astype(jnp.int32)
    rank = route[:, ROUTE_RANK:ROUTE_RANK + 2].astype(jnp.int32)
    dest = jnp.sum(jnp.where(sel[..., None] == jnp.arange(N_EXPERTS), off, 0), axis=-1) + rank
    dest_flat = dest.reshape(n // tt, tt, 2).transpose(0, 2, 1).reshape(-1)

    fill_rows = jnp.concatenate([off + cnt, n_used]).astype(jnp.int32)
    xs = _dispatch(h, nw, dest_flat, fill_rows, (n_tiles + 1) * MOE_TM, tt)
    ys = _experts(xs, tile_expert, n_used, wg, wu, wd, n_tiles)
    return _combine(h, route, fnw, dest_flat, ys, tt, final)


def _final_norm_body(h_ref, w_ref, out_ref):
    out_ref[...] = _rms(h_ref[...], w_ref[...])


def _final_norm(h, w):
    n, d = h.shape
    tm = TILE * _largest_divisor(n // TILE, 8)
    return pl.pallas_call(
        _final_norm_body,
        grid=(n // tm,),
        in_specs=[pl.BlockSpec((tm, d), lambda i: (i, 0)), pl.BlockSpec((1, d), lambda i: (0, 0))],
        out_specs=pl.BlockSpec((tm, d), lambda i: (i, 0)),
        out_shape=jax.ShapeDtypeStruct((n, d), f32),
        compiler_params=_cparams("parallel"),
        name="final_norm",
    )(h, w)


def _rope_tables(seq, pad):
    t = np.arange(seq)
    meta_pos = np.arange(N_META) - N_META
    row = np.concatenate([meta_pos, t // GRID_W]).astype(np.float32)
    col = np.concatenate([meta_pos, t % GRID_W]).astype(np.float32)
    half = HEAD_DIM // 2
    inv = np.float32(ROPE_THETA) ** (-np.arange(0, half, 2, dtype=np.float32) / np.float32(half))
    ang = np.concatenate([row[:, None] * inv, col[:, None] * inv], axis=-1)
    ang = np.tile(np.repeat(ang, 2, axis=-1), (1, QK_WIDTH // HEAD_DIM))
    ang = np.pad(ang, ((pad, 0), (0, 0))).astype(np.float32)
    return jnp.asarray(np.cos(ang)), jnp.asarray(np.sin(ang))


def _pair_swap_matrix(width):
    i = jnp.arange(width)
    p = jnp.zeros((width, width), f32)
    p = p.at[i[1::2], i[0::2]].set(-1.0)
    p = p.at[i[0::2], i[1::2]].set(1.0)
    return p.astype(bf16)


def _group_mean_matrix(width, group):
    i = jnp.arange(width)
    return ((i[:, None] // group == i[None, :] // group).astype(f32) / group).astype(bf16)


def _t5_bucket(rel):
    nb = REL_BUCKETS // 2
    max_exact = nb // 2
    ret = (rel > 0).astype(jnp.int32) * nb
    n = jnp.abs(rel)
    nf = jnp.maximum(n, 1).astype(f32)
    large = max_exact + (jnp.log(nf / max_exact) / math.log(REL_MAX_DIST / max_exact)
                         * (nb - max_exact)).astype(jnp.int32)
    large = jnp.minimum(large, nb - 1)
    return ret + jnp.where(n < max_exact, n, large)


def _swa_bias_tables(rel_bias, lp):
    def lookup(bucket):
        out = jnp.zeros((rel_bias.shape[1],) + bucket.shape, f32)
        for b in range(REL_BUCKETS):
            out = jnp.where((bucket == b)[None], rel_bias[b].astype(f32)[:, None, None], out)
        return out

    qi = jnp.arange(TILE)
    ki = jnp.arange(3 * TILE)
    rel = ki[None, :] - TILE - qi[:, None]
    band = jnp.where((jnp.abs(rel) <= WINDOW)[None], lookup(_t5_bucket(rel)), NEG)
    pos = jnp.arange(lp) - (TILE - N_META)
    rel_m = jnp.arange(N_META)[None, :] - pos[:, None]
    meta = lookup(_t5_bucket(rel_m))
    nt = lp // TILE
    band_t = jnp.transpose(band, (2, 0, 1)).reshape(3 * TILE, 4 * TILE) * LOG2E
    meta_t = (meta.reshape(4, nt, TILE, N_META).transpose(1, 3, 0, 2).reshape(nt, N_META, 4 * TILE)
              * LOG2E)
    return band_t, meta_t


def _row(v, width=None):
    v = v.astype(f32).reshape(1, -1)
    if width is not None and v.shape[1] < width:
        v = jnp.pad(v, ((0, 0), (0, width - v.shape[1])))
    return v


def kernel(x, meta_tokens, rel_bias, norm_mix_w, norm_ffn_w, w_in, ssd_conv_w, ssd_conv_b, ssd_dt_bias, ssd_a_log, ssd_d, ssd_norm_w, gla_gate_w2, gla_gate_b, gla_norm_w, swa_sink, gqa_q_norm_w, gqa_k_norm_w, w_out, ffn_w_gate, ffn_w_up, ffn_w_down, moe_router, moe_w_gate, moe_w_up, moe_w_down, final_norm_w):
    bsz, seq, d = x.shape
    depth = w_in.shape[0]
    pad = (-(seq + N_META)) % TILE
    assert pad == TILE - N_META and seq % TILE == 0
    lp = pad + N_META + seq
    n = bsz * lp

    meta = jnp.broadcast_to(meta_tokens[None].astype(x.dtype), (bsz, N_META, d))
    h = jnp.concatenate([jnp.zeros((bsz, pad, d), x.dtype), meta, x], axis=1).reshape(n, d)

    cos, sin = _rope_tables(seq, pad)
    rot = _pair_swap_matrix(QK_WIDTH)
    gavg = _group_mean_matrix(QK_WIDTH, HEAD_DIM)
    bias_band, bias_meta = _swa_bias_tables(rel_bias, lp)
    offs = [0]
    for s in IN_SIZES:
        offs.append(offs[-1] + s)

    for i in range(depth):
        wi = w_in[i]
        cols = [wi[:, offs[j]:offs[j + 1]] for j in PACK_ORDER]
        w_pack = jnp.concatenate(cols + [jnp.zeros((d, PACK_WIDTH - offs[-1]), wi.dtype)], axis=1).astype(bf16)
        o_ssd, o_small, o_gla, sk, ak, aq, av, sq, sv = _inproj(
            h, _row(norm_mix_w[i]), w_pack, cos, sin, rot, gavg,
            _row(jnp.concatenate([jnp.tile(gqa_q_norm_w[i], 4), jnp.tile(gqa_k_norm_w[i], 2)])), bsz, lp)

        convw = jnp.pad(ssd_conv_w[i].astype(f32), ((0, 8 - SSD_CONV), (0, 0)))
        convb = _row(ssd_conv_b[i])
        dtb = _row(ssd_dt_bias[i].reshape(-1), 128)
        alog = _row(ssd_a_log[i].reshape(-1), 128)
        yf, xbc = _ssd(False, o_ssd, o_small, convw, convb, dtb, alog,
                       _row(jnp.repeat(ssd_d[i], SSD_HEAD_DIM)), bsz, lp)
        y_ssd = _ssd(True, o_ssd, o_small, convw, convb, dtb, alog, (yf, xbc, _row(ssd_norm_w[i])), bsz, lp)

        def gate_w(direction):
            lo = SMALL_GA + GLA_GATE_RANK * direction
            full = jnp.zeros((128, GLA_KEY), f32).at[lo:lo + GLA_GATE_RANK].set(gla_gate_w2[i, direction].astype(f32))
            return full.astype(bf16)

        of = _gla(False, o_gla, o_small, gate_w(0), _row(gla_gate_b[i, 0]), None, bsz, lp)
        y_gla = _gla(True, o_gla, o_small, gate_w(1), _row(gla_gate_b[i, 1]),
                     (of, _row(jnp.tile(gla_norm_w[i], GLA_HEADS)), gavg), bsz, lp)

        sink = _row(jnp.repeat(swa_sink[i].astype(f32), TILE)) * LOG2E
        y_swa = _swa(sq, sk, sv, bias_band, bias_meta, sink, bsz, lp)
        y_g2 = _flash(aq, ak, av, bsz, lp)

        mix = (y_ssd, y_gla, y_swa, y_g2)
        wo = w_out[i].astype(bf16)
        j = i // 2
        if i % 2 == 0:
            h = _ffn(mix, h, wo, _row(norm_ffn_w[i]), ffn_w_gate[j].astype(bf16), ffn_w_up[j].astype(bf16),
                     ffn_w_down[j].astype(bf16))
            if i == depth - 1:
                h = _final_norm(h, _row(final_norm_w))
        else:
            router = jnp.pad(moe_router[j].astype(f32), ((0, 0), (0, 128 - N_EXPERTS)))
            h = _moe(mix, h, wo, _row(norm_ffn_w[i]), router, moe_w_gate[j].astype(bf16),
                     moe_w_up[j].astype(bf16), moe_w_down[j].astype(bf16), _row(final_norm_w), i == depth - 1)
    return h.reshape(bsz, lp, d)[:, pad + N_META:]
```
